```python
import jax, jax.numpy as jnp
from jax import lax
import numpy as np

D_MODEL = 1024
BATCH = 16
SEQ = 2048
DEPTH = 2
DEC_BATCH = 128
DEC_SEQ = 4
PAST_LEN = 16384
PAGE_SIZE = 128

RET_HEADS = 4
RET_QK_DIM = D_MODEL // RET_HEADS
RET_V_DIM = 2 * RET_QK_DIM
RET_QK_W = RET_HEADS * RET_QK_DIM
RET_V_W = RET_HEADS * RET_V_DIM
RET_CHUNK = 128
RET_ROPE_THETA = 10000.0
SWA_HEAD_DIM = 64
SWA_Q_HEADS = D_MODEL // SWA_HEAD_DIM
SWA_KV_HEADS = 4
SWA_GROUP = SWA_Q_HEADS // SWA_KV_HEADS
WINDOW = 128
SWA_BLOCK = WINDOW
ROPE_THETA = 500000.0
ROT_DIM = SWA_HEAD_DIM // 4
D_FF = 4 * D_MODEL
EPS = 1e-6
NEG = -1e30

kernel_name = "yoco_retention_swa_sink_decoder_step"


def rms_norm(x, g):
    xf = x.astype(jnp.float32)
    y = xf * lax.rsqrt(jnp.mean(xf * xf, axis=-1, keepdims=True) + EPS)
    return y.astype(x.dtype) * g.astype(x.dtype)


def rope(x, pos, inv_freq):
    half = inv_freq.shape[0]
    rot = 2 * half
    ang = pos[:, None] * inv_freq[None, :]
    cos = jnp.cos(ang)[None, :, None, :].astype(x.dtype)
    sin = jnp.sin(ang)[None, :, None, :].astype(x.dtype)
    x1 = x[..., :half]
    x2 = x[..., half:rot]
    return jnp.concatenate([x1 * cos - x2 * sin, x2 * cos + x1 * sin, x[..., rot:]], axis=-1)


def ret_inv_freq():
    return 1.0 / (RET_ROPE_THETA ** jnp.linspace(0.0, 1.0, RET_QK_DIM // 2, dtype=jnp.float32))


def partial_inv_freq():
    half = ROT_DIM // 2
    return ROPE_THETA ** (-jnp.arange(half, dtype=jnp.float32) / half)


def retention_chunkwise(q, k, v, state0, chunk):
    B, T, H, dk = q.shape
    n = T // chunk
    log_g = jnp.log1p(-jnp.exp2(-5.0 - jnp.arange(H, dtype=jnp.float32)))
    idx = jnp.arange(chunk, dtype=jnp.float32)
    rel = idx[:, None] - idx[None, :]
    intra = jnp.where(rel[None] >= 0, jnp.exp(log_g[:, None, None] * jnp.maximum(rel, 0.0)[None]), 0.0)
    q_decay = jnp.exp(log_g[:, None] * (idx[None, :] + 1.0))
    k_decay = jnp.exp(log_g[:, None] * (chunk - 1.0 - idx[None, :]))
    chunk_decay = jnp.exp(log_g * chunk)[None, :, None, None]

    def to_chunks(a):
        return jnp.moveaxis(a.reshape(B, n, chunk, H, a.shape[-1]), 1, 0)

    def step(S, qkv):
        qc, kc, vc = qkv
        sc = jnp.einsum('bihd,bjhd->bhij', qc, kc) * intra[None]
        o = jnp.einsum('bhij,bjhe->bihe', sc, vc)
        o = o + jnp.einsum('bihd,hi,bhde->bihe', qc, q_decay, S)
        S = chunk_decay * S + jnp.einsum('bjhd,hj,bjhe->bhde', kc, k_decay, vc)
        return S, o

    S, o = lax.scan(step, state0, (to_chunks(q), to_chunks(k), to_chunks(v)))
    o = jnp.moveaxis(o, 0, 1).reshape(B, T, H, v.shape[-1])
    return o, S


def retention_layer(h, pos, state0, g_pre, w_in, w_out, g_post):
    B, T, _ = h.shape
    x = rms_norm(h, g_pre)
    proj = x @ w_in
    q = proj[..., :RET_QK_W].reshape(B, T, RET_HEADS, RET_QK_DIM)
    k = proj[..., RET_QK_W:2 * RET_QK_W].reshape(B, T, RET_HEADS, RET_QK_DIM)
    v = proj[..., 2 * RET_QK_W:2 * RET_QK_W + RET_V_W].reshape(B, T, RET_HEADS, RET_V_DIM)
    gate = proj[..., 2 * RET_QK_W + RET_V_W:]
    inv = ret_inv_freq()
    q = rope(q, pos, inv)
    k = rope(k, pos, inv) * (RET_QK_DIM ** -0.5)
    chunk = RET_CHUNK if T % RET_CHUNK == 0 else T
    o, S = retention_chunkwise(q.astype(jnp.float32), k.astype(jnp.float32),
                               v.astype(jnp.float32), state0.astype(jnp.float32), chunk)
    o = o * lax.rsqrt(jnp.mean(o * o, axis=-1, keepdims=True) + EPS)
    o = o.astype(h.dtype).reshape(B, T, RET_V_W) * jax.nn.silu(gate)
    return h + rms_norm(o @ w_out, g_post), S


def shared_kv(h, pos, g_kv, w_kv):
    B, T, _ = h.shape
    kv = (rms_norm(h, g_kv) @ w_kv).reshape(B, T, 2, SWA_KV_HEADS, SWA_HEAD_DIM)
    k = rope(kv[:, :, 0], pos, partial_inv_freq())
    return k, kv[:, :, 1]


def sink_attention(q, k, v, qpos, kpos, sinks):
    s = jnp.einsum('bnqkgd,bnskd->bnkgqs', q, k).astype(jnp.float32) * (SWA_HEAD_DIM ** -0.5)
    rel = qpos[:, :, None] - kpos[:, None, :]
    ok = (rel >= 0) & (rel < WINDOW) & (kpos[:, None, :] >= 0)
    s = jnp.where(ok[None, :, None, None], s, NEG)
    sk = sinks.astype(jnp.float32).reshape(SWA_KV_HEADS, SWA_GROUP)[None, None, :, :, None, None]
    sk = jnp.broadcast_to(sk, s.shape[:-1] + (1,))
    p = jax.nn.softmax(jnp.concatenate([s, sk], axis=-1), axis=-1)[..., :-1]
    return jnp.einsum('bnkgqs,bnskd->bnqkgd', p.astype(v.dtype), v)


def attend_prompt(q, k, v, sinks):
    B, T, H, hd = q.shape
    n = T // SWA_BLOCK
    qb = q.reshape(B, n, SWA_BLOCK, SWA_KV_HEADS, SWA_GROUP, hd)

    def band(a):
        ab = a.reshape(B, n, SWA_BLOCK, SWA_KV_HEADS, hd)
        prev = jnp.concatenate([jnp.zeros_like(ab[:, :1]), ab[:, :-1]], axis=1)
        return jnp.concatenate([prev, ab], axis=2)

    qpos = jnp.arange(T, dtype=jnp.int32).reshape(n, SWA_BLOCK)
    kpos = (jnp.arange(n, dtype=jnp.int32)[:, None] - 1) * SWA_BLOCK + jnp.arange(2 * SWA_BLOCK, dtype=jnp.int32)[None]
    o = sink_attention(qb, band(k), band(v), qpos, kpos, sinks)
    return o.reshape(B, T, H * hd)


def attend_sample(q, k_all, v_all, sinks, q_start):
    B, T, H, hd = q.shape
    S = k_all.shape[1]
    qpos = (q_start + jnp.arange(T, dtype=jnp.int32))[None]
    kpos = (q_start - (S - T) + jnp.arange(S, dtype=jnp.int32))[None]
    o = sink_attention(q.reshape(B, 1, T, SWA_KV_HEADS, SWA_GROUP, hd), k_all[:, None], v_all[:, None],
                       qpos, kpos, sinks)
    return o.reshape(B, T, H * hd)


def swa_layer(h, pos, attend, g_pre, w_q, w_o, g_post):
    B, T, _ = h.shape
    q = (rms_norm(h, g_pre) @ w_q).reshape(B, T, SWA_Q_HEADS, SWA_HEAD_DIM)
    q = rope(q, pos, partial_inv_freq())
    o = attend(q)
    return h + rms_norm(o @ w_o, g_post)


def ffn(h, g_pre, w1, w2, g_post):
    u = jax.nn.relu(rms_norm(h, g_pre) @ w1)
    return h + rms_norm((u * u) @ w2, g_post)


def setup_inputs(seed: int = 0) -> dict:
    key = jax.random.key(seed)
    ks = jax.random.split(key, 24)
    n_a = DEPTH // 2
    n_b = DEPTH - n_a
    w_cache = min(WINDOW, PAST_LEN)

    def nrm(k, shape, scale):
        return jax.random.normal(k, shape, jnp.float32) * scale

    def gain(k, shape):
        return 1.0 + 0.05 * jax.random.normal(k, shape, jnp.float32)

    w_in_width = 2 * RET_QK_W + 2 * RET_V_W
    kv_shape = (DEC_BATCH, w_cache, SWA_KV_HEADS, SWA_HEAD_DIM)
    return {
        "x_prompt": nrm(ks[0], (BATCH, SEQ, D_MODEL), 1.0),
        "x_sample": nrm(ks[1], (DEC_BATCH, DEC_SEQ, D_MODEL), 1.0),
        "state_ret": nrm(ks[2], (n_a, DEC_BATCH, RET_HEADS, RET_QK_DIM, RET_V_DIM), 0.5),
        "cache_k_win": nrm(ks[3], kv_shape, 1.0),
        "cache_v_win": nrm(ks[4], kv_shape, 1.0),
        "ret_norm_pre": gain(ks[5], (n_a, D_MODEL)),
        "ret_w_in": nrm(ks[6], (n_a, D_MODEL, w_in_width), D_MODEL ** -0.5),
        "ret_w_out": nrm(ks[7], (n_a, RET_V_W, D_MODEL), RET_V_W ** -0.5),
        "ret_norm_post": gain(ks[8], (n_a, D_MODEL)),
        "kv_norm": gain(ks[9], (D_MODEL,)),
        "w_kv": nrm(ks[10], (D_MODEL, 2 * SWA_KV_HEADS * SWA_HEAD_DIM), D_MODEL ** -0.5),
        "swa_norm_pre": gain(ks[11], (n_b, D_MODEL)),
        "swa_w_q": nrm(ks[12], (n_b, D_MODEL, SWA_Q_HEADS * SWA_HEAD_DIM), D_MODEL ** -0.5),
        "swa_sinks": nrm(ks[13], (n_b, SWA_Q_HEADS), 1.0),
        "swa_w_o": nrm(ks[14], (n_b, SWA_Q_HEADS * SWA_HEAD_DIM, D_MODEL), (SWA_Q_HEADS * SWA_HEAD_DIM) ** -0.5),
        "swa_norm_post": gain(ks[15], (n_b, D_MODEL)),
        "ffn_norm_pre": gain(ks[16], (DEPTH, D_MODEL)),
        "ffn_w1": nrm(ks[17], (DEPTH, D_MODEL, D_FF), D_MODEL ** -0.5),
        "ffn_w2": nrm(ks[18], (DEPTH, D_FF, D_MODEL), D_FF ** -0.5),
        "ffn_norm_post": gain(ks[19], (DEPTH, D_MODEL)),
    }


def reference(x_prompt, x_sample, state_ret, cache_k_win, cache_v_win,
              ret_norm_pre, ret_w_in, ret_w_out, ret_norm_post,
              kv_norm, w_kv,
              swa_norm_pre, swa_w_q, swa_sinks, swa_w_o, swa_norm_post,
              ffn_norm_pre, ffn_w1, ffn_w2, ffn_norm_post):
    n_a = DEPTH // 2

    def trunk(h, pos, ret_state0, make_attend):
        ret_states = []
        k = v = None
        attend = None
        for l in range(DEPTH):
            if l < n_a:
                h, S = retention_layer(h, pos, ret_state0[l], ret_norm_pre[l], ret_w_in[l],
                                       ret_w_out[l], ret_norm_post[l])
                ret_states.append(S)
            else:
                if l == n_a:
                    k, v = shared_kv(h, pos, kv_norm, w_kv)
                    attend = make_attend(k, v)
                b = l - n_a
                h = swa_layer(h, pos, functools_partial(attend, swa_sinks[b]), swa_norm_pre[b],
                              swa_w_q[b], swa_w_o[b], swa_norm_post[b])
            h = ffn(h, ffn_norm_pre[l], ffn_w1[l], ffn_w2[l], ffn_norm_post[l])
        return h, jnp.stack(ret_states), k, v

    def functools_partial(attend, sinks):
        return lambda q: attend(q, sinks)

    B_p, T_p, _ = x_prompt.shape
    pos_p = jnp.arange(T_p, dtype=jnp.float32)
    zero_state = jnp.zeros((n_a, B_p, RET_HEADS, RET_QK_DIM, RET_V_DIM), jnp.float32)
    y_prompt, state_ret_p, k_p, v_p = trunk(
        x_prompt, pos_p, zero_state,
        lambda k, v: (lambda q, sinks: attend_prompt(q, k, v, sinks)))
    w_p = min(WINDOW, T_p)
    k_win_p = k_p[:, T_p - w_p:]
    v_win_p = v_p[:, T_p - w_p:]

    B_s, T_s, _ = x_sample.shape
    pos_s = PAST_LEN + jnp.arange(T_s, dtype=jnp.float32)
    w_s = cache_k_win.shape[1]
    kv_all = {}

    def make_attend_sample(k, v):
        kv_all['k'] = jnp.concatenate([cache_k_win.astype(k.dtype), k], axis=1)
        kv_all['v'] = jnp.concatenate([cache_v_win.astype(v.dtype), v], axis=1)
        return lambda q, sinks: attend_sample(q, kv_all['k'], kv_all['v'], sinks, PAST_LEN)

    y_sample, state_ret_s, _, _ = trunk(x_sample, pos_s, state_ret, make_attend_sample)
    k_win_s = kv_all['k'][:, T_s:T_s + w_s]
    v_win_s = kv_all['v'][:, T_s:T_s + w_s]

    return (y_prompt, y_sample, state_ret_p, state_ret_s, k_win_p, v_win_p, k_win_s, v_win_s)
```

```python
import functools

import jax
import jax.numpy as jnp
from jax import lax
from jax.experimental import pallas as pl
from jax.experimental.pallas import tpu as pltpu

DEPTH = 2
PAST_LEN = 16384
RET_HEADS = 4
RET_CHUNK = 128
RET_ROPE_THETA = 10000.0
SWA_HEAD_DIM = 64
SWA_KV_HEADS = 4
WINDOW = 128
ROPE_THETA = 500000.0
ROT_DIM = SWA_HEAD_DIM // 4
EPS = 1e-6
NEG = -1e30

LANES = 128
SUBLANES = 8
VMEM_CAP_BYTES = 64 * 1024 * 1024
VMEM_BUDGET_BYTES = VMEM_CAP_BYTES - 8 * 1024 * 1024

TOKEN_TILE = 512
COL_CHUNK = 1024
RET_STEP_TOKENS = 512
SAMPLE_GROUP = 2

F32 = jnp.float32
BF16 = jnp.bfloat16


def _params(semantics, vmem_bytes):
    limit = int(min(max(vmem_bytes, 16 * 1024 * 1024), VMEM_BUDGET_BYTES))
    return pltpu.CompilerParams(dimension_semantics=semantics, vmem_limit_bytes=limit)


def _resident(shape):
    nd = len(shape)
    return pl.BlockSpec(shape, lambda *_: (0,) * nd, pipeline_mode=pl.Buffered(1))


def _nbytes(shape, dtype):
    n = 1
    for s in shape:
        n *= s
    return n * jnp.dtype(dtype).itemsize


def _rms_rows(x):
    return x * lax.rsqrt(jnp.mean(x * x, axis=-1, keepdims=True) + EPS)


def _dot(a, b):
    return jnp.dot(a, b, preferred_element_type=F32)


def _dot_nt(a, b):
    return lax.dot_general(a, b, (((1,), (1,)), ((), ())), preferred_element_type=F32)


def _dot_tn(a, b):
    return lax.dot_general(a, b, (((0,), (0,)), ((), ())), preferred_element_type=F32)


def _ret_in_body(h_ref, g_ref, w_ref, cos_ref, sin_ref, q_ref, k_ref, v_ref, sg_ref, *, heads, dk, dv):
    xn = (_rms_rows(h_ref[...]) * g_ref[...]).astype(BF16)
    cos = cos_ref[...]
    sin = sin_ref[...]
    half = dk // 2
    qk_w = heads * dk
    v_w = heads * dv

    def proj(lo, width):
        return _dot(xn, w_ref[:, lo:lo + width])

    for base, ref, scale in ((0, q_ref, None), (qk_w, k_ref, dk ** -0.5)):
        p = proj(base, qk_w)
        for hh in range(heads):
            x1 = p[:, hh * dk:hh * dk + half]
            x2 = p[:, hh * dk + half:(hh + 1) * dk]
            o1 = x1 * cos - x2 * sin
            o2 = x2 * cos + x1 * sin
            if scale is not None:
                o1 = o1 * scale
                o2 = o2 * scale
            ref[:, hh * dk:hh * dk + half] = o1.astype(ref.dtype)
            ref[:, hh * dk + half:(hh + 1) * dk] = o2.astype(ref.dtype)
    cw = min(COL_CHUNK, v_w)
    for c in range(v_w // cw):
        v_ref[:, c * cw:(c + 1) * cw] = proj(2 * qk_w + c * cw, cw).astype(v_ref.dtype)
    for c in range(v_w // cw):
        gate = proj(2 * qk_w + v_w + c * cw, cw)
        sg_ref[:, c * cw:(c + 1) * cw] = (gate * jax.nn.sigmoid(gate)).astype(sg_ref.dtype)


def _ret_in(h, g, w_in, cos, sin, *, heads, dk, dv, out_dtype):
    n, d = h.shape
    tm = min(TOKEN_TILE, n)
    qk_w, v_w = heads * dk, heads * dv
    pos_tiles = cos.shape[0] // tm
    row = lambda i: (i, 0)
    tab = lambda i: (i % pos_tiles, 0)
    vmem = (2 * _nbytes((tm, d), F32) + _nbytes(w_in.shape, BF16) + 4 * _nbytes((tm, dk // 2), F32)
            + 2 * _nbytes((tm, 2 * qk_w + 2 * v_w), out_dtype) + 4 * _nbytes((tm, COL_CHUNK), F32))
    return pl.pallas_call(
        functools.partial(_ret_in_body, heads=heads, dk=dk, dv=dv),
        grid=(n // tm,),
        in_specs=[pl.BlockSpec((tm, d), row), _resident((1, d)), _resident(w_in.shape),
                  pl.BlockSpec((tm, dk // 2), tab), pl.BlockSpec((tm, dk // 2), tab)],
        out_specs=[pl.BlockSpec((tm, qk_w), row), pl.BlockSpec((tm, qk_w), row),
                   pl.BlockSpec((tm, v_w), row), pl.BlockSpec((tm, v_w), row)],
        out_shape=[jax.ShapeDtypeStruct((n, qk_w), out_dtype), jax.ShapeDtypeStruct((n, qk_w), out_dtype),
                   jax.ShapeDtypeStruct((n, v_w), out_dtype), jax.ShapeDtypeStruct((n, v_w), out_dtype)],
        compiler_params=_params(("parallel",), vmem),
        name="ret_in",
    )(h, g, w_in, cos, sin)


def _ret_prompt_body(lg_ref, q_ref, k_ref, v_ref, sg_ref, o_ref, s_out_ref, s_ref, *, chunk):
    hh = pl.program_id(1)
    t = pl.program_id(2)
    lg = lg_ref[hh]
    tc, dk = q_ref.shape
    dv = v_ref.shape[1]

    @pl.when(t == 0)
    def _():
        s_ref[...] = jnp.zeros_like(s_ref)

    ri = lax.broadcasted_iota(jnp.int32, (chunk, chunk), 0)
    ci = lax.broadcasted_iota(jnp.int32, (chunk, chunk), 1)
    rel = (ri - ci).astype(F32)
    intra = jnp.where(rel >= 0, jnp.exp(lg * jnp.maximum(rel, 0.0)), 0.0)
    row_v = lax.broadcasted_iota(jnp.int32, (chunk, dv), 0).astype(F32)
    row_k = lax.broadcasted_iota(jnp.int32, (chunk, dk), 0).astype(F32)
    q_decay = jnp.exp(lg * (row_v + 1.0))
    k_decay = jnp.exp(lg * (chunk - 1.0 - row_k))
    chunk_decay = jnp.exp(jnp.full((1, dv), lg * chunk, F32))

    for c in range(tc // chunk):
        rows = pl.ds(c * chunk, chunk)
        q = q_ref[rows, :]
        k = k_ref[rows, :]
        v = v_ref[rows, :]
        s_prev = s_ref[...]
        sc = _dot_nt(q, k) * intra
        o = _dot(sc.astype(BF16), v) + q_decay * _dot(q, s_prev.astype(BF16))
        kd = (k.astype(F32) * k_decay).astype(BF16)
        s_ref[...] = chunk_decay * s_prev + _dot_tn(kd, v)
        o = _rms_rows(o)
        o_ref[rows, :] = (o * sg_ref[rows, :].astype(F32)).astype(o_ref.dtype)

    @pl.when(t == pl.num_programs(2) - 1)
    def _():
        s_out_ref[0, 0] = s_ref[...]


def _ret_prompt(log_g, q, k, v, sg, *, batch, heads, dk, dv):
    n = q.shape[0]
    seq = n // batch
    chunk = RET_CHUNK if seq % RET_CHUNK == 0 else seq
    tc = min(RET_STEP_TOKENS, seq)
    tc = tc if (tc % chunk == 0 and seq % tc == 0) else chunk
    nt = seq // tc
    blk = lambda b, h, t: (b * nt + t, h)
    vmem = (4 * _nbytes((tc, dk), BF16) + 6 * _nbytes((tc, dv), BF16) + 3 * _nbytes((dk, dv), F32)
            + 8 * _nbytes((chunk, dv), F32) + 2 * _nbytes((dk, dv), F32))
    return pl.pallas_call(
        functools.partial(_ret_prompt_body, chunk=chunk),
        grid=(batch, heads, nt),
        in_specs=[pl.BlockSpec(memory_space=pltpu.SMEM),
                  pl.BlockSpec((tc, dk), blk), pl.BlockSpec((tc, dk), blk),
                  pl.BlockSpec((tc, dv), blk), pl.BlockSpec((tc, dv), blk)],
        out_specs=[pl.BlockSpec((tc, dv), blk),
                   pl.BlockSpec((1, 1, dk, dv), lambda b, h, t: (b, h, 0, 0))],
        out_shape=[jax.ShapeDtypeStruct((n, heads * dv), BF16),
                   jax.ShapeDtypeStruct((batch, heads, dk, dv), F32)],
        scratch_shapes=[pltpu.VMEM((dk, dv), F32)],
        compiler_params=_params(("parallel", "parallel", "arbitrary"), vmem),
        name="ret_prompt",
    )(log_g, q, k, v, sg)


def _ret_sample_body(lg_ref, q_ref, k_ref, v_ref, sg_ref, s_in_ref, o_ref, s_out_ref, *, seq, heads, dk, dv):
    rows = q_ref.shape[0]
    group = rows // seq
    ri = lax.broadcasted_iota(jnp.int32, (rows, rows), 0)
    ci = lax.broadcasted_iota(jnp.int32, (rows, rows), 1)
    same = (ri // seq) == (ci // seq)
    rel = (ri - ci).astype(F32)
    row_v = lax.broadcasted_iota(jnp.int32, (rows, dv), 0)
    row_k = lax.broadcasted_iota(jnp.int32, (rows, dk), 0)
    for hh in range(heads):
        lg = lg_ref[hh]
        intra = jnp.where(same & (rel >= 0), jnp.exp(lg * jnp.maximum(rel, 0.0)), 0.0)
        q_decay = jnp.exp(lg * ((row_v % seq).astype(F32) + 1.0))
        k_decay = jnp.exp(lg * (seq - 1.0 - (row_k % seq).astype(F32)))
        chunk_decay = jnp.exp(jnp.full((1, dv), lg * seq, F32))
        q = q_ref[:, hh * dk:(hh + 1) * dk].astype(BF16)
        kf = k_ref[:, hh * dk:(hh + 1) * dk]
        v = v_ref[:, hh * dv:(hh + 1) * dv].astype(BF16)
        sc = _dot_nt(q, kf.astype(BF16)) * intra
        o = _dot(sc.astype(BF16), v)
        kd = kf * k_decay
        inter = jnp.zeros((rows, dv), F32)
        for g in range(group):
            s_prev = s_in_ref[g, hh]
            mine_v = (row_v // seq) == g
            mine_k = (row_k // seq) == g
            inter = jnp.where(mine_v, _dot(q, s_prev.astype(BF16)), inter)
            kd_g = jnp.where(mine_k, kd, 0.0).astype(BF16)
            s_out_ref[g, hh] = chunk_decay * s_prev + _dot_tn(kd_g, v)
        o = _rms_rows(o + q_decay * inter)
        o_ref[:, hh * dv:(hh + 1) * dv] = (o * sg_ref[:, hh * dv:(hh + 1) * dv]).astype(o_ref.dtype)


def _ret_sample(log_g, q, k, v, sg, state, *, seq, heads, dk, dv):
    n = q.shape[0]
    batch = n // seq
    group = SAMPLE_GROUP if batch % SAMPLE_GROUP == 0 else batch
    rows = group * seq
    row = lambda i: (i, 0)
    st = lambda i: (i, 0, 0, 0)
    vmem = (4 * _nbytes((group, heads, dk, dv), F32) + 8 * _nbytes((rows, heads * dv), F32)
            + 4 * _nbytes((dk, dv), F32))
    return pl.pallas_call(
        functools.partial(_ret_sample_body, seq=seq, heads=heads, dk=dk, dv=dv),
        grid=(batch // group,),
        in_specs=[pl.BlockSpec(memory_space=pltpu.SMEM),
                  pl.BlockSpec((rows, heads * dk), row), pl.BlockSpec((rows, heads * dk), row),
                  pl.BlockSpec((rows, heads * dv), row), pl.BlockSpec((rows, heads * dv), row),
                  pl.BlockSpec((group, heads, dk, dv), st)],
        out_specs=[pl.BlockSpec((rows, heads * dv), row), pl.BlockSpec((group, heads, dk, dv), st)],
        out_shape=[jax.ShapeDtypeStruct((n, heads * dv), F32),
                   jax.ShapeDtypeStruct((batch, heads, dk, dv), F32)],
        compiler_params=_params(("parallel",), vmem),
        name="ret_sample",
    )(log_g, q, k, v, sg, state)


def _out_ffn_body(o_ref, h_ref, wo_ref, g_post_ref, g_pre_ref, w1_ref, w2_ref, g_ffn_ref, y_ref):
    a = _dot(o_ref[...].astype(BF16), wo_ref[...])
    h1 = h_ref[...] + _rms_rows(a) * g_post_ref[...]
    x = (_rms_rows(h1) * g_pre_ref[...]).astype(BF16)
    d_ff = w1_ref.shape[1]
    fc = min(COL_CHUNK, d_ff)
    acc = jnp.zeros(h1.shape, F32)
    for c in range(d_ff // fc):
        u = jnp.maximum(_dot(x, w1_ref[:, c * fc:(c + 1) * fc]), 0.0)
        acc = acc + _dot((u * u).astype(BF16), w2_ref[c * fc:(c + 1) * fc, :])
    y_ref[...] = h1 + _rms_rows(acc) * g_ffn_ref[...]


def _out_ffn(o, h, w_o, g_post, g_pre, w1, w2, g_ffn):
    n, d = h.shape
    kdim = o.shape[1]
    tm = min(TOKEN_TILE, n)
    row = lambda i: (i, 0)
    vmem = (2 * _nbytes((tm, kdim), o.dtype) + 4 * _nbytes((tm, d), F32) + _nbytes(w_o.shape, BF16)
            + _nbytes(w1.shape, BF16) + _nbytes(w2.shape, BF16) + 6 * _nbytes((tm, COL_CHUNK), F32))
    return pl.pallas_call(
        _out_ffn_body,
        grid=(n // tm,),
        in_specs=[pl.BlockSpec((tm, kdim), row), pl.BlockSpec((tm, d), row), _resident(w_o.shape),
                  _resident((1, d)), _resident((1, d)), _resident(w1.shape), _resident(w2.shape),
                  _resident((1, d))],
        out_specs=pl.BlockSpec((tm, d), row),
        out_shape=jax.ShapeDtypeStruct((n, d), F32),
        compiler_params=_params(("parallel",), vmem),
        name="out_ffn",
    )(o, h, w_o, g_post, g_pre, w1, w2, g_ffn)


def _partial_rope(x, c_tab, sa_tab, sb_tab):
    half = ROT_DIM // 2
    outs = []
    for j in range(x.shape[1] // LANES):
        s = x[:, j * LANES:(j + 1) * LANES]
        outs.append(s * c_tab + pltpu.roll(s, LANES - half, axis=1) * sa_tab + pltpu.roll(s, half, axis=1) * sb_tab)
    return outs


def _swa_in_body(h_ref, g_q_ref, g_kv_ref, wq_ref, wkv_ref, c_ref, sa_ref, sb_ref, q_ref, k_ref, v_ref):
    y = _rms_rows(h_ref[...])
    xq = (y * g_q_ref[...]).astype(BF16)
    xkv = (y * g_kv_ref[...]).astype(BF16)
    c_tab, sa_tab, sb_tab = c_ref[...], sa_ref[...], sb_ref[...]
    q = _dot(xq, wq_ref[...]) * (SWA_HEAD_DIM ** -0.5)
    for j, s in enumerate(_partial_rope(q, c_tab, sa_tab, sb_tab)):
        q_ref[:, j * LANES:(j + 1) * LANES] = s.astype(q_ref.dtype)
    kv = _dot(xkv, wkv_ref[...])
    kw = k_ref.shape[1]
    for j, s in enumerate(_partial_rope(kv[:, :kw], c_tab, sa_tab, sb_tab)):
        k_ref[:, j * LANES:(j + 1) * LANES] = s
    v_ref[...] = kv[:, kw:]


def _swa_in(h, g_q, g_kv, w_q, w_kv, c_tab, sa_tab, sb_tab, *, q_dtype):
    n, d = h.shape
    tm = min(TOKEN_TILE, n)
    qw = w_q.shape[1]
    kw = w_kv.shape[1] // 2
    pos_tiles = c_tab.shape[0] // tm
    row = lambda i: (i, 0)
    tab = lambda i: (i % pos_tiles, 0)
    vmem = (2 * _nbytes((tm, d), F32) + _nbytes(w_q.shape, BF16) + _nbytes(w_kv.shape, BF16)
            + 6 * _nbytes((tm, LANES), F32) + 2 * _nbytes((tm, qw), q_dtype) + 4 * _nbytes((tm, kw), F32)
            + 6 * _nbytes((tm, qw), F32))
    return pl.pallas_call(
        _swa_in_body,
        grid=(n // tm,),
        in_specs=[pl.BlockSpec((tm, d), row), _resident((1, d)), _resident((1, d)),
                  _resident(w_q.shape), _resident(w_kv.shape),
                  pl.BlockSpec((tm, LANES), tab), pl.BlockSpec((tm, LANES), tab), pl.BlockSpec((tm, LANES), tab)],
        out_specs=[pl.BlockSpec((tm, qw), row), pl.BlockSpec((tm, kw), row), pl.BlockSpec((tm, kw), row)],
        out_shape=[jax.ShapeDtypeStruct((n, qw), q_dtype), jax.ShapeDtypeStruct((n, kw), F32),
                   jax.ShapeDtypeStruct((n, kw), F32)],
        compiler_params=_params(("parallel",), vmem),
        name="swa_in",
    )(h, g_q, g_kv, w_q, w_kv, c_tab, sa_tab, sb_tab)


def _sink_rows(sinks_ref, kvh, group, rows_per_head, shape):
    r = lax.broadcasted_iota(jnp.int32, shape, 0) // rows_per_head
    col = jnp.full(shape, sinks_ref[kvh * group], F32)
    for g in range(1, group):
        col = jnp.where(r == g, sinks_ref[kvh * group + g], col)
    return col


def _attn_prompt_body(sinks_ref, q_ref, kp_ref, kc_ref, vp_ref, vc_ref, o_ref, *, group):
    i = pl.program_id(1)
    blk = q_ref.shape[0]
    hd = SWA_HEAD_DIM
    kvh_n = kc_ref.shape[1] // hd
    k_all = jnp.concatenate([kp_ref[...], kc_ref[...]], axis=0).astype(BF16)
    v_all = jnp.concatenate([vp_ref[...], vc_ref[...]], axis=0).astype(BF16)
    rows = group * blk
    qi = lax.broadcasted_iota(jnp.int32, (rows, 2 * blk), 0) % blk
    kj = lax.broadcasted_iota(jnp.int32, (rows, 2 * blk), 1)
    rel = qi + blk - kj
    ok = (rel >= 0) & (rel < WINDOW) & ((kj >= blk) | (i > 0))
    for kvh in range(kvh_n):
        q = jnp.concatenate([q_ref[:, (kvh * group + g) * hd:(kvh * group + g + 1) * hd] for g in range(group)], axis=0)
        s = jnp.where(ok, _dot_nt(q, k_all[:, kvh * hd:(kvh + 1) * hd]), NEG)
        sink = _sink_rows(sinks_ref, kvh, group, blk, (rows, 1))
        m = jnp.maximum(jnp.max(s, axis=-1, keepdims=True), sink)
        e = jnp.exp(s - m)
        denom = jnp.sum(e, axis=-1, keepdims=True) + jnp.exp(sink - m)
        o = _dot(e.astype(BF16), v_all[:, kvh * hd:(kvh + 1) * hd]) / denom
        for g in range(group):
            hq = kvh * group + g
            o_ref[:, hq * hd:(hq + 1) * hd] = o[g * blk:(g + 1) * blk].astype(o_ref.dtype)


def _attn_prompt(sinks, q, k, v, *, batch):
    n, qw = q.shape
    kw = k.shape[1]
    seq = n // batch
    blk = WINDOW
    nb = seq // blk
    group = qw // kw
    cur = lambda b, i: (b * nb + i, 0)
    prev = lambda b, i: (b * nb + jnp.maximum(i - 1, 0), 0)
    vmem = (4 * _nbytes((blk, qw), BF16) + 8 * _nbytes((blk, kw), F32) + 8 * _nbytes((group * blk, 2 * blk), F32))
    return pl.pallas_call(
        functools.partial(_attn_prompt_body, group=group),
        grid=(batch, nb),
        in_specs=[pl.BlockSpec(memory_space=pltpu.SMEM), pl.BlockSpec((blk, qw), cur),
                  pl.BlockSpec((blk, kw), prev), pl.BlockSpec((blk, kw), cur),
                  pl.BlockSpec((blk, kw), prev), pl.BlockSpec((blk, kw), cur)],
        out_specs=pl.BlockSpec((blk, qw), cur),
        out_shape=jax.ShapeDtypeStruct((n, qw), BF16),
        compiler_params=_params(("parallel", "parallel"), vmem),
        name="attn_prompt",
    )(sinks, q, k, k, v, v)


def _attn_sample_body(sinks_ref, q_ref, kn_ref, vn_ref, kc_ref, vc_ref, o_ref, kw_ref, vw_ref, *, seq, group, q_start):
    rows = q_ref.shape[0]
    bgroup = rows // seq
    win = kc_ref.shape[1]
    hd = SWA_HEAD_DIM
    kvh_n = kn_ref.shape[1] // hd
    srows = group * rows
    kn = kn_ref[...]
    vn = vn_ref[...]
    r_c = lax.broadcasted_iota(jnp.int32, (srows, win), 0) % rows
    c_c = lax.broadcasted_iota(jnp.int32, (srows, win), 1)
    rel_c = (r_c % seq) + win - c_c
    ok_c = (rel_c >= 0) & (rel_c < WINDOW) & (q_start - win + c_c >= 0)
    r_n = lax.broadcasted_iota(jnp.int32, (srows, rows), 0) % rows
    c_n = lax.broadcasted_iota(jnp.int32, (srows, rows), 1)
    rel_n = (r_n % seq) - (c_n % seq)
    ok_n = (rel_n >= 0) & (rel_n < WINDOW) & ((r_n // seq) == (c_n // seq))
    owner = (lax.broadcasted_iota(jnp.int32, (srows, hd), 0) % rows) // seq
    for kvh in range(kvh_n):
        cols = slice(kvh * hd, (kvh + 1) * hd)
        q = jnp.concatenate([q_ref[:, (kvh * group + g) * hd:(kvh * group + g + 1) * hd] for g in range(group)],
                            axis=0).astype(BF16)
        sink = _sink_rows(sinks_ref, kvh, group, rows, (srows, 1))
        s_n = jnp.where(ok_n, _dot_nt(q, kn[:, cols].astype(BF16)), NEG)
        m_n = jnp.max(s_n, axis=-1, keepdims=True)
        o = jnp.zeros((srows, hd), F32)
        for b in range(bgroup):
            s_c = jnp.where(ok_c, _dot_nt(q, kc_ref[b, :, cols].astype(BF16)), NEG)
            m = jnp.maximum(jnp.maximum(jnp.max(s_c, axis=-1, keepdims=True), m_n), sink)
            e_c = jnp.exp(s_c - m)
            e_n = jnp.exp(s_n - m)
            denom = jnp.sum(e_c, axis=-1, keepdims=True) + jnp.sum(e_n, axis=-1, keepdims=True) + jnp.exp(sink - m)
            o_b = (_dot(e_c.astype(BF16), vc_ref[b, :, cols].astype(BF16))
                   + _dot(e_n.astype(BF16), vn[:, cols].astype(BF16))) / denom
            o = jnp.where(owner == b, o_b, o)
        for g in range(group):
            hq = kvh * group + g
            o_ref[:, hq * hd:(hq + 1) * hd] = o[g * rows:(g + 1) * rows].astype(o_ref.dtype)
    for b in range(bgroup):
        kw_ref[b, 0:win - seq, :] = kc_ref[b, seq:win, :]
        kw_ref[b, win - seq:win, :] = kn[b * seq:(b + 1) * seq, :]
        vw_ref[b, 0:win - seq, :] = vc_ref[b, seq:win, :]
        vw_ref[b, win - seq:win, :] = vn[b * seq:(b + 1) * seq, :]


def _attn_sample(sinks, q, k_new, v_new, k_cache, v_cache, *, seq, q_start):
    n, qw = q.shape
    kw = k_new.shape[1]
    batch, win, _ = k_cache.shape
    group = qw // kw
    bgroup = SAMPLE_GROUP if batch % SAMPLE_GROUP == 0 else batch
    rows = bgroup * seq
    row = lambda i: (i, 0)
    cache = lambda i: (i, 0, 0)
    vmem = 8 * _nbytes((bgroup, win, kw), F32) + 8 * _nbytes((rows, qw), F32) + 16 * _nbytes((group * rows, win), F32)
    return pl.pallas_call(
        functools.partial(_attn_sample_body, seq=seq, group=group, q_start=q_start),
        grid=(batch // bgroup,),
        in_specs=[pl.BlockSpec(memory_space=pltpu.SMEM), pl.BlockSpec((rows, qw), row),
                  pl.BlockSpec((rows, kw), row), pl.BlockSpec((rows, kw), row),
                  pl.BlockSpec((bgroup, win, kw), cache), pl.BlockSpec((bgroup, win, kw), cache)],
        out_specs=[pl.BlockSpec((rows, qw), row), pl.BlockSpec((bgroup, win, kw), cache),
                   pl.BlockSpec((bgroup, win, kw), cache)],
        out_shape=[jax.ShapeDtypeStruct((n, qw), F32), jax.ShapeDtypeStruct((batch, win, kw), F32),
                   jax.ShapeDtypeStruct((batch, win, kw), F32)],
        compiler_params=_params(("parallel",), vmem),
        name="attn_sample",
    )(sinks, q, k_new, v_new, k_cache, v_cache)


def _ret_rope_tables(pos, dk):
    inv = 1.0 / (RET_ROPE_THETA ** jnp.linspace(0.0, 1.0, dk // 2, dtype=F32))
    ang = pos[:, None] * inv[None, :]
    return jnp.cos(ang), jnp.sin(ang)


def _swa_rope_tables(pos):
    half = ROT_DIM // 2
    inv = ROPE_THETA ** (-jnp.arange(half, dtype=F32) / half)
    ang = pos[:, None] * inv[None, :]
    cos, sin = jnp.cos(ang), jnp.sin(ang)
    n = pos.shape[0]
    pad = jnp.zeros((n, SWA_HEAD_DIM - 2 * half), F32)
    c_head = jnp.concatenate([cos, cos, pad + 1.0], axis=1)
    sa_head = jnp.concatenate([-sin, jnp.zeros_like(sin), pad], axis=1)
    sb_head = jnp.concatenate([jnp.zeros_like(sin), sin, pad], axis=1)
    reps = LANES // SWA_HEAD_DIM
    return tuple(jnp.tile(t, (1, reps)) for t in (c_head, sa_head, sb_head))


def _tile_rows(tab, seq, n):
    tm = min(TOKEN_TILE, n)
    return tab if seq >= tm else jnp.tile(tab, (tm // seq, 1))


def kernel(x_prompt, x_sample, state_ret, cache_k_win, cache_v_win, ret_norm_pre, ret_w_in, ret_w_out, ret_norm_post, kv_norm, w_kv, swa_norm_pre, swa_w_q, swa_sinks, swa_w_o, swa_norm_post, ffn_norm_pre, ffn_w1, ffn_w2, ffn_norm_post):
    n_a = DEPTH // 2
    assert n_a == 1 and DEPTH == 2, "one retention layer followed by one sliding-window layer"
    d = x_prompt.shape[-1]
    heads = RET_HEADS
    dk = ret_w_out.shape[-1] // heads
    dv = ret_w_out.shape[-2] // heads
    kvh, hd = SWA_KV_HEADS, SWA_HEAD_DIM
    row2 = lambda g: g.reshape(1, d)
    log_g = jnp.log1p(-jnp.exp2(-5.0 - jnp.arange(heads, dtype=F32)))

    w_in = ret_w_in[0].astype(BF16)
    w_out = ret_w_out[0].astype(BF16)
    wq = swa_w_q[0].astype(BF16)
    wkv = w_kv.astype(BF16)
    wo = swa_w_o[0].astype(BF16)
    w1 = ffn_w1.astype(BF16)
    w2 = ffn_w2.astype(BF16)
    sinks = swa_sinks[0]

    def trunk(x, pos, ret_core, attend, act_dtype):
        b, t, _ = x.shape
        n = b * t
        h = x.reshape(n, d)
        cos, sin = (_tile_rows(tab, t, n) for tab in _ret_rope_tables(pos, dk))
        q, k, v, sg = _ret_in(h, row2(ret_norm_pre[0]), w_in, cos, sin, heads=heads, dk=dk, dv=dv, out_dtype=act_dtype)
        o, state = ret_core(q, k, v, sg)
        h = _out_ffn(o, h, w_out, row2(ret_norm_post[0]), row2(ffn_norm_pre[0]), w1[0], w2[0], row2(ffn_norm_post[0]))
        tabs = tuple(_tile_rows(tab, t, n) for tab in _swa_rope_tables(pos))
        q, k, v = _swa_in(h, row2(swa_norm_pre[0]), row2(kv_norm), wq, wkv, *tabs, q_dtype=act_dtype)
        o, extra = attend(q, k, v)
        h = _out_ffn(o, h, wo, row2(swa_norm_post[0]), row2(ffn_norm_pre[1]), w1[1], w2[1], row2(ffn_norm_post[1]))
        return h.reshape(b, t, d), state, k, v, extra

    b_p, t_p, _ = x_prompt.shape
    y_prompt, state_p, k_p, v_p, _ = trunk(
        x_prompt, jnp.arange(t_p, dtype=F32),
        lambda q, k, v, sg: _ret_prompt(log_g, q, k, v, sg, batch=b_p, heads=heads, dk=dk, dv=dv),
        lambda q, k, v: (_attn_prompt(sinks, q, k, v, batch=b_p), None),
        BF16)
    w_p = min(WINDOW, t_p)
    k_win_p = k_p.reshape(b_p, t_p, kvh, hd)[:, t_p - w_p:]
    v_win_p = v_p.reshape(b_p, t_p, kvh, hd)[:, t_p - w_p:]

    b_s, t_s, _ = x_sample.shape
    w_s = cache_k_win.shape[1]
    kc = cache_k_win.reshape(b_s, w_s, kvh * hd)
    vc = cache_v_win.reshape(b_s, w_s, kvh * hd)

    def attend_sample(q, k, v):
        o, k_win, v_win = _attn_sample(sinks, q, k, v, kc, vc, seq=t_s, q_start=PAST_LEN)
        return o, (k_win, v_win)

    y_sample, state_s, _, _, (k_win_s, v_win_s) = trunk(
        x_sample, PAST_LEN + jnp.arange(t_s, dtype=F32),
        lambda q, k, v, sg: _ret_sample(log_g, q, k, v, sg, state_ret[0], seq=t_s, heads=heads, dk=dk, dv=dv),
        attend_sample, F32)

    return (y_prompt, y_sample, state_p[None], state_s[None], k_win_p, v_win_p,
            k_win_s.reshape(b_s, w_s, kvh, hd), v_win_s.reshape(b_s, w_s, kvh, hd))
```

```python
import functools

import jax
import jax.numpy as jnp
from jax import lax
from jax.experimental import pallas as pl
from jax.experimental.pallas import tpu as pltpu

DEPTH = 2
PAST_LEN = 16384
RET_HEADS = 4
RET_CHUNK = 128
RET_ROPE_THETA = 10000.0
SWA_HEAD_DIM = 64
SWA_KV_HEADS = 4
WINDOW = 128
ROPE_THETA = 500000.0
ROT_DIM = SWA_HEAD_DIM // 4
EPS = 1e-6
NEG = -1e30
LOG2E = 1.4426950408889634

LANES = 128
SUBLANES = 8
BF16_SUBLANES = 16
VMEM_CAP_BYTES = 64 * 1024 * 1024
VMEM_BUDGET_BYTES = VMEM_CAP_BYTES - 8 * 1024 * 1024

TOKEN_TILE = 512
COL_CHUNK = 1024
RET_STEP_TOKENS = 512
SAMPLE_GROUP = 2
ATTN_BLOCKS_PER_STEP = 4

F32 = jnp.float32
BF16 = jnp.bfloat16


def _params(semantics, vmem_bytes):
    limit = int(min(max(vmem_bytes, 16 * 1024 * 1024), VMEM_BUDGET_BYTES))
    return pltpu.CompilerParams(dimension_semantics=semantics, vmem_limit_bytes=limit)


def _resident(shape):
    nd = len(shape)
    return pl.BlockSpec(shape, lambda *_: (0,) * nd, pipeline_mode=pl.Buffered(1))


def _nbytes(shape, dtype):
    n = 1
    for s in shape:
        n *= s
    return n * jnp.dtype(dtype).itemsize


def _rms_rows(x):
    return x * lax.rsqrt(jnp.mean(x * x, axis=-1, keepdims=True) + EPS)


def _dot(a, b):
    return jnp.dot(a, b, preferred_element_type=F32)


def _dot_nt(a, b):
    return lax.dot_general(a, b, (((1,), (1,)), ((), ())), preferred_element_type=F32)


def _dot_tn(a, b):
    return lax.dot_general(a, b, (((0,), (0,)), ((), ())), preferred_element_type=F32)


def _ret_in_body(h_ref, g_ref, w_ref, cos_ref, sin_ref, q_ref, k_ref, v_ref, sg_ref, *, heads, dk, dv):
    xn = (_rms_rows(h_ref[...]) * g_ref[...]).astype(BF16)
    cos = cos_ref[...]
    sin = sin_ref[...]
    half = dk // 2
    qk_w = heads * dk
    v_w = heads * dv

    def proj(lo, width):
        return _dot(xn, w_ref[:, lo:lo + width])

    for base, ref, scale in ((0, q_ref, None), (qk_w, k_ref, dk ** -0.5)):
        p = proj(base, qk_w)
        for hh in range(heads):
            x1 = p[:, hh * dk:hh * dk + half]
            x2 = p[:, hh * dk + half:(hh + 1) * dk]
            o1 = x1 * cos - x2 * sin
            o2 = x2 * cos + x1 * sin
            if scale is not None:
                o1 = o1 * scale
                o2 = o2 * scale
            ref[:, hh * dk:hh * dk + half] = o1.astype(ref.dtype)
            ref[:, hh * dk + half:(hh + 1) * dk] = o2.astype(ref.dtype)
    cw = min(COL_CHUNK, v_w)
    for c in range(v_w // cw):
        v_ref[:, c * cw:(c + 1) * cw] = proj(2 * qk_w + c * cw, cw).astype(v_ref.dtype)
    for c in range(v_w // cw):
        gate = proj(2 * qk_w + v_w + c * cw, cw)
        sg_ref[:, c * cw:(c + 1) * cw] = (gate * jax.nn.sigmoid(gate)).astype(sg_ref.dtype)


def _ret_in(h, g, w_in, cos, sin, *, heads, dk, dv, out_dtype):
    n, d = h.shape
    tm = min(TOKEN_TILE, n)
    qk_w, v_w = heads * dk, heads * dv
    pos_tiles = cos.shape[0] // tm
    row = lambda i: (i, 0)
    tab = lambda i: (i % pos_tiles, 0)
    vmem = (2 * _nbytes((tm, d), F32) + _nbytes(w_in.shape, BF16) + 4 * _nbytes((tm, dk // 2), F32)
            + 2 * _nbytes((tm, 2 * qk_w + 2 * v_w), out_dtype) + 4 * _nbytes((tm, COL_CHUNK), F32))
    return pl.pallas_call(
        functools.partial(_ret_in_body, heads=heads, dk=dk, dv=dv),
        grid=(n // tm,),
        in_specs=[pl.BlockSpec((tm, d), row), _resident((1, d)), _resident(w_in.shape),
                  pl.BlockSpec((tm, dk // 2), tab), pl.BlockSpec((tm, dk // 2), tab)],
        out_specs=[pl.BlockSpec((tm, qk_w), row), pl.BlockSpec((tm, qk_w), row),
                   pl.BlockSpec((tm, v_w), row), pl.BlockSpec((tm, v_w), row)],
        out_shape=[jax.ShapeDtypeStruct((n, qk_w), out_dtype), jax.ShapeDtypeStruct((n, qk_w), out_dtype),
                   jax.ShapeDtypeStruct((n, v_w), out_dtype), jax.ShapeDtypeStruct((n, v_w), out_dtype)],
        compiler_params=_params(("parallel",), vmem),
        name="ret_in",
    )(h, g, w_in, cos, sin)


def _ret_prompt_body(lg_ref, q_ref, k_ref, v_ref, sg_ref, o_ref, s_out_ref, s_ref, *, chunk):
    hh = pl.program_id(1)
    t = pl.program_id(2)
    lg = lg_ref[hh]
    tc, dk = q_ref.shape
    dv = v_ref.shape[1]

    @pl.when(t == 0)
    def _():
        s_ref[...] = jnp.zeros_like(s_ref)

    ri = lax.broadcasted_iota(jnp.int32, (chunk, chunk), 0)
    ci = lax.broadcasted_iota(jnp.int32, (chunk, chunk), 1)
    rel = (ri - ci).astype(F32)
    intra = jnp.where(rel >= 0, jnp.exp(lg * jnp.maximum(rel, 0.0)), 0.0)
    row_v = lax.broadcasted_iota(jnp.int32, (chunk, dv), 0).astype(F32)
    row_k = lax.broadcasted_iota(jnp.int32, (chunk, dk), 0).astype(F32)
    q_decay = jnp.exp(lg * (row_v + 1.0))
    k_decay = jnp.exp(lg * (chunk - 1.0 - row_k))
    chunk_decay = jnp.exp(jnp.full((1, dv), lg * chunk, F32))

    for c in range(tc // chunk):
        rows = pl.ds(c * chunk, chunk)
        q = q_ref[rows, :]
        k = k_ref[rows, :]
        v = v_ref[rows, :]
        s_prev = s_ref[...]
        sc = _dot_nt(q, k) * intra
        o = _dot(sc.astype(BF16), v) + q_decay * _dot(q, s_prev.astype(BF16))
        kd = (k.astype(F32) * k_decay).astype(BF16)
        s_ref[...] = chunk_decay * s_prev + _dot_tn(kd, v)
        o = _rms_rows(o)
        o_ref[rows, :] = (o * sg_ref[rows, :].astype(F32)).astype(o_ref.dtype)

    @pl.when(t == pl.num_programs(2) - 1)
    def _():
        s_out_ref[0, 0] = s_ref[...]


def _ret_prompt(log_g, q, k, v, sg, *, batch, heads, dk, dv):
    n = q.shape[0]
    seq = n // batch
    chunk = RET_CHUNK if seq % RET_CHUNK == 0 else seq
    tc = min(RET_STEP_TOKENS, seq)
    tc = tc if (tc % chunk == 0 and seq % tc == 0) else chunk
    nt = seq // tc
    blk = lambda b, h, t: (b * nt + t, h)
    vmem = (4 * _nbytes((tc, dk), BF16) + 6 * _nbytes((tc, dv), BF16) + 3 * _nbytes((dk, dv), F32)
            + 8 * _nbytes((chunk, dv), F32) + 2 * _nbytes((dk, dv), F32))
    return pl.pallas_call(
        functools.partial(_ret_prompt_body, chunk=chunk),
        grid=(batch, heads, nt),
        in_specs=[pl.BlockSpec(memory_space=pltpu.SMEM),
                  pl.BlockSpec((tc, dk), blk), pl.BlockSpec((tc, dk), blk),
                  pl.BlockSpec((tc, dv), blk), pl.BlockSpec((tc, dv), blk)],
        out_specs=[pl.BlockSpec((tc, dv), blk),
                   pl.BlockSpec((1, 1, dk, dv), lambda b, h, t: (b, h, 0, 0))],
        out_shape=[jax.ShapeDtypeStruct((n, heads * dv), BF16),
                   jax.ShapeDtypeStruct((batch, heads, dk, dv), F32)],
        scratch_shapes=[pltpu.VMEM((dk, dv), F32)],
        compiler_params=_params(("parallel", "parallel", "arbitrary"), vmem),
        name="ret_prompt",
    )(log_g, q, k, v, sg)


def _ret_sample_body(lg_ref, q_ref, k_ref, v_ref, sg_ref, s_in_ref, o_ref, s_out_ref, *, seq, heads, dk, dv):
    rows = q_ref.shape[0]
    group = rows // seq
    ri = lax.broadcasted_iota(jnp.int32, (rows, rows), 0)
    ci = lax.broadcasted_iota(jnp.int32, (rows, rows), 1)
    same = (ri // seq) == (ci // seq)
    rel = (ri - ci).astype(F32)
    row_v = lax.broadcasted_iota(jnp.int32, (rows, dv), 0)
    row_k = lax.broadcasted_iota(jnp.int32, (rows, dk), 0)
    for hh in range(heads):
        lg = lg_ref[hh]
        intra = jnp.where(same & (rel >= 0), jnp.exp(lg * jnp.maximum(rel, 0.0)), 0.0)
        q_decay = jnp.exp(lg * ((row_v % seq).astype(F32) + 1.0))
        k_decay = jnp.exp(lg * (seq - 1.0 - (row_k % seq).astype(F32)))
        chunk_decay = jnp.exp(jnp.full((1, dv), lg * seq, F32))
        q = q_ref[:, hh * dk:(hh + 1) * dk].astype(BF16)
        kf = k_ref[:, hh * dk:(hh + 1) * dk]
        v = v_ref[:, hh * dv:(hh + 1) * dv].astype(BF16)
        sc = _dot_nt(q, kf.astype(BF16)) * intra
        o = _dot(sc.astype(BF16), v)
        kd = kf * k_decay
        inter = jnp.zeros((rows, dv), F32)
        for g in range(group):
            s_prev = s_in_ref[g, hh]
            mine_v = (row_v // seq) == g
            mine_k = (row_k // seq) == g
            inter = jnp.where(mine_v, _dot(q, s_prev.astype(BF16)), inter)
            kd_g = jnp.where(mine_k, kd, 0.0).astype(BF16)
            s_out_ref[g, hh] = chunk_decay * s_prev + _dot_tn(kd_g, v)
        o = _rms_rows(o + q_decay * inter)
        o_ref[:, hh * dv:(hh + 1) * dv] = (o * sg_ref[:, hh * dv:(hh + 1) * dv]).astype(o_ref.dtype)


def _ret_sample(log_g, q, k, v, sg, state, *, seq, heads, dk, dv):
    n = q.shape[0]
    batch = n // seq
    group = SAMPLE_GROUP if batch % SAMPLE_GROUP == 0 else batch
    rows = group * seq
    row = lambda i: (i, 0)
    st = lambda i: (i, 0, 0, 0)
    vmem = (4 * _nbytes((group, heads, dk, dv), F32) + 8 * _nbytes((rows, heads * dv), F32)
            + 4 * _nbytes((dk, dv), F32))
    return pl.pallas_call(
        functools.partial(_ret_sample_body, seq=seq, heads=heads, dk=dk, dv=dv),
        grid=(batch // group,),
        in_specs=[pl.BlockSpec(memory_space=pltpu.SMEM),
                  pl.BlockSpec((rows, heads * dk), row), pl.BlockSpec((rows, heads * dk), row),
                  pl.BlockSpec((rows, heads * dv), row), pl.BlockSpec((rows, heads * dv), row),
                  pl.BlockSpec((group, heads, dk, dv), st)],
        out_specs=[pl.BlockSpec((rows, heads * dv), row), pl.BlockSpec((group, heads, dk, dv), st)],
        out_shape=[jax.ShapeDtypeStruct((n, heads * dv), F32),
                   jax.ShapeDtypeStruct((batch, heads, dk, dv), F32)],
        compiler_params=_params(("parallel",), vmem),
        name="ret_sample",
    )(log_g, q, k, v, sg, state)


def _out_ffn_body(o_ref, h_ref, wo_ref, g_post_ref, g_pre_ref, w1_ref, w2_ref, g_ffn_ref, y_ref, *, o_transposed):
    if o_transposed:
        a = _dot_tn(o_ref[0], wo_ref[...])
    else:
        a = _dot(o_ref[...].astype(BF16), wo_ref[...])
    h1 = h_ref[...] + _rms_rows(a) * g_post_ref[...]
    x = (_rms_rows(h1) * g_pre_ref[...]).astype(BF16)
    d_ff = w1_ref.shape[1]
    fc = min(COL_CHUNK, d_ff)
    acc = jnp.zeros(h1.shape, F32)
    for c in range(d_ff // fc):
        u = jnp.maximum(_dot(x, w1_ref[:, c * fc:(c + 1) * fc]), 0.0)
        acc = acc + _dot((u * u).astype(BF16), w2_ref[c * fc:(c + 1) * fc, :])
    y_ref[...] = h1 + _rms_rows(acc) * g_ffn_ref[...]


def _out_ffn(o, h, w_o, g_post, g_pre, w1, w2, g_ffn):
    n, d = h.shape
    kdim = w_o.shape[0]
    tm = min(TOKEN_TILE, n)
    row = lambda i: (i, 0)
    o_transposed = o.ndim == 3
    if o_transposed:
        tiles = o.shape[2] // tm
        o_spec = pl.BlockSpec((1, kdim, tm), lambda i: (i // tiles, 0, i % tiles))
    else:
        o_spec = pl.BlockSpec((tm, kdim), row)
    vmem = (2 * _nbytes((tm, kdim), o.dtype) + 4 * _nbytes((tm, d), F32) + _nbytes(w_o.shape, BF16)
            + _nbytes(w1.shape, BF16) + _nbytes(w2.shape, BF16) + 6 * _nbytes((tm, COL_CHUNK), F32))
    return pl.pallas_call(
        functools.partial(_out_ffn_body, o_transposed=o_transposed),
        grid=(n // tm,),
        in_specs=[o_spec, pl.BlockSpec((tm, d), row), _resident(w_o.shape),
                  _resident((1, d)), _resident((1, d)), _resident(w1.shape), _resident(w2.shape),
                  _resident((1, d))],
        out_specs=pl.BlockSpec((tm, d), row),
        out_shape=jax.ShapeDtypeStruct((n, d), F32),
        compiler_params=_params(("parallel",), vmem),
        name="out_ffn",
    )(o, h, w_o, g_post, g_pre, w1, w2, g_ffn)


def _partial_rope(x, c_tab, sa_tab, sb_tab):
    half = ROT_DIM // 2
    outs = []
    for j in range(x.shape[1] // LANES):
        s = x[:, j * LANES:(j + 1) * LANES]
        outs.append(s * c_tab + pltpu.roll(s, LANES - half, axis=1) * sa_tab + pltpu.roll(s, half, axis=1) * sb_tab)
    return outs


def _swa_in_body(h_ref, g_q_ref, g_kv_ref, wq_ref, wkv_ref, c_ref, sa_ref, sb_ref, q_ref, k_ref, v_ref):
    y = _rms_rows(h_ref[...])
    xq = (y * g_q_ref[...]).astype(BF16)
    xkv = (y * g_kv_ref[...]).astype(BF16)
    c_tab, sa_tab, sb_tab = c_ref[...], sa_ref[...], sb_ref[...]
    q = _dot(xq, wq_ref[...]) * (SWA_HEAD_DIM ** -0.5)
    for j, s in enumerate(_partial_rope(q, c_tab, sa_tab, sb_tab)):
        q_ref[:, j * LANES:(j + 1) * LANES] = s.astype(q_ref.dtype)
    kv = _dot(xkv, wkv_ref[...])
    kw = k_ref.shape[1]
    for j, s in enumerate(_partial_rope(kv[:, :kw], c_tab, sa_tab, sb_tab)):
        k_ref[:, j * LANES:(j + 1) * LANES] = s
    v_ref[...] = kv[:, kw:]


def _swa_in(h, g_q, g_kv, w_q, w_kv, c_tab, sa_tab, sb_tab, *, q_dtype):
    n, d = h.shape
    tm = min(TOKEN_TILE, n)
    qw = w_q.shape[1]
    kw = w_kv.shape[1] // 2
    pos_tiles = c_tab.shape[0] // tm
    row = lambda i: (i, 0)
    tab = lambda i: (i % pos_tiles, 0)
    vmem = (2 * _nbytes((tm, d), F32) + _nbytes(w_q.shape, BF16) + _nbytes(w_kv.shape, BF16)
            + 6 * _nbytes((tm, LANES), F32) + 2 * _nbytes((tm, qw), q_dtype) + 4 * _nbytes((tm, kw), F32)
            + 6 * _nbytes((tm, qw), F32))
    return pl.pallas_call(
        _swa_in_body,
        grid=(n // tm,),
        in_specs=[pl.BlockSpec((tm, d), row), _resident((1, d)), _resident((1, d)),
                  _resident(w_q.shape), _resident(w_kv.shape),
                  pl.BlockSpec((tm, LANES), tab), pl.BlockSpec((tm, LANES), tab), pl.BlockSpec((tm, LANES), tab)],
        out_specs=[pl.BlockSpec((tm, qw), row), pl.BlockSpec((tm, kw), row), pl.BlockSpec((tm, kw), row)],
        out_shape=[jax.ShapeDtypeStruct((n, qw), q_dtype), jax.ShapeDtypeStruct((n, kw), F32),
                   jax.ShapeDtypeStruct((n, kw), F32)],
        compiler_params=_params(("parallel",), vmem),
        name="swa_in",
    )(h, g_q, g_kv, w_q, w_kv, c_tab, sa_tab, sb_tab)


def _swa_in_t_body(h_ref, g_q_ref, g_kv_ref, wqt_ref, wk_ref, wvt_ref, wv_ref, cos_t_ref, sin_t_ref,
                   c_ref, sa_ref, sb_ref, qt_ref, k_ref, vt_ref, kwin_ref, vwin_ref, *, tiles):
    y = _rms_rows(h_ref[...])
    xq = (y * g_q_ref[...]).astype(BF16)
    xkv = (y * g_kv_ref[...]).astype(BF16)
    tm = xq.shape[0]
    hd = SWA_HEAD_DIM
    half = ROT_DIM // 2
    cos_t, sin_t = cos_t_ref[...], sin_t_ref[...]
    qt = _dot_nt(wqt_ref[...], xq) * (hd ** -0.5 * LOG2E)
    for hq in range(qt.shape[0] // hd):
        base = hq * hd
        x1 = qt[base:base + half]
        x2 = qt[base + half:base + 2 * half]
        rot = jnp.concatenate([x1 * cos_t - x2 * sin_t, x2 * cos_t + x1 * sin_t], axis=0)
        qt_ref[0, base:base + 2 * half, :] = rot.astype(qt_ref.dtype)
        qt_ref[0, base + 2 * half:base + hd, :] = qt[base + 2 * half:base + hd].astype(qt_ref.dtype)
    k_rot = _partial_rope(_dot(xkv, wk_ref[...]), c_ref[...], sa_ref[...], sb_ref[...])
    for j, s in enumerate(k_rot):
        k_ref[:, j * LANES:(j + 1) * LANES] = s.astype(k_ref.dtype)
    vt_ref[0] = _dot_nt(wvt_ref[...], xkv).astype(vt_ref.dtype)
    win = kwin_ref.shape[0]

    @pl.when(pl.program_id(0) % tiles == tiles - 1)
    def _():
        for j, s in enumerate(k_rot):
            kwin_ref[:, j * LANES:(j + 1) * LANES] = s[tm - win:, :]
        vwin_ref[...] = _dot(xkv[tm - win:, :], wv_ref[...])


def _swa_in_t(h, g_q, g_kv, w_q, w_kv, cos_t, sin_t, c_tab, sa_tab, sb_tab, *, batch, win):
    n, d = h.shape
    seq = n // batch
    tm = min(TOKEN_TILE, seq)
    tiles = seq // tm
    qw = w_q.shape[1]
    kw = w_kv.shape[1] // 2
    wqt = w_q.T
    wk, wv = w_kv[:, :kw], w_kv[:, kw:]
    wvt = wv.T
    row = lambda i: (i, 0)
    tab = lambda i: (i % tiles, 0)
    tab_t = lambda i: (0, i % tiles)
    feat = lambda i: (i // tiles, 0, i % tiles)
    per_seq = lambda i: (i // tiles, 0)
    half = ROT_DIM // 2
    vmem = (2 * _nbytes((tm, d), F32) + 2 * _nbytes(w_q.shape, BF16) + 3 * _nbytes(w_kv.shape, BF16)
            + 8 * _nbytes((tm, LANES), F32) + 2 * _nbytes((tm, qw + 2 * kw), BF16) + 4 * _nbytes((win, kw), F32)
            + 4 * _nbytes((tm, qw), F32))
    return pl.pallas_call(
        functools.partial(_swa_in_t_body, tiles=tiles),
        grid=(n // tm,),
        in_specs=[pl.BlockSpec((tm, d), row), _resident((1, d)), _resident((1, d)),
                  _resident(wqt.shape), _resident(wk.shape), _resident(wvt.shape), _resident(wv.shape),
                  pl.BlockSpec((half, tm), tab_t), pl.BlockSpec((half, tm), tab_t),
                  pl.BlockSpec((tm, LANES), tab), pl.BlockSpec((tm, LANES), tab), pl.BlockSpec((tm, LANES), tab)],
        out_specs=[pl.BlockSpec((1, qw, tm), feat), pl.BlockSpec((tm, kw), row), pl.BlockSpec((1, kw, tm), feat),
                   pl.BlockSpec((win, kw), per_seq), pl.BlockSpec((win, kw), per_seq)],
        out_shape=[jax.ShapeDtypeStruct((batch, qw, seq), BF16), jax.ShapeDtypeStruct((n, kw), BF16),
                   jax.ShapeDtypeStruct((batch, kw, seq), BF16),
                   jax.ShapeDtypeStruct((batch * win, kw), F32), jax.ShapeDtypeStruct((batch * win, kw), F32)],
        compiler_params=_params(("arbitrary",), vmem),
        name="swa_in_t",
    )(h, g_q, g_kv, wqt, wk, wvt, wv, cos_t, sin_t, c_tab, sa_tab, sb_tab)


def _sink_rows(sinks_ref, kvh, group, rows_per_head, shape):
    r = lax.broadcasted_iota(jnp.int32, shape, 0) // rows_per_head
    col = jnp.full(shape, sinks_ref[kvh * group], F32)
    for g in range(1, group):
        col = jnp.where(r == g, sinks_ref[kvh * group + g], col)
    return col


def _attn_prompt_body(sinks_ref, mask_ref, qt_ref, kp_ref, kc_ref, vtp_ref, vtc_ref, ot_ref, *, group):
    blk = kp_ref.shape[0]
    nblk = kc_ref.shape[0] // blk
    hd = SWA_HEAD_DIM
    kvh_n = kc_ref.shape[1] // hd
    cols = group * blk
    kj = lax.broadcasted_iota(jnp.int32, (blk, cols), 0)
    qi = lax.broadcasted_iota(jnp.int32, (blk, cols), 1) % blk
    own = kj <= qi
    lane_head = lax.broadcasted_iota(jnp.int32, (1, cols), 1) // blk
    ones_rows = jnp.ones((BF16_SUBLANES, blk), BF16)
    has_prev = pl.program_id(1) > 0

    def scores(j, kvh):
        tile, lo = divmod(kvh * hd, LANES)
        q4t = jnp.concatenate([qt_ref[0, (kvh * group + g) * hd:(kvh * group + g + 1) * hd, j * blk:(j + 1) * blk]
                               for g in range(group)], axis=1)
        rhs = jnp.concatenate([q4t if part * hd == lo else jnp.zeros_like(q4t) for part in range(LANES // hd)], axis=0)
        k_tile = slice(tile * LANES, (tile + 1) * LANES)
        k_prev = kp_ref[:, k_tile] if j == 0 else kc_ref[(j - 1) * blk:j * blk, k_tile]
        return _dot(kc_ref[j * blk:(j + 1) * blk, k_tile], rhs), _dot(k_prev, rhs)

    def finish(j, kvh, s_own, s_prev):
        if j == 0:
            s_prev = jnp.where(has_prev, s_prev, NEG)
        s = jnp.where(own, s_own, s_prev)
        sink = jnp.full((1, cols), sinks_ref[kvh * group] * LOG2E, F32)
        for g in range(1, group):
            sink = jnp.where(lane_head == g, sinks_ref[kvh * group + g] * LOG2E, sink)
        m = jnp.maximum(jnp.max(s, axis=0, keepdims=True), sink)
        e = jnp.exp2(s - m).astype(BF16)
        p_own = e * mask_ref[...]
        p = jnp.concatenate([p_own, e - p_own], axis=0)
        head_rows = slice(kvh * hd, (kvh + 1) * hd)
        vt_own = vtc_ref[0, head_rows, j * blk:(j + 1) * blk]
        vt_prev = vtp_ref[0, head_rows, :] if j == 0 else vtc_ref[0, head_rows, (j - 1) * blk:j * blk]
        vt = jnp.concatenate([jnp.concatenate([vt_own, ones_rows], axis=0),
                              jnp.concatenate([vt_prev, ones_rows], axis=0)], axis=1)
        acc = _dot(vt, p)
        denom = acc[hd:hd + 1, :] + jnp.exp2(sink - m)
        ot = acc[:hd] / denom
        for g in range(group):
            hq = kvh * group + g
            ot_ref[0, hq * hd:(hq + 1) * hd, j * blk:(j + 1) * blk] = ot[:, g * blk:(g + 1) * blk].astype(ot_ref.dtype)

    units = [(j, kvh) for j in range(nblk) for kvh in range(kvh_n)]
    pending = scores(*units[0])
    for idx, unit in enumerate(units):
        upcoming = scores(*units[idx + 1]) if idx + 1 < len(units) else None
        finish(*unit, *pending)
        pending = upcoming


def _attn_prompt(sinks, qt, k, vt):
    batch, qw, seq = qt.shape
    kw = k.shape[1]
    blk = WINDOW
    nblk = ATTN_BLOCKS_PER_STEP if seq % (ATTN_BLOCKS_PER_STEP * blk) == 0 else 1
    span = nblk * blk
    steps = seq // span
    group = qw // kw
    cols = group * blk
    own = (jnp.arange(blk)[:, None] <= (jnp.arange(cols) % blk)[None, :]).astype(BF16)
    cur_t = lambda b, i: (b, 0, i)
    prev_t = lambda b, i: (b, 0, jnp.maximum(i * nblk - 1, 0))
    cur = lambda b, i: (b * steps + i, 0)
    prev = lambda b, i: (b * steps * nblk + jnp.maximum(i * nblk - 1, 0), 0)
    vmem = (4 * _nbytes((qw, span), BF16) + 6 * _nbytes((span, kw), BF16) + 16 * _nbytes((blk, cols), F32))
    return pl.pallas_call(
        functools.partial(_attn_prompt_body, group=group),
        grid=(batch, steps),
        in_specs=[pl.BlockSpec(memory_space=pltpu.SMEM), _resident((blk, cols)), pl.BlockSpec((1, qw, span), cur_t),
                  pl.BlockSpec((blk, kw), prev), pl.BlockSpec((span, kw), cur),
                  pl.BlockSpec((1, kw, blk), prev_t), pl.BlockSpec((1, kw, span), cur_t)],
        out_specs=pl.BlockSpec((1, qw, span), cur_t),
        out_shape=jax.ShapeDtypeStruct((batch, qw, seq), BF16),
        compiler_params=_params(("parallel", "parallel"), vmem),
        name="attn_prompt",
    )(sinks, own, qt, k, k, vt, vt)


def _attn_sample_body(sinks_ref, q_ref, kn_ref, vn_ref, kc_ref, vc_ref, o_ref, kw_ref, vw_ref, *, seq, group, q_start):
    rows = q_ref.shape[0]
    bgroup = rows // seq
    win = kc_ref.shape[1]
    hd = SWA_HEAD_DIM
    kvh_n = kn_ref.shape[1] // hd
    srows = group * rows
    kn = kn_ref[...]
    vn = vn_ref[...]
    r_c = lax.broadcasted_iota(jnp.int32, (srows, win), 0) % rows
    c_c = lax.broadcasted_iota(jnp.int32, (srows, win), 1)
    rel_c = (r_c % seq) + win - c_c
    ok_c = (rel_c >= 0) & (rel_c < WINDOW) & (q_start - win + c_c >= 0)
    r_n = lax.broadcasted_iota(jnp.int32, (srows, rows), 0) % rows
    c_n = lax.broadcasted_iota(jnp.int32, (srows, rows), 1)
    rel_n = (r_n % seq) - (c_n % seq)
    ok_n = (rel_n >= 0) & (rel_n < WINDOW) & ((r_n // seq) == (c_n // seq))
    owner = (lax.broadcasted_iota(jnp.int32, (srows, hd), 0) % rows) // seq
    for kvh in range(kvh_n):
        cols = slice(kvh * hd, (kvh + 1) * hd)
        q = jnp.concatenate([q_ref[:, (kvh * group + g) * hd:(kvh * group + g + 1) * hd] for g in range(group)],
                            axis=0).astype(BF16)
        sink = _sink_rows(sinks_ref, kvh, group, rows, (srows, 1))
        s_n = jnp.where(ok_n, _dot_nt(q, kn[:, cols].astype(BF16)), NEG)
        m_n = jnp.max(s_n, axis=-1, keepdims=True)
        o = jnp.zeros((srows, hd), F32)
        for b in range(bgroup):
            s_c = jnp.where(ok_c, _dot_nt(q, kc_ref[b, :, cols].astype(BF16)), NEG)
            m = jnp.maximum(jnp.maximum(jnp.max(s_c, axis=-1, keepdims=True), m_n), sink)
            e_c = jnp.exp(s_c - m)
            e_n = jnp.exp(s_n - m)
            denom = jnp.sum(e_c, axis=-1, keepdims=True) + jnp.sum(e_n, axis=-1, keepdims=True) + jnp.exp(sink - m)
            o_b = (_dot(e_c.astype(BF16), vc_ref[b, :, cols].astype(BF16))
                   + _dot(e_n.astype(BF16), vn[:, cols].astype(BF16))) / denom
            o = jnp.where(owner == b, o_b, o)
        for g in range(group):
            hq = kvh * group + g
            o_ref[:, hq * hd:(hq + 1) * hd] = o[g * rows:(g + 1) * rows].astype(o_ref.dtype)
    for b in range(bgroup):
        kw_ref[b, 0:win - seq, :] = kc_ref[b, seq:win, :]
        kw_ref[b, win - seq:win, :] = kn[b * seq:(b + 1) * seq, :]
        vw_ref[b, 0:win - seq, :] = vc_ref[b, seq:win, :]
        vw_ref[b, win - seq:win, :] = vn[b * seq:(b + 1) * seq, :]


def _attn_sample(sinks, q, k_new, v_new, k_cache, v_cache, *, seq, q_start):
    n, qw = q.shape
    kw = k_new.shape[1]
    batch, win, _ = k_cache.shape
    group = qw // kw
    bgroup = SAMPLE_GROUP if batch % SAMPLE_GROUP == 0 else batch
    rows = bgroup * seq
    row = lambda i: (i, 0)
    cache = lambda i: (i, 0, 0)
    vmem = 8 * _nbytes((bgroup, win, kw), F32) + 8 * _nbytes((rows, qw), F32) + 16 * _nbytes((group * rows, win), F32)
    return pl.pallas_call(
        functools.partial(_attn_sample_body, seq=seq, group=group, q_start=q_start),
        grid=(batch // bgroup,),
        in_specs=[pl.BlockSpec(memory_space=pltpu.SMEM), pl.BlockSpec((rows, qw), row),
                  pl.BlockSpec((rows, kw), row), pl.BlockSpec((rows, kw), row),
                  pl.BlockSpec((bgroup, win, kw), cache), pl.BlockSpec((bgroup, win, kw), cache)],
        out_specs=[pl.BlockSpec((rows, qw), row), pl.BlockSpec((bgroup, win, kw), cache),
                   pl.BlockSpec((bgroup, win, kw), cache)],
        out_shape=[jax.ShapeDtypeStruct((n, qw), F32), jax.ShapeDtypeStruct((batch, win, kw), F32),
                   jax.ShapeDtypeStruct((batch, win, kw), F32)],
        compiler_params=_params(("parallel",), vmem),
        name="attn_sample",
    )(sinks, q, k_new, v_new, k_cache, v_cache)


def _ret_rope_tables(pos, dk):
    inv = 1.0 / (RET_ROPE_THETA ** jnp.linspace(0.0, 1.0, dk // 2, dtype=F32))
    ang = pos[:, None] * inv[None, :]
    return jnp.cos(ang), jnp.sin(ang)


def _swa_cos_sin(pos):
    half = ROT_DIM // 2
    inv = ROPE_THETA ** (-jnp.arange(half, dtype=F32) / half)
    ang = pos[:, None] * inv[None, :]
    return jnp.cos(ang), jnp.sin(ang)


def _swa_rope_tables(pos):
    half = ROT_DIM // 2
    cos, sin = _swa_cos_sin(pos)
    n = pos.shape[0]
    pad = jnp.zeros((n, SWA_HEAD_DIM - 2 * half), F32)
    c_head = jnp.concatenate([cos, cos, pad + 1.0], axis=1)
    sa_head = jnp.concatenate([-sin, jnp.zeros_like(sin), pad], axis=1)
    sb_head = jnp.concatenate([jnp.zeros_like(sin), sin, pad], axis=1)
    reps = LANES // SWA_HEAD_DIM
    return tuple(jnp.tile(t, (1, reps)) for t in (c_head, sa_head, sb_head))


def _tile_rows(tab, seq, n):
    tm = min(TOKEN_TILE, n)
    return tab if seq >= tm else jnp.tile(tab, (tm // seq, 1))


def kernel(x_prompt, x_sample, state_ret, cache_k_win, cache_v_win, ret_norm_pre, ret_w_in, ret_w_out, ret_norm_post, kv_norm, w_kv, swa_norm_pre, swa_w_q, swa_sinks, swa_w_o, swa_norm_post, ffn_norm_pre, ffn_w1, ffn_w2, ffn_norm_post):
    n_a = DEPTH // 2
    assert n_a == 1 and DEPTH == 2, "one retention layer followed by one sliding-window layer"
    d = x_prompt.shape[-1]
    heads = RET_HEADS
    dk = ret_w_out.shape[-1] // heads
    dv = ret_w_out.shape[-2] // heads
    kvh, hd = SWA_KV_HEADS, SWA_HEAD_DIM
    row2 = lambda g: g.reshape(1, d)
    log_g = jnp.log1p(-jnp.exp2(-5.0 - jnp.arange(heads, dtype=F32)))

    w_in = ret_w_in[0].astype(BF16)
    w_out = ret_w_out[0].astype(BF16)
    wq = swa_w_q[0].astype(BF16)
    wkv = w_kv.astype(BF16)
    wo = swa_w_o[0].astype(BF16)
    w1 = ffn_w1.astype(BF16)
    w2 = ffn_w2.astype(BF16)
    sinks = swa_sinks[0]

    def trunk(x, pos, ret_core, swa_mixer, act_dtype):
        b, t, _ = x.shape
        n = b * t
        h = x.reshape(n, d)
        cos, sin = (_tile_rows(tab, t, n) for tab in _ret_rope_tables(pos, dk))
        q, k, v, sg = _ret_in(h, row2(ret_norm_pre[0]), w_in, cos, sin, heads=heads, dk=dk, dv=dv, out_dtype=act_dtype)
        o, state = ret_core(q, k, v, sg)
        h = _out_ffn(o, h, w_out, row2(ret_norm_post[0]), row2(ffn_norm_pre[0]), w1[0], w2[0], row2(ffn_norm_post[0]))
        o, k_win, v_win = swa_mixer(h, pos, b, t)
        h = _out_ffn(o, h, wo, row2(swa_norm_post[0]), row2(ffn_norm_pre[1]), w1[1], w2[1], row2(ffn_norm_post[1]))
        return h.reshape(b, t, d), state, k_win, v_win

    b_p, t_p, _ = x_prompt.shape
    w_p = min(WINDOW, t_p)

    def swa_prompt(h, pos, b, t):
        cos, sin = _swa_cos_sin(pos)
        qt, k, vt, k_win, v_win = _swa_in_t(h, row2(swa_norm_pre[0]), row2(kv_norm), wq, wkv, cos.T, sin.T,
                                            *_swa_rope_tables(pos), batch=b, win=w_p)
        return _attn_prompt(sinks, qt, k, vt), k_win, v_win

    y_prompt, state_p, k_win_p, v_win_p = trunk(
        x_prompt, jnp.arange(t_p, dtype=F32),
        lambda q, k, v, sg: _ret_prompt(log_g, q, k, v, sg, batch=b_p, heads=heads, dk=dk, dv=dv),
        swa_prompt, BF16)

    b_s, t_s, _ = x_sample.shape
    w_s = cache_k_win.shape[1]
    kc = cache_k_win.reshape(b_s, w_s, kvh * hd)
    vc = cache_v_win.reshape(b_s, w_s, kvh * hd)

    def swa_sample(h, pos, b, t):
        tabs = tuple(_tile_rows(tab, t, b * t) for tab in _swa_rope_tables(pos))
        q, k, v = _swa_in(h, row2(swa_norm_pre[0]), row2(kv_norm), wq, wkv, *tabs, q_dtype=F32)
        return _attn_sample(sinks, q, k, v, kc, vc, seq=t, q_start=PAST_LEN)

    y_sample, state_s, k_win_s, v_win_s = trunk(
        x_sample, PAST_LEN + jnp.arange(t_s, dtype=F32),
        lambda q, k, v, sg: _ret_sample(log_g, q, k, v, sg, state_ret[0], seq=t_s, heads=heads, dk=dk, dv=dv),
        swa_sample, F32)

    return (y_prompt, y_sample, state_p[None], state_s[None],
            k_win_p.reshape(b_p, w_p, kvh, hd), v_win_p.reshape(b_p, w_p, kvh, hd),
            k_win_s.reshape(b_s, w_s, kvh, hd), v_win_s.reshape(b_s, w_s, kvh, hd))
```

```python
import functools

import jax
import jax.numpy as jnp
from jax import lax
from jax.experimental import pallas as pl
from jax.experimental.pallas import tpu as pltpu

DEPTH = 2
PAST_LEN = 16384
RET_HEADS = 4
RET_ROPE_THETA = 10000.0
SWA_HEAD_DIM = 64
SWA_KV_HEADS = 4
WINDOW = 128
ROPE_THETA = 500000.0
ROT_DIM = SWA_HEAD_DIM // 4
EPS = 1e-6
NEG = -1e30
LOG2E = 1.4426950408889634

LANES = 128
SUBLANES = 8
BF16_SUBLANES = 16
VMEM_CAP_BYTES = 64 * 1024 * 1024
VMEM_BUDGET_BYTES = VMEM_CAP_BYTES - 8 * 1024 * 1024

TOKEN_TILE = 512
COL_CHUNK = 1024
RET_KERNEL_CHUNK = 256
RET_STEP_TOKENS = 1024
SAMPLE_GROUP = 2
ATTN_BLOCKS_PER_STEP = 4

F32 = jnp.float32
BF16 = jnp.bfloat16


def _params(semantics, vmem_bytes):
    limit = int(min(max(vmem_bytes, 16 * 1024 * 1024), VMEM_BUDGET_BYTES))
    return pltpu.CompilerParams(dimension_semantics=semantics, vmem_limit_bytes=limit)


def _resident(shape):
    nd = len(shape)
    return pl.BlockSpec(shape, lambda *_: (0,) * nd, pipeline_mode=pl.Buffered(1))


def _nbytes(shape, dtype):
    n = 1
    for s in shape:
        n *= s
    return n * jnp.dtype(dtype).itemsize


def _rms_rows(x):
    return x * lax.rsqrt(jnp.mean(x * x, axis=-1, keepdims=True) + EPS)


def _dot(a, b):
    return jnp.dot(a, b, preferred_element_type=F32)


def _dot_nt(a, b):
    return lax.dot_general(a, b, (((1,), (1,)), ((), ())), preferred_element_type=F32)


def _dot_tn(a, b):
    return lax.dot_general(a, b, (((0,), (0,)), ((), ())), preferred_element_type=F32)


def _ret_in_body(h_ref, g_ref, w_ref, cos_ref, sin_ref, kscale_ref, q_ref, kd_ref, v_ref, sg_ref, *, heads, dk, dv):
    xn = (_rms_rows(h_ref[...]) * g_ref[...]).astype(BF16)
    cos = cos_ref[...]
    sin = sin_ref[...]
    half = dk // 2
    qk_w = heads * dk
    v_w = heads * dv

    def proj(lo, width):
        return _dot(xn, w_ref[:, lo:lo + width])

    for base, ref, scale_ref in ((0, q_ref, None), (qk_w, kd_ref, kscale_ref)):
        p = proj(base, qk_w)
        for hh in range(heads):
            lo, mid, hi = hh * dk, hh * dk + half, (hh + 1) * dk
            x1 = p[:, lo:mid]
            x2 = p[:, mid:hi]
            o1 = x1 * cos - x2 * sin
            o2 = x2 * cos + x1 * sin
            if scale_ref is not None:
                o1 = o1 * scale_ref[:, lo:mid]
                o2 = o2 * scale_ref[:, mid:hi]
            ref[:, lo:mid] = o1.astype(ref.dtype)
            ref[:, mid:hi] = o2.astype(ref.dtype)
    cw = min(COL_CHUNK, v_w)
    for c in range(v_w // cw):
        v_ref[:, c * cw:(c + 1) * cw] = proj(2 * qk_w + c * cw, cw).astype(v_ref.dtype)
    for c in range(v_w // cw):
        gate = proj(2 * qk_w + v_w + c * cw, cw)
        sg_ref[:, c * cw:(c + 1) * cw] = (gate * jax.nn.sigmoid(gate)).astype(sg_ref.dtype)


def _ret_in(h, g, w_in, cos, sin, kscale, *, heads, dk, dv, out_dtype):
    n, d = h.shape
    tm = min(TOKEN_TILE, n)
    qk_w, v_w = heads * dk, heads * dv
    pos_tiles = cos.shape[0] // tm
    row = lambda i: (i, 0)
    tab = lambda i: (i % pos_tiles, 0)
    vmem = (2 * _nbytes((tm, d), F32) + _nbytes(w_in.shape, BF16) + 4 * _nbytes((tm, dk // 2), F32)
            + _nbytes((tm, qk_w), F32) + 2 * _nbytes((tm, 2 * qk_w + 2 * v_w), out_dtype)
            + 4 * _nbytes((tm, COL_CHUNK), F32))
    return pl.pallas_call(
        functools.partial(_ret_in_body, heads=heads, dk=dk, dv=dv),
        grid=(n // tm,),
        in_specs=[pl.BlockSpec((tm, d), row), _resident((1, d)), _resident(w_in.shape),
                  pl.BlockSpec((tm, dk // 2), tab), pl.BlockSpec((tm, dk // 2), tab), _resident((tm, qk_w))],
        out_specs=[pl.BlockSpec((tm, qk_w), row), pl.BlockSpec((tm, qk_w), row),
                   pl.BlockSpec((tm, v_w), row), pl.BlockSpec((tm, v_w), row)],
        out_shape=[jax.ShapeDtypeStruct((n, qk_w), out_dtype), jax.ShapeDtypeStruct((n, qk_w), out_dtype),
                   jax.ShapeDtypeStruct((n, v_w), out_dtype), jax.ShapeDtypeStruct((n, v_w), out_dtype)],
        compiler_params=_params(("parallel",), vmem),
        name="ret_in",
    )(h, g, w_in, cos, sin, kscale)


def _ret_prompt_body(lg_ref, q_ref, kd_ref, v_ref, sg_ref, o_ref, s_out_ref, s_ref, *, chunk):
    hh = pl.program_id(1)
    t = pl.program_id(2)
    lg = lg_ref[hh]
    tc, dk = q_ref.shape
    dv = v_ref.shape[1]

    @pl.when(t == 0)
    def _():
        s_ref[...] = jnp.zeros_like(s_ref)

    ri = lax.broadcasted_iota(jnp.int32, (chunk, chunk), 0)
    ci = lax.broadcasted_iota(jnp.int32, (chunk, chunk), 1)
    causal = jnp.where(ri >= ci, jnp.exp(jnp.full((chunk, chunk), -lg * chunk, F32)), 0.0)
    row_v = lax.broadcasted_iota(jnp.int32, (chunk, dv), 0).astype(F32)
    q_decay = jnp.exp(lg * (row_v + 1.0))
    chunk_decay = jnp.exp(jnp.full((1, dv), lg * chunk, F32))

    def scores(c):
        rows = pl.ds(c * chunk, chunk)
        return _dot_nt(q_ref[rows, :], kd_ref[rows, :])

    qk = scores(0)
    for c in range(tc // chunk):
        rows = pl.ds(c * chunk, chunk)
        qk_next = scores(c + 1) if (c + 1) * chunk < tc else None
        v = v_ref[rows, :]
        grow = _dot_tn(kd_ref[rows, :], v)
        s_prev = s_ref[...]
        lhs = jnp.concatenate([(qk * causal).astype(BF16), q_ref[rows, :]], axis=1)
        rhs = jnp.concatenate([v, s_prev.astype(BF16)], axis=0)
        o = _rms_rows(q_decay * _dot(lhs, rhs))
        s_ref[...] = chunk_decay * s_prev + grow
        o_ref[rows, :] = (o * sg_ref[rows, :].astype(F32)).astype(o_ref.dtype)
        qk = qk_next

    @pl.when(t == pl.num_programs(2) - 1)
    def _():
        s_out_ref[0, 0] = s_ref[...]


def _ret_chunk(seq):
    return RET_KERNEL_CHUNK if seq % RET_KERNEL_CHUNK == 0 else seq


def _ret_prompt(log_g, q, kd, v, sg, *, batch, heads, dk, dv):
    n = q.shape[0]
    seq = n // batch
    chunk = _ret_chunk(seq)
    tc = min(RET_STEP_TOKENS, seq)
    tc = tc if (tc % chunk == 0 and seq % tc == 0) else chunk
    nt = seq // tc
    blk = lambda b, h, t: (b * nt + t, h)
    vmem = (4 * _nbytes((tc, dk), BF16) + 6 * _nbytes((tc, dv), BF16) + 3 * _nbytes((dk, dv), F32)
            + 8 * _nbytes((chunk, dv), F32) + 4 * _nbytes((chunk, chunk), F32) + 2 * _nbytes((dk, dv), F32))
    return pl.pallas_call(
        functools.partial(_ret_prompt_body, chunk=chunk),
        grid=(batch, heads, nt),
        in_specs=[pl.BlockSpec(memory_space=pltpu.SMEM),
                  pl.BlockSpec((tc, dk), blk), pl.BlockSpec((tc, dk), blk),
                  pl.BlockSpec((tc, dv), blk), pl.BlockSpec((tc, dv), blk)],
        out_specs=[pl.BlockSpec((tc, dv), blk),
                   pl.BlockSpec((1, 1, dk, dv), lambda b, h, t: (b, h, 0, 0))],
        out_shape=[jax.ShapeDtypeStruct((n, heads * dv), BF16),
                   jax.ShapeDtypeStruct((batch, heads, dk, dv), F32)],
        scratch_shapes=[pltpu.VMEM((dk, dv), F32)],
        compiler_params=_params(("parallel", "parallel", "arbitrary"), vmem),
        name="ret_prompt",
    )(log_g, q, kd, v, sg)


def _ret_sample_body(lg_ref, q_ref, kd_ref, v_ref, sg_ref, s_in_ref, o_ref, s_out_ref, *, seq, heads, dk, dv):
    rows = q_ref.shape[0]
    group = rows // seq
    ri = lax.broadcasted_iota(jnp.int32, (rows, rows), 0)
    ci = lax.broadcasted_iota(jnp.int32, (rows, rows), 1)
    visible = ((ri // seq) == (ci // seq)) & (ri >= ci)
    row_v = lax.broadcasted_iota(jnp.int32, (rows, dv), 0)
    row_k = lax.broadcasted_iota(jnp.int32, (rows, dk), 0)
    for hh in range(heads):
        lg = lg_ref[hh]
        causal = jnp.where(visible, jnp.exp(jnp.full((rows, rows), -lg * seq, F32)), 0.0)
        q_decay = jnp.exp(lg * ((row_v % seq).astype(F32) + 1.0))
        chunk_decay = jnp.exp(jnp.full((1, dv), lg * seq, F32))
        q = q_ref[:, hh * dk:(hh + 1) * dk].astype(BF16)
        kd = kd_ref[:, hh * dk:(hh + 1) * dk]
        v = v_ref[:, hh * dv:(hh + 1) * dv].astype(BF16)
        o = _dot((_dot_nt(q, kd.astype(BF16)) * causal).astype(BF16), v)
        for g in range(group):
            s_prev = s_in_ref[g, hh]
            o = jnp.where((row_v // seq) == g, o + _dot(q, s_prev.astype(BF16)), o)
            kd_g = jnp.where((row_k // seq) == g, kd, 0.0).astype(BF16)
            s_out_ref[g, hh] = chunk_decay * s_prev + _dot_tn(kd_g, v)
        o = _rms_rows(q_decay * o)
        o_ref[:, hh * dv:(hh + 1) * dv] = (o * sg_ref[:, hh * dv:(hh + 1) * dv]).astype(o_ref.dtype)


def _ret_sample(log_g, q, kd, v, sg, state, *, seq, heads, dk, dv):
    n = q.shape[0]
    batch = n // seq
    group = SAMPLE_GROUP if batch % SAMPLE_GROUP == 0 else batch
    rows = group * seq
    row = lambda i: (i, 0)
    st = lambda i: (i, 0, 0, 0)
    vmem = (4 * _nbytes((group, heads, dk, dv), F32) + 8 * _nbytes((rows, heads * dv), F32)
            + 4 * _nbytes((dk, dv), F32))
    return pl.pallas_call(
        functools.partial(_ret_sample_body, seq=seq, heads=heads, dk=dk, dv=dv),
        grid=(batch // group,),
        in_specs=[pl.BlockSpec(memory_space=pltpu.SMEM),
                  pl.BlockSpec((rows, heads * dk), row), pl.BlockSpec((rows, heads * dk), row),
                  pl.BlockSpec((rows, heads * dv), row), pl.BlockSpec((rows, heads * dv), row),
                  pl.BlockSpec((group, heads, dk, dv), st)],
        out_specs=[pl.BlockSpec((rows, heads * dv), row), pl.BlockSpec((group, heads, dk, dv), st)],
        out_shape=[jax.ShapeDtypeStruct((n, heads * dv), F32),
                   jax.ShapeDtypeStruct((batch, heads, dk, dv), F32)],
        compiler_params=_params(("parallel",), vmem),
        name="ret_sample",
    )(log_g, q, kd, v, sg, state)


def _out_ffn_body(o_ref, h_ref, wo_ref, g_post_ref, g_pre_ref, w1_ref, w2_ref, g_ffn_ref, y_ref, *, o_transposed):
    if o_transposed:
        a = _dot_tn(o_ref[0], wo_ref[...])
    else:
        a = _dot(o_ref[...].astype(BF16), wo_ref[...])
    h1 = h_ref[...] + _rms_rows(a) * g_post_ref[...]
    x = (_rms_rows(h1) * g_pre_ref[...]).astype(BF16)
    d_ff = w1_ref.shape[1]
    fc = min(COL_CHUNK, d_ff)
    acc = jnp.zeros(h1.shape, F32)
    for c in range(d_ff // fc):
        u = jnp.maximum(_dot(x, w1_ref[:, c * fc:(c + 1) * fc]), 0.0)
        acc = acc + _dot((u * u).astype(BF16), w2_ref[c * fc:(c + 1) * fc, :])
    y_ref[...] = h1 + _rms_rows(acc) * g_ffn_ref[...]


def _out_ffn(o, h, w_o, g_post, g_pre, w1, w2, g_ffn):
    n, d = h.shape
    kdim = w_o.shape[0]
    tm = min(TOKEN_TILE, n)
    row = lambda i: (i, 0)
    o_transposed = o.ndim == 3
    if o_transposed:
        tiles = o.shape[2] // tm
        o_spec = pl.BlockSpec((1, kdim, tm), lambda i: (i // tiles, 0, i % tiles))
    else:
        o_spec = pl.BlockSpec((tm, kdim), row)
    vmem = (2 * _nbytes((tm, kdim), o.dtype) + 4 * _nbytes((tm, d), F32) + _nbytes(w_o.shape, BF16)
            + _nbytes(w1.shape, BF16) + _nbytes(w2.shape, BF16) + 6 * _nbytes((tm, COL_CHUNK), F32))
    return pl.pallas_call(
        functools.partial(_out_ffn_body, o_transposed=o_transposed),
        grid=(n // tm,),
        in_specs=[o_spec, pl.BlockSpec((tm, d), row), _resident(w_o.shape),
                  _resident((1, d)), _resident((1, d)), _resident(w1.shape), _resident(w2.shape),
                  _resident((1, d))],
        out_specs=pl.BlockSpec((tm, d), row),
        out_shape=jax.ShapeDtypeStruct((n, d), F32),
        compiler_params=_params(("parallel",), vmem),
        name="out_ffn",
    )(o, h, w_o, g_post, g_pre, w1, w2, g_ffn)


def _partial_rope(x, c_tab, sa_tab, sb_tab):
    half = ROT_DIM // 2
    outs = []
    for j in range(x.shape[1] // LANES):
        s = x[:, j * LANES:(j + 1) * LANES]
        outs.append(s * c_tab + pltpu.roll(s, LANES - half, axis=1) * sa_tab + pltpu.roll(s, half, axis=1) * sb_tab)
    return outs


def _swa_in_body(h_ref, g_q_ref, g_kv_ref, wq_ref, wkv_ref, c_ref, sa_ref, sb_ref, q_ref, k_ref, v_ref):
    y = _rms_rows(h_ref[...])
    xq = (y * g_q_ref[...]).astype(BF16)
    xkv = (y * g_kv_ref[...]).astype(BF16)
    c_tab, sa_tab, sb_tab = c_ref[...], sa_ref[...], sb_ref[...]
    q = _dot(xq, wq_ref[...]) * (SWA_HEAD_DIM ** -0.5)
    for j, s in enumerate(_partial_rope(q, c_tab, sa_tab, sb_tab)):
        q_ref[:, j * LANES:(j + 1) * LANES] = s.astype(q_ref.dtype)
    kv = _dot(xkv, wkv_ref[...])
    kw = k_ref.shape[1]
    for j, s in enumerate(_partial_rope(kv[:, :kw], c_tab, sa_tab, sb_tab)):
        k_ref[:, j * LANES:(j + 1) * LANES] = s
    v_ref[...] = kv[:, kw:]


def _swa_in(h, g_q, g_kv, w_q, w_kv, c_tab, sa_tab, sb_tab, *, q_dtype):
    n, d = h.shape
    tm = min(TOKEN_TILE, n)
    qw = w_q.shape[1]
    kw = w_kv.shape[1] // 2
    pos_tiles = c_tab.shape[0] // tm
    row = lambda i: (i, 0)
    tab = lambda i: (i % pos_tiles, 0)
    vmem = (2 * _nbytes((tm, d), F32) + _nbytes(w_q.shape, BF16) + _nbytes(w_kv.shape, BF16)
            + 6 * _nbytes((tm, LANES), F32) + 2 * _nbytes((tm, qw), q_dtype) + 4 * _nbytes((tm, kw), F32)
            + 6 * _nbytes((tm, qw), F32))
    return pl.pallas_call(
        _swa_in_body,
        grid=(n // tm,),
        in_specs=[pl.BlockSpec((tm, d), row), _resident((1, d)), _resident((1, d)),
                  _resident(w_q.shape), _resident(w_kv.shape),
                  pl.BlockSpec((tm, LANES), tab), pl.BlockSpec((tm, LANES), tab), pl.BlockSpec((tm, LANES), tab)],
        out_specs=[pl.BlockSpec((tm, qw), row), pl.BlockSpec((tm, kw), row), pl.BlockSpec((tm, kw), row)],
        out_shape=[jax.ShapeDtypeStruct((n, qw), q_dtype), jax.ShapeDtypeStruct((n, kw), F32),
                   jax.ShapeDtypeStruct((n, kw), F32)],
        compiler_params=_params(("parallel",), vmem),
        name="swa_in",
    )(h, g_q, g_kv, w_q, w_kv, c_tab, sa_tab, sb_tab)


def _swa_in_t_body(h_ref, g_q_ref, g_kv_ref, wqt_ref, wk_ref, wvt_ref, wv_ref, cos_t_ref, sin_t_ref,
                   c_ref, sa_ref, sb_ref, qt_ref, k_ref, vt_ref, kwin_ref, vwin_ref, *, tiles):
    y = _rms_rows(h_ref[...])
    xq = (y * g_q_ref[...]).astype(BF16)
    xkv = (y * g_kv_ref[...]).astype(BF16)
    tm = xq.shape[0]
    hd = SWA_HEAD_DIM
    half = ROT_DIM // 2
    cos_t, sin_t = cos_t_ref[...], sin_t_ref[...]
    qt = _dot_nt(wqt_ref[...], xq) * (hd ** -0.5 * LOG2E)
    for hq in range(qt.shape[0] // hd):
        base = hq * hd
        x1 = qt[base:base + half]
        x2 = qt[base + half:base + 2 * half]
        rot = jnp.concatenate([x1 * cos_t - x2 * sin_t, x2 * cos_t + x1 * sin_t], axis=0)
        qt_ref[0, base:base + 2 * half, :] = rot.astype(qt_ref.dtype)
        qt_ref[0, base + 2 * half:base + hd, :] = qt[base + 2 * half:base + hd].astype(qt_ref.dtype)
    k_rot = _partial_rope(_dot(xkv, wk_ref[...]), c_ref[...], sa_ref[...], sb_ref[...])
    for j, s in enumerate(k_rot):
        k_ref[:, j * LANES:(j + 1) * LANES] = s.astype(k_ref.dtype)
    vt_ref[0] = _dot_nt(wvt_ref[...], xkv).astype(vt_ref.dtype)
    win = kwin_ref.shape[0]

    @pl.when(pl.program_id(0) % tiles == tiles - 1)
    def _():
        for j, s in enumerate(k_rot):
            kwin_ref[:, j * LANES:(j + 1) * LANES] = s[tm - win:, :]
        vwin_ref[...] = _dot(xkv[tm - win:, :], wv_ref[...])


def _swa_in_t(h, g_q, g_kv, w_q, w_kv, cos_t, sin_t, c_tab, sa_tab, sb_tab, *, batch, win):
    n, d = h.shape
    seq = n // batch
    tm = min(TOKEN_TILE, seq)
    tiles = seq // tm
    qw = w_q.shape[1]
    kw = w_kv.shape[1] // 2
    wqt = w_q.T
    wk, wv = w_kv[:, :kw], w_kv[:, kw:]
    wvt = wv.T
    row = lambda i: (i, 0)
    tab = lambda i: (i % tiles, 0)
    tab_t = lambda i: (0, i % tiles)
    feat = lambda i: (i // tiles, 0, i % tiles)
    per_seq = lambda i: (i // tiles, 0)
    half = ROT_DIM // 2
    vmem = (2 * _nbytes((tm, d), F32) + 2 * _nbytes(w_q.shape, BF16) + 3 * _nbytes(w_kv.shape, BF16)
            + 8 * _nbytes((tm, LANES), F32) + 2 * _nbytes((tm, qw + 2 * kw), BF16) + 4 * _nbytes((win, kw), F32)
            + 4 * _nbytes((tm, qw), F32))
    return pl.pallas_call(
        functools.partial(_swa_in_t_body, tiles=tiles),
        grid=(n // tm,),
        in_specs=[pl.BlockSpec((tm, d), row), _resident((1, d)), _resident((1, d)),
                  _resident(wqt.shape), _resident(wk.shape), _resident(wvt.shape), _resident(wv.shape),
                  pl.BlockSpec((half, tm), tab_t), pl.BlockSpec((half, tm), tab_t),
                  pl.BlockSpec((tm, LANES), tab), pl.BlockSpec((tm, LANES), tab), pl.BlockSpec((tm, LANES), tab)],
        out_specs=[pl.BlockSpec((1, qw, tm), feat), pl.BlockSpec((tm, kw), row), pl.BlockSpec((1, kw, tm), feat),
                   pl.BlockSpec((win, kw), per_seq), pl.BlockSpec((win, kw), per_seq)],
        out_shape=[jax.ShapeDtypeStruct((batch, qw, seq), BF16), jax.ShapeDtypeStruct((n, kw), BF16),
                   jax.ShapeDtypeStruct((batch, kw, seq), BF16),
                   jax.ShapeDtypeStruct((batch * win, kw), F32), jax.ShapeDtypeStruct((batch * win, kw), F32)],
        compiler_params=_params(("arbitrary",), vmem),
        name="swa_in_t",
    )(h, g_q, g_kv, wqt, wk, wvt, wv, cos_t, sin_t, c_tab, sa_tab, sb_tab)


def _sink_rows(sinks_ref, kvh, group, rows_per_head, shape):
    r = lax.broadcasted_iota(jnp.int32, shape, 0) // rows_per_head
    col = jnp.full(shape, sinks_ref[kvh * group], F32)
    for g in range(1, group):
        col = jnp.where(r == g, sinks_ref[kvh * group + g], col)
    return col


def _attn_prompt_body(sinks_ref, mask_ref, qt_ref, kp_ref, kc_ref, vtp_ref, vtc_ref, ot_ref, *, group):
    blk = kp_ref.shape[0]
    nblk = kc_ref.shape[0] // blk
    hd = SWA_HEAD_DIM
    kvh_n = kc_ref.shape[1] // hd
    cols = group * blk
    kj = lax.broadcasted_iota(jnp.int32, (blk, cols), 0)
    qi = lax.broadcasted_iota(jnp.int32, (blk, cols), 1) % blk
    own = kj <= qi
    lane_head = lax.broadcasted_iota(jnp.int32, (1, cols), 1) // blk
    ones_rows = jnp.ones((BF16_SUBLANES, blk), BF16)
    has_prev = pl.program_id(1) > 0

    def scores(j, kvh):
        tile, lo = divmod(kvh * hd, LANES)
        q4t = jnp.concatenate([qt_ref[0, (kvh * group + g) * hd:(kvh * group + g + 1) * hd, j * blk:(j + 1) * blk]
                               for g in range(group)], axis=1)
        rhs = jnp.concatenate([q4t if part * hd == lo else jnp.zeros_like(q4t) for part in range(LANES // hd)], axis=0)
        k_tile = slice(tile * LANES, (tile + 1) * LANES)
        k_prev = kp_ref[:, k_tile] if j == 0 else kc_ref[(j - 1) * blk:j * blk, k_tile]
        return _dot(kc_ref[j * blk:(j + 1) * blk, k_tile], rhs), _dot(k_prev, rhs)

    def finish(j, kvh, s_own, s_prev):
        if j == 0:
            s_prev = jnp.where(has_prev, s_prev, NEG)
        s = jnp.where(own, s_own, s_prev)
        sink = jnp.full((1, cols), sinks_ref[kvh * group] * LOG2E, F32)
        for g in range(1, group):
            sink = jnp.where(lane_head == g, sinks_ref[kvh * group + g] * LOG2E, sink)
        m = jnp.maximum(jnp.max(s, axis=0, keepdims=True), sink)
        e = jnp.exp2(s - m).astype(BF16)
        p_own = e * mask_ref[...]
        p = jnp.concatenate([p_own, e - p_own], axis=0)
        head_rows = slice(kvh * hd, (kvh + 1) * hd)
        vt_own = vtc_ref[0, head_rows, j * blk:(j + 1) * blk]
        vt_prev = vtp_ref[0, head_rows, :] if j == 0 else vtc_ref[0, head_rows, (j - 1) * blk:j * blk]
        vt = jnp.concatenate([jnp.concatenate([vt_own, ones_rows], axis=0),
                              jnp.concatenate([vt_prev, ones_rows], axis=0)], axis=1)
        acc = _dot(vt, p)
        denom = acc[hd:hd + 1, :] + jnp.exp2(sink - m)
        ot = acc[:hd] / denom
        for g in range(group):
            hq = kvh * group + g
            ot_ref[0, hq * hd:(hq + 1) * hd, j * blk:(j + 1) * blk] = ot[:, g * blk:(g + 1) * blk].astype(ot_ref.dtype)

    units = [(j, kvh) for j in range(nblk) for kvh in range(kvh_n)]
    pending = scores(*units[0])
    for idx, unit in enumerate(units):
        upcoming = scores(*units[idx + 1]) if idx + 1 < len(units) else None
        finish(*unit, *pending)
        pending = upcoming


def _attn_prompt(sinks, qt, k, vt):
    batch, qw, seq = qt.shape
    kw = k.shape[1]
    blk = WINDOW
    nblk = ATTN_BLOCKS_PER_STEP if seq % (ATTN_BLOCKS_PER_STEP * blk) == 0 else 1
    span = nblk * blk
    steps = seq // span
    group = qw // kw
    cols = group * blk
    own = (jnp.arange(blk)[:, None] <= (jnp.arange(cols) % blk)[None, :]).astype(BF16)
    cur_t = lambda b, i: (b, 0, i)
    prev_t = lambda b, i: (b, 0, jnp.maximum(i * nblk - 1, 0))
    cur = lambda b, i: (b * steps + i, 0)
    prev = lambda b, i: (b * steps * nblk + jnp.maximum(i * nblk - 1, 0), 0)
    vmem = (4 * _nbytes((qw, span), BF16) + 6 * _nbytes((span, kw), BF16) + 16 * _nbytes((blk, cols), F32))
    return pl.pallas_call(
        functools.partial(_attn_prompt_body, group=group),
        grid=(batch, steps),
        in_specs=[pl.BlockSpec(memory_space=pltpu.SMEM), _resident((blk, cols)), pl.BlockSpec((1, qw, span), cur_t),
                  pl.BlockSpec((blk, kw), prev), pl.BlockSpec((span, kw), cur),
                  pl.BlockSpec((1, kw, blk), prev_t), pl.BlockSpec((1, kw, span), cur_t)],
        out_specs=pl.BlockSpec((1, qw, span), cur_t),
        out_shape=jax.ShapeDtypeStruct((batch, qw, seq), BF16),
        compiler_params=_params(("parallel", "parallel"), vmem),
        name="attn_prompt",
    )(sinks, own, qt, k, k, vt, vt)


def _attn_sample_body(sinks_ref, q_ref, kn_ref, vn_ref, kc_ref, vc_ref, o_ref, kw_ref, vw_ref, *, seq, group, q_start):
    rows = q_ref.shape[0]
    bgroup = rows // seq
    win = kc_ref.shape[1]
    hd = SWA_HEAD_DIM
    kvh_n = kn_ref.shape[1] // hd
    srows = group * rows
    kn = kn_ref[...]
    vn = vn_ref[...]
    r_c = lax.broadcasted_iota(jnp.int32, (srows, win), 0) % rows
    c_c = lax.broadcasted_iota(jnp.int32, (srows, win), 1)
    rel_c = (r_c % seq) + win - c_c
    ok_c = (rel_c >= 0) & (rel_c < WINDOW) & (q_start - win + c_c >= 0)
    r_n = lax.broadcasted_iota(jnp.int32, (srows, rows), 0) % rows
    c_n = lax.broadcasted_iota(jnp.int32, (srows, rows), 1)
    rel_n = (r_n % seq) - (c_n % seq)
    ok_n = (rel_n >= 0) & (rel_n < WINDOW) & ((r_n // seq) == (c_n // seq))
    owner = (lax.broadcasted_iota(jnp.int32, (srows, hd), 0) % rows) // seq
    for kvh in range(kvh_n):
        cols = slice(kvh * hd, (kvh + 1) * hd)
        q = jnp.concatenate([q_ref[:, (kvh * group + g) * hd:(kvh * group + g + 1) * hd] for g in range(group)],
                            axis=0).astype(BF16)
        sink = _sink_rows(sinks_ref, kvh, group, rows, (srows, 1))
        s_n = jnp.where(ok_n, _dot_nt(q, kn[:, cols].astype(BF16)), NEG)
        m_n = jnp.max(s_n, axis=-1, keepdims=True)
        o = jnp.zeros((srows, hd), F32)
        for b in range(bgroup):
            s_c = jnp.where(ok_c, _dot_nt(q, kc_ref[b, :, cols].astype(BF16)), NEG)
            m = jnp.maximum(jnp.maximum(jnp.max(s_c, axis=-1, keepdims=True), m_n), sink)
            e_c = jnp.exp(s_c - m)
            e_n = jnp.exp(s_n - m)
            denom = jnp.sum(e_c, axis=-1, keepdims=True) + jnp.sum(e_n, axis=-1, keepdims=True) + jnp.exp(sink - m)
            o_b = (_dot(e_c.astype(BF16), vc_ref[b, :, cols].astype(BF16))
                   + _dot(e_n.astype(BF16), vn[:, cols].astype(BF16))) / denom
            o = jnp.where(owner == b, o_b, o)
        for g in range(group):
            hq = kvh * group + g
            o_ref[:, hq * hd:(hq + 1) * hd] = o[g * rows:(g + 1) * rows].astype(o_ref.dtype)
    for b in range(bgroup):
        kw_ref[b, 0:win - seq, :] = kc_ref[b, seq:win, :]
        kw_ref[b, win - seq:win, :] = kn[b * seq:(b + 1) * seq, :]
        vw_ref[b, 0:win - seq, :] = vc_ref[b, seq:win, :]
        vw_ref[b, win - seq:win, :] = vn[b * seq:(b + 1) * seq, :]


def _attn_sample(sinks, q, k_new, v_new, k_cache, v_cache, *, seq, q_start):
    n, qw = q.shape
    kw = k_new.shape[1]
    batch, win, _ = k_cache.shape
    group = qw // kw
    bgroup = SAMPLE_GROUP if batch % SAMPLE_GROUP == 0 else batch
    rows = bgroup * seq
    row = lambda i: (i, 0)
    cache = lambda i: (i, 0, 0)
    vmem = 8 * _nbytes((bgroup, win, kw), F32) + 8 * _nbytes((rows, qw), F32) + 16 * _nbytes((group * rows, win), F32)
    return pl.pallas_call(
        functools.partial(_attn_sample_body, seq=seq, group=group, q_start=q_start),
        grid=(batch // bgroup,),
        in_specs=[pl.BlockSpec(memory_space=pltpu.SMEM), pl.BlockSpec((rows, qw), row),
                  pl.BlockSpec((rows, kw), row), pl.BlockSpec((rows, kw), row),
                  pl.BlockSpec((bgroup, win, kw), cache), pl.BlockSpec((bgroup, win, kw), cache)],
        out_specs=[pl.BlockSpec((rows, qw), row), pl.BlockSpec((bgroup, win, kw), cache),
                   pl.BlockSpec((bgroup, win, kw), cache)],
        out_shape=[jax.ShapeDtypeStruct((n, qw), F32), jax.ShapeDtypeStruct((batch, win, kw), F32),
                   jax.ShapeDtypeStruct((batch, win, kw), F32)],
        compiler_params=_params(("parallel",), vmem),
        name="attn_sample",
    )(sinks, q, k_new, v_new, k_cache, v_cache)


def _ret_rope_tables(pos, dk):
    inv = 1.0 / (RET_ROPE_THETA ** jnp.linspace(0.0, 1.0, dk // 2, dtype=F32))
    ang = pos[:, None] * inv[None, :]
    return jnp.cos(ang), jnp.sin(ang)


def _ret_key_scale(log_g, seq, n, dk):
    chunk = _ret_chunk(seq)
    tm = min(TOKEN_TILE, n)
    assert tm % chunk == 0
    left = (chunk - 1 - jnp.arange(tm) % chunk).astype(F32)
    per_head = jnp.exp(log_g[None, :] * left[:, None]) * dk ** -0.5
    return jnp.repeat(per_head, dk, axis=1)


def _swa_cos_sin(pos):
    half = ROT_DIM // 2
    inv = ROPE_THETA ** (-jnp.arange(half, dtype=F32) / half)
    ang = pos[:, None] * inv[None, :]
    return jnp.cos(ang), jnp.sin(ang)


def _swa_rope_tables(pos):
    half = ROT_DIM // 2
    cos, sin = _swa_cos_sin(pos)
    n = pos.shape[0]
    pad = jnp.zeros((n, SWA_HEAD_DIM - 2 * half), F32)
    c_head = jnp.concatenate([cos, cos, pad + 1.0], axis=1)
    sa_head = jnp.concatenate([-sin, jnp.zeros_like(sin), pad], axis=1)
    sb_head = jnp.concatenate([jnp.zeros_like(sin), sin, pad], axis=1)
    reps = LANES // SWA_HEAD_DIM
    return tuple(jnp.tile(t, (1, reps)) for t in (c_head, sa_head, sb_head))


def _tile_rows(tab, seq, n):
    tm = min(TOKEN_TILE, n)
    return tab if seq >= tm else jnp.tile(tab, (tm // seq, 1))


def kernel(x_prompt, x_sample, state_ret, cache_k_win, cache_v_win, ret_norm_pre, ret_w_in, ret_w_out, ret_norm_post, kv_norm, w_kv, swa_norm_pre, swa_w_q, swa_sinks, swa_w_o, swa_norm_post, ffn_norm_pre, ffn_w1, ffn_w2, ffn_norm_post):
    n_a = DEPTH // 2
    assert n_a == 1 and DEPTH == 2, "one retention layer followed by one sliding-window layer"
    d = x_prompt.shape[-1]
    heads = RET_HEADS
    dk = ret_w_out.shape[-1] // heads
    dv = ret_w_out.shape[-2] // heads
    kvh, hd = SWA_KV_HEADS, SWA_HEAD_DIM
    row2 = lambda g: g.reshape(1, d)
    log_g = jnp.log1p(-jnp.exp2(-5.0 - jnp.arange(heads, dtype=F32)))

    w_in = ret_w_in[0].astype(BF16)
    w_out = ret_w_out[0].astype(BF16)
    wq = swa_w_q[0].astype(BF16)
    wkv = w_kv.astype(BF16)
    wo = swa_w_o[0].astype(BF16)
    w1 = ffn_w1.astype(BF16)
    w2 = ffn_w2.astype(BF16)
    sinks = swa_sinks[0]

    def trunk(x, pos, ret_core, swa_mixer, act_dtype):
        b, t, _ = x.shape
        n = b * t
        h = x.reshape(n, d)
        cos, sin = (_tile_rows(tab, t, n) for tab in _ret_rope_tables(pos, dk))
        q, kd, v, sg = _ret_in(h, row2(ret_norm_pre[0]), w_in, cos, sin, _ret_key_scale(log_g, t, n, dk),
                               heads=heads, dk=dk, dv=dv, out_dtype=act_dtype)
        o, state = ret_core(q, kd, v, sg)
        h = _out_ffn(o, h, w_out, row2(ret_norm_post[0]), row2(ffn_norm_pre[0]), w1[0], w2[0], row2(ffn_norm_post[0]))
        o, k_win, v_win = swa_mixer(h, pos, b, t)
        h = _out_ffn(o, h, wo, row2(swa_norm_post[0]), row2(ffn_norm_pre[1]), w1[1], w2[1], row2(ffn_norm_post[1]))
        return h.reshape(b, t, d), state, k_win, v_win

    b_p, t_p, _ = x_prompt.shape
    w_p = min(WINDOW, t_p)

    def swa_prompt(h, pos, b, t):
        cos, sin = _swa_cos_sin(pos)
        qt, k, vt, k_win, v_win = _swa_in_t(h, row2(swa_norm_pre[0]), row2(kv_norm), wq, wkv, cos.T, sin.T,
                                            *_swa_rope_tables(pos), batch=b, win=w_p)
        return _attn_prompt(sinks, qt, k, vt), k_win, v_win

    y_prompt, state_p, k_win_p, v_win_p = trunk(
        x_prompt, jnp.arange(t_p, dtype=F32),
        lambda q, k, v, sg: _ret_prompt(log_g, q, k, v, sg, batch=b_p, heads=heads, dk=dk, dv=dv),
        swa_prompt, BF16)

    b_s, t_s, _ = x_sample.shape
    w_s = cache_k_win.shape[1]
    kc = cache_k_win.reshape(b_s, w_s, kvh * hd)
    vc = cache_v_win.reshape(b_s, w_s, kvh * hd)

    def swa_sample(h, pos, b, t):
        tabs = tuple(_tile_rows(tab, t, b * t) for tab in _swa_rope_tables(pos))
        q, k, v = _swa_in(h, row2(swa_norm_pre[0]), row2(kv_norm), wq, wkv, *tabs, q_dtype=F32)
        return _attn_sample(sinks, q, k, v, kc, vc, seq=t, q_start=PAST_LEN)

    y_sample, state_s, k_win_s, v_win_s = trunk(
        x_sample, PAST_LEN + jnp.arange(t_s, dtype=F32),
        lambda q, k, v, sg: _ret_sample(log_g, q, k, v, sg, state_ret[0], seq=t_s, heads=heads, dk=dk, dv=dv),
        swa_sample, F32)

    return (y_prompt, y_sample, state_p[None], state_s[None],
            k_win_p.reshape(b_p, w_p, kvh, hd), v_win_p.reshape(b_p, w_p, kvh, hd),
            k_win_s.reshape(b_s, w_s, kvh, hd), v_win_s.reshape(b_s, w_s, kvh, hd))
```

```python
import functools

import jax
import jax.numpy as jnp
from jax import lax
from jax.experimental import pallas as pl
from jax.experimental.pallas import tpu as pltpu

DEPTH = 2
PAST_LEN = 16384
RET_HEADS = 4
RET_ROPE_THETA = 10000.0
SWA_HEAD_DIM = 64
SWA_KV_HEADS = 4
WINDOW = 128
ROPE_THETA = 500000.0
ROT_DIM = SWA_HEAD_DIM // 4
EPS = 1e-6
NEG = -1e30
LOG2E = 1.4426950408889634

LANES = 128
SUBLANES = 8
BF16_SUBLANES = 16
VMEM_CAP_BYTES = 64 * 1024 * 1024
VMEM_BUDGET_BYTES = VMEM_CAP_BYTES - 8 * 1024 * 1024

TOKEN_TILE = 512
COL_CHUNK = 1024
RET_KERNEL_CHUNK = 256
RET_STEP_TOKENS = 1024
SAMPLE_GROUP = 2
ATTN_BLOCKS_PER_STEP = 4

F32 = jnp.float32
BF16 = jnp.bfloat16


def _params(semantics, vmem_bytes):
    limit = int(min(max(vmem_bytes, 16 * 1024 * 1024), VMEM_BUDGET_BYTES))
    return pltpu.CompilerParams(dimension_semantics=semantics, vmem_limit_bytes=limit)


def _resident(shape):
    nd = len(shape)
    return pl.BlockSpec(shape, lambda *_: (0,) * nd, pipeline_mode=pl.Buffered(1))


def _nbytes(shape, dtype):
    n = 1
    for s in shape:
        n *= s
    return n * jnp.dtype(dtype).itemsize


def _rms_rows(x):
    return x * lax.rsqrt(jnp.mean(x * x, axis=-1, keepdims=True) + EPS)


def _dot(a, b):
    return jnp.dot(a, b, preferred_element_type=F32)


def _dot_nt(a, b):
    return lax.dot_general(a, b, (((1,), (1,)), ((), ())), preferred_element_type=F32)


def _dot_tn(a, b):
    return lax.dot_general(a, b, (((0,), (0,)), ((), ())), preferred_element_type=F32)


def _ret_in_body(h_ref, g_ref, w_ref, cos_ref, sin_ref, kscale_ref, q_ref, kd_ref, v_ref, sg_ref, *, heads, dk, dv):
    xn = (_rms_rows(h_ref[...]) * g_ref[...]).astype(BF16)
    cos = cos_ref[...]
    sin = sin_ref[...]
    half = dk // 2
    qk_w = heads * dk
    v_w = heads * dv

    def proj(lo, width):
        return _dot(xn, w_ref[:, lo:lo + width])

    for base, ref, scale_ref in ((0, q_ref, None), (qk_w, kd_ref, kscale_ref)):
        p = proj(base, qk_w)
        for hh in range(heads):
            lo, mid, hi = hh * dk, hh * dk + half, (hh + 1) * dk
            x1 = p[:, lo:mid]
            x2 = p[:, mid:hi]
            o1 = x1 * cos - x2 * sin
            o2 = x2 * cos + x1 * sin
            if scale_ref is not None:
                o1 = o1 * scale_ref[:, lo:mid]
                o2 = o2 * scale_ref[:, mid:hi]
            ref[hh, :, :half] = o1.astype(ref.dtype)
            ref[hh, :, half:] = o2.astype(ref.dtype)
    cw = min(COL_CHUNK, v_w)
    per_chunk = cw // dv
    for c in range(v_w // cw):
        v = proj(2 * qk_w + c * cw, cw)
        for j in range(per_chunk):
            v_ref[c * per_chunk + j] = v[:, j * dv:(j + 1) * dv].astype(v_ref.dtype)
    for c in range(v_w // cw):
        gate = proj(2 * qk_w + v_w + c * cw, cw)
        sg = gate * jax.nn.sigmoid(gate)
        for j in range(per_chunk):
            sg_ref[c * per_chunk + j] = sg[:, j * dv:(j + 1) * dv].astype(sg_ref.dtype)


def _ret_in(h, g, w_in, cos, sin, kscale, *, heads, dk, dv, out_dtype):
    n, d = h.shape
    tm = min(TOKEN_TILE, n)
    qk_w, v_w = heads * dk, heads * dv
    pos_tiles = cos.shape[0] // tm
    row = lambda i: (i, 0)
    tab = lambda i: (i % pos_tiles, 0)
    by_head = lambda i: (0, i, 0)
    vmem = (2 * _nbytes((tm, d), F32) + _nbytes(w_in.shape, BF16) + 4 * _nbytes((tm, dk // 2), F32)
            + _nbytes((tm, qk_w), F32) + 2 * _nbytes((tm, 2 * qk_w + 2 * v_w), out_dtype)
            + 4 * _nbytes((tm, COL_CHUNK), F32))
    return pl.pallas_call(
        functools.partial(_ret_in_body, heads=heads, dk=dk, dv=dv),
        grid=(n // tm,),
        in_specs=[pl.BlockSpec((tm, d), row), _resident((1, d)), _resident(w_in.shape),
                  pl.BlockSpec((tm, dk // 2), tab), pl.BlockSpec((tm, dk // 2), tab), _resident((tm, qk_w))],
        out_specs=[pl.BlockSpec((heads, tm, dk), by_head), pl.BlockSpec((heads, tm, dk), by_head),
                   pl.BlockSpec((heads, tm, dv), by_head), pl.BlockSpec((heads, tm, dv), by_head)],
        out_shape=[jax.ShapeDtypeStruct((heads, n, dk), out_dtype), jax.ShapeDtypeStruct((heads, n, dk), out_dtype),
                   jax.ShapeDtypeStruct((heads, n, dv), out_dtype), jax.ShapeDtypeStruct((heads, n, dv), out_dtype)],
        compiler_params=_params(("parallel",), vmem),
        name="ret_in",
    )(h, g, w_in, cos, sin, kscale)


def _ret_prompt_body(lg_ref, q_ref, kd_ref, v_ref, sg_ref, o_ref, s_out_ref, s_ref, *, chunk):
    hh = pl.program_id(1)
    t = pl.program_id(2)
    lg = lg_ref[hh]
    _, tc, dk = q_ref.shape
    dv = v_ref.shape[2]

    @pl.when(t == 0)
    def _():
        s_ref[...] = jnp.zeros_like(s_ref)

    ri = lax.broadcasted_iota(jnp.int32, (chunk, chunk), 0)
    ci = lax.broadcasted_iota(jnp.int32, (chunk, chunk), 1)
    causal = jnp.where(ri >= ci, jnp.exp(jnp.full((chunk, chunk), -lg * chunk, F32)), 0.0)
    row_v = lax.broadcasted_iota(jnp.int32, (chunk, dv), 0).astype(F32)
    q_decay = jnp.exp(lg * (row_v + 1.0))
    chunk_decay = jnp.exp(jnp.full((1, dv), lg * chunk, F32))

    def scores(c):
        rows = pl.ds(c * chunk, chunk)
        return _dot_nt(q_ref[0, rows, :], kd_ref[0, rows, :])

    qk = scores(0)
    for c in range(tc // chunk):
        rows = pl.ds(c * chunk, chunk)
        qk_next = scores(c + 1) if (c + 1) * chunk < tc else None
        v = v_ref[0, rows, :]
        grow = _dot_tn(kd_ref[0, rows, :], v)
        s_prev = s_ref[...]
        lhs = jnp.concatenate([(qk * causal).astype(BF16), q_ref[0, rows, :]], axis=1)
        rhs = jnp.concatenate([v, s_prev.astype(BF16)], axis=0)
        o = _rms_rows(q_decay * _dot(lhs, rhs))
        s_ref[...] = chunk_decay * s_prev + grow
        o_ref[0, rows, :] = (o * sg_ref[0, rows, :].astype(F32)).astype(o_ref.dtype)
        qk = qk_next

    @pl.when(t == pl.num_programs(2) - 1)
    def _():
        s_out_ref[0, 0] = s_ref[...]


def _ret_chunk(seq):
    return RET_KERNEL_CHUNK if seq % RET_KERNEL_CHUNK == 0 else seq


def _ret_prompt(log_g, q, kd, v, sg, *, batch):
    heads, n, dk = q.shape
    dv = v.shape[2]
    seq = n // batch
    chunk = _ret_chunk(seq)
    tc = min(RET_STEP_TOKENS, seq)
    tc = tc if (tc % chunk == 0 and seq % tc == 0) else chunk
    nt = seq // tc
    blk = lambda b, h, t: (h, b * nt + t, 0)
    vmem = (4 * _nbytes((tc, dk), BF16) + 6 * _nbytes((tc, dv), BF16) + 3 * _nbytes((dk, dv), F32)
            + 8 * _nbytes((chunk, dv), F32) + 4 * _nbytes((chunk, chunk), F32) + 2 * _nbytes((dk, dv), F32))
    return pl.pallas_call(
        functools.partial(_ret_prompt_body, chunk=chunk),
        grid=(batch, heads, nt),
        in_specs=[pl.BlockSpec(memory_space=pltpu.SMEM),
                  pl.BlockSpec((1, tc, dk), blk), pl.BlockSpec((1, tc, dk), blk),
                  pl.BlockSpec((1, tc, dv), blk), pl.BlockSpec((1, tc, dv), blk)],
        out_specs=[pl.BlockSpec((1, tc, dv), blk),
                   pl.BlockSpec((1, 1, dk, dv), lambda b, h, t: (b, h, 0, 0))],
        out_shape=[jax.ShapeDtypeStruct((heads, n, dv), BF16),
                   jax.ShapeDtypeStruct((batch, heads, dk, dv), F32)],
        scratch_shapes=[pltpu.VMEM((dk, dv), F32)],
        compiler_params=_params(("parallel", "parallel", "arbitrary"), vmem),
        name="ret_prompt",
    )(log_g, q, kd, v, sg)


def _ret_sample_body(lg_ref, q_ref, kd_ref, v_ref, sg_ref, s_in_ref, o_ref, s_out_ref, *, seq):
    heads, rows, dk = q_ref.shape
    dv = v_ref.shape[2]
    group = rows // seq
    ri = lax.broadcasted_iota(jnp.int32, (rows, rows), 0)
    ci = lax.broadcasted_iota(jnp.int32, (rows, rows), 1)
    visible = ((ri // seq) == (ci // seq)) & (ri >= ci)
    row_v = lax.broadcasted_iota(jnp.int32, (rows, dv), 0)
    row_k = lax.broadcasted_iota(jnp.int32, (rows, dk), 0)
    for hh in range(heads):
        lg = lg_ref[hh]
        causal = jnp.where(visible, jnp.exp(jnp.full((rows, rows), -lg * seq, F32)), 0.0)
        q_decay = jnp.exp(lg * ((row_v % seq).astype(F32) + 1.0))
        chunk_decay = jnp.exp(jnp.full((1, dv), lg * seq, F32))
        q = q_ref[hh].astype(BF16)
        kd = kd_ref[hh]
        v = v_ref[hh].astype(BF16)
        o = _dot((_dot_nt(q, kd.astype(BF16)) * causal).astype(BF16), v)
        for g in range(group):
            s_prev = s_in_ref[g, hh]
            o = jnp.where((row_v // seq) == g, o + _dot(q, s_prev.astype(BF16)), o)
            kd_g = jnp.where((row_k // seq) == g, kd, 0.0).astype(BF16)
            s_out_ref[g, hh] = chunk_decay * s_prev + _dot_tn(kd_g, v)
        o = _rms_rows(q_decay * o)
        o_ref[hh] = (o * sg_ref[hh]).astype(o_ref.dtype)


def _ret_sample(log_g, q, kd, v, sg, state, *, seq):
    heads, n, dk = q.shape
    dv = v.shape[2]
    batch = n // seq
    group = SAMPLE_GROUP if batch % SAMPLE_GROUP == 0 else batch
    rows = group * seq
    by_head = lambda i: (0, i, 0)
    st = lambda i: (i, 0, 0, 0)
    vmem = (4 * _nbytes((group, heads, dk, dv), F32) + 8 * _nbytes((rows, heads * dv), F32)
            + 4 * _nbytes((dk, dv), F32))
    return pl.pallas_call(
        functools.partial(_ret_sample_body, seq=seq),
        grid=(batch // group,),
        in_specs=[pl.BlockSpec(memory_space=pltpu.SMEM),
                  pl.BlockSpec((heads, rows, dk), by_head), pl.BlockSpec((heads, rows, dk), by_head),
                  pl.BlockSpec((heads, rows, dv), by_head), pl.BlockSpec((heads, rows, dv), by_head),
                  pl.BlockSpec((group, heads, dk, dv), st)],
        out_specs=[pl.BlockSpec((heads, rows, dv), by_head), pl.BlockSpec((group, heads, dk, dv), st)],
        out_shape=[jax.ShapeDtypeStruct((heads, n, dv), F32),
                   jax.ShapeDtypeStruct((batch, heads, dk, dv), F32)],
        compiler_params=_params(("parallel",), vmem),
        name="ret_sample",
    )(log_g, q, kd, v, sg, state)


def _out_ffn_body(o_ref, h_ref, wo_ref, g_post_ref, g_pre_ref, w1_ref, w2_ref, g_ffn_ref, y_ref, *, o_layout):
    if o_layout == "features":
        a = _dot_tn(o_ref[0], wo_ref[...])
    elif o_layout == "heads":
        o = jnp.concatenate([o_ref[hh] for hh in range(o_ref.shape[0])], axis=1)
        a = _dot(o.astype(BF16), wo_ref[...])
    else:
        a = _dot(o_ref[...].astype(BF16), wo_ref[...])
    h1 = h_ref[...] + _rms_rows(a) * g_post_ref[...]
    x = (_rms_rows(h1) * g_pre_ref[...]).astype(BF16)
    d_ff = w1_ref.shape[1]
    fc = min(COL_CHUNK, d_ff)
    acc = jnp.zeros(h1.shape, F32)
    for c in range(d_ff // fc):
        u = jnp.maximum(_dot(x, w1_ref[:, c * fc:(c + 1) * fc]), 0.0)
        acc = acc + _dot((u * u).astype(BF16), w2_ref[c * fc:(c + 1) * fc, :])
    y_ref[...] = h1 + _rms_rows(acc) * g_ffn_ref[...]


def _out_ffn(o, h, w_o, g_post, g_pre, w1, w2, g_ffn, *, o_layout):
    n, d = h.shape
    kdim = w_o.shape[0]
    tm = min(TOKEN_TILE, n)
    row = lambda i: (i, 0)
    if o_layout == "features":
        tiles = o.shape[2] // tm
        o_spec = pl.BlockSpec((1, kdim, tm), lambda i: (i // tiles, 0, i % tiles))
    elif o_layout == "heads":
        o_spec = pl.BlockSpec((o.shape[0], tm, o.shape[2]), lambda i: (0, i, 0))
    else:
        o_spec = pl.BlockSpec((tm, kdim), row)
    vmem = (2 * _nbytes((tm, kdim), o.dtype) + 4 * _nbytes((tm, d), F32) + _nbytes(w_o.shape, BF16)
            + _nbytes(w1.shape, BF16) + _nbytes(w2.shape, BF16) + 6 * _nbytes((tm, COL_CHUNK), F32))
    return pl.pallas_call(
        functools.partial(_out_ffn_body, o_layout=o_layout),
        grid=(n // tm,),
        in_specs=[o_spec, pl.BlockSpec((tm, d), row), _resident(w_o.shape),
                  _resident((1, d)), _resident((1, d)), _resident(w1.shape), _resident(w2.shape),
                  _resident((1, d))],
        out_specs=pl.BlockSpec((tm, d), row),
        out_shape=jax.ShapeDtypeStruct((n, d), F32),
        compiler_params=_params(("parallel",), vmem),
        name="out_ffn",
    )(o, h, w_o, g_post, g_pre, w1, w2, g_ffn)


def _partial_rope(x, c_tab, sa_tab, sb_tab):
    half = ROT_DIM // 2
    outs = []
    for j in range(x.shape[1] // LANES):
        s = x[:, j * LANES:(j + 1) * LANES]
        outs.append(s * c_tab + pltpu.roll(s, LANES - half, axis=1) * sa_tab + pltpu.roll(s, half, axis=1) * sb_tab)
    return outs


def _swa_in_body(h_ref, g_q_ref, g_kv_ref, wq_ref, wkv_ref, c_ref, sa_ref, sb_ref, q_ref, k_ref, v_ref):
    y = _rms_rows(h_ref[...])
    xq = (y * g_q_ref[...]).astype(BF16)
    xkv = (y * g_kv_ref[...]).astype(BF16)
    c_tab, sa_tab, sb_tab = c_ref[...], sa_ref[...], sb_ref[...]
    q = _dot(xq, wq_ref[...]) * (SWA_HEAD_DIM ** -0.5)
    for j, s in enumerate(_partial_rope(q, c_tab, sa_tab, sb_tab)):
        q_ref[:, j * LANES:(j + 1) * LANES] = s.astype(q_ref.dtype)
    kv = _dot(xkv, wkv_ref[...])
    kw = k_ref.shape[1]
    for j, s in enumerate(_partial_rope(kv[:, :kw], c_tab, sa_tab, sb_tab)):
        k_ref[:, j * LANES:(j + 1) * LANES] = s
    v_ref[...] = kv[:, kw:]


def _swa_in(h, g_q, g_kv, w_q, w_kv, c_tab, sa_tab, sb_tab, *, q_dtype):
    n, d = h.shape
    tm = min(TOKEN_TILE, n)
    qw = w_q.shape[1]
    kw = w_kv.shape[1] // 2
    pos_tiles = c_tab.shape[0] // tm
    row = lambda i: (i, 0)
    tab = lambda i: (i % pos_tiles, 0)
    vmem = (2 * _nbytes((tm, d), F32) + _nbytes(w_q.shape, BF16) + _nbytes(w_kv.shape, BF16)
            + 6 * _nbytes((tm, LANES), F32) + 2 * _nbytes((tm, qw), q_dtype) + 4 * _nbytes((tm, kw), F32)
            + 6 * _nbytes((tm, qw), F32))
    return pl.pallas_call(
        _swa_in_body,
        grid=(n // tm,),
        in_specs=[pl.BlockSpec((tm, d), row), _resident((1, d)), _resident((1, d)),
                  _resident(w_q.shape), _resident(w_kv.shape),
                  pl.BlockSpec((tm, LANES), tab), pl.BlockSpec((tm, LANES), tab), pl.BlockSpec((tm, LANES), tab)],
        out_specs=[pl.BlockSpec((tm, qw), row), pl.BlockSpec((tm, kw), row), pl.BlockSpec((tm, kw), row)],
        out_shape=[jax.ShapeDtypeStruct((n, qw), q_dtype), jax.ShapeDtypeStruct((n, kw), F32),
                   jax.ShapeDtypeStruct((n, kw), F32)],
        compiler_params=_params(("parallel",), vmem),
        name="swa_in",
    )(h, g_q, g_kv, w_q, w_kv, c_tab, sa_tab, sb_tab)


def _swa_in_t_body(h_ref, g_q_ref, g_kv_ref, wqt_ref, wk_ref, wvt_ref, wv_ref, cos_t_ref, sin_t_ref,
                   c_ref, sa_ref, sb_ref, qt_ref, k_ref, vt_ref, kwin_ref, vwin_ref, *, tiles):
    y = _rms_rows(h_ref[...])
    xq = (y * g_q_ref[...]).astype(BF16)
    xkv = (y * g_kv_ref[...]).astype(BF16)
    tm = xq.shape[0]
    hd = SWA_HEAD_DIM
    half = ROT_DIM // 2
    cos_t, sin_t = cos_t_ref[...], sin_t_ref[...]
    qt = _dot_nt(wqt_ref[...], xq) * (hd ** -0.5 * LOG2E)
    for hq in range(qt.shape[0] // hd):
        base = hq * hd
        x1 = qt[base:base + half]
        x2 = qt[base + half:base + 2 * half]
        rot = jnp.concatenate([x1 * cos_t - x2 * sin_t, x2 * cos_t + x1 * sin_t], axis=0)
        qt_ref[0, base:base + 2 * half, :] = rot.astype(qt_ref.dtype)
        qt_ref[0, base + 2 * half:base + hd, :] = qt[base + 2 * half:base + hd].astype(qt_ref.dtype)
    k_rot = _partial_rope(_dot(xkv, wk_ref[...]), c_ref[...], sa_ref[...], sb_ref[...])
    for j, s in enumerate(k_rot):
        k_ref[:, j * LANES:(j + 1) * LANES] = s.astype(k_ref.dtype)
    vt_ref[0] = _dot_nt(wvt_ref[...], xkv).astype(vt_ref.dtype)
    win = kwin_ref.shape[0]

    @pl.when(pl.program_id(0) % tiles == tiles - 1)
    def _():
        for j, s in enumerate(k_rot):
            kwin_ref[:, j * LANES:(j + 1) * LANES] = s[tm - win:, :]
        vwin_ref[...] = _dot(xkv[tm - win:, :], wv_ref[...])


def _swa_in_t(h, g_q, g_kv, w_q, w_kv, cos_t, sin_t, c_tab, sa_tab, sb_tab, *, batch, win):
    n, d = h.shape
    seq = n // batch
    tm = min(TOKEN_TILE, seq)
    tiles = seq // tm
    qw = w_q.shape[1]
    kw = w_kv.shape[1] // 2
    wqt = w_q.T
    wk, wv = w_kv[:, :kw], w_kv[:, kw:]
    wvt = wv.T
    row = lambda i: (i, 0)
    tab = lambda i: (i % tiles, 0)
    tab_t = lambda i: (0, i % tiles)
    feat = lambda i: (i // tiles, 0, i % tiles)
    per_seq = lambda i: (i // tiles, 0)
    half = ROT_DIM // 2
    vmem = (2 * _nbytes((tm, d), F32) + 2 * _nbytes(w_q.shape, BF16) + 3 * _nbytes(w_kv.shape, BF16)
            + 8 * _nbytes((tm, LANES), F32) + 2 * _nbytes((tm, qw + 2 * kw), BF16) + 4 * _nbytes((win, kw), F32)
            + 4 * _nbytes((tm, qw), F32))
    return pl.pallas_call(
        functools.partial(_swa_in_t_body, tiles=tiles),
        grid=(n // tm,),
        in_specs=[pl.BlockSpec((tm, d), row), _resident((1, d)), _resident((1, d)),
                  _resident(wqt.shape), _resident(wk.shape), _resident(wvt.shape), _resident(wv.shape),
                  pl.BlockSpec((half, tm), tab_t), pl.BlockSpec((half, tm), tab_t),
                  pl.BlockSpec((tm, LANES), tab), pl.BlockSpec((tm, LANES), tab), pl.BlockSpec((tm, LANES), tab)],
        out_specs=[pl.BlockSpec((1, qw, tm), feat), pl.BlockSpec((tm, kw), row), pl.BlockSpec((1, kw, tm), feat),
                   pl.BlockSpec((win, kw), per_seq), pl.BlockSpec((win, kw), per_seq)],
        out_shape=[jax.ShapeDtypeStruct((batch, qw, seq), BF16), jax.ShapeDtypeStruct((n, kw), BF16),
                   jax.ShapeDtypeStruct((batch, kw, seq), BF16),
                   jax.ShapeDtypeStruct((batch * win, kw), F32), jax.ShapeDtypeStruct((batch * win, kw), F32)],
        compiler_params=_params(("arbitrary",), vmem),
        name="swa_in_t",
    )(h, g_q, g_kv, wqt, wk, wvt, wv, cos_t, sin_t, c_tab, sa_tab, sb_tab)


def _sink_rows(sinks_ref, kvh, group, rows_per_head, shape):
    r = lax.broadcasted_iota(jnp.int32, shape, 0) // rows_per_head
    col = jnp.full(shape, sinks_ref[kvh * group], F32)
    for g in range(1, group):
        col = jnp.where(r == g, sinks_ref[kvh * group + g], col)
    return col


def _attn_prompt_body(sinks_ref, mask_ref, qt_ref, kp_ref, kc_ref, vtp_ref, vtc_ref, ot_ref, *, group):
    blk = kp_ref.shape[0]
    nblk = kc_ref.shape[0] // blk
    hd = SWA_HEAD_DIM
    kvh_n = kc_ref.shape[1] // hd
    cols = group * blk
    kj = lax.broadcasted_iota(jnp.int32, (blk, cols), 0)
    qi = lax.broadcasted_iota(jnp.int32, (blk, cols), 1) % blk
    own = kj <= qi
    lane_head = lax.broadcasted_iota(jnp.int32, (1, cols), 1) // blk
    ones_rows = jnp.ones((BF16_SUBLANES, blk), BF16)
    has_prev = pl.program_id(1) > 0

    def scores(j, kvh):
        tile, lo = divmod(kvh * hd, LANES)
        q4t = jnp.concatenate([qt_ref[0, (kvh * group + g) * hd:(kvh * group + g + 1) * hd, j * blk:(j + 1) * blk]
                               for g in range(group)], axis=1)
        rhs = jnp.concatenate([q4t if part * hd == lo else jnp.zeros_like(q4t) for part in range(LANES // hd)], axis=0)
        k_tile = slice(tile * LANES, (tile + 1) * LANES)
        k_prev = kp_ref[:, k_tile] if j == 0 else kc_ref[(j - 1) * blk:j * blk, k_tile]
        return _dot(kc_ref[j * blk:(j + 1) * blk, k_tile], rhs), _dot(k_prev, rhs)

    def finish(j, kvh, s_own, s_prev):
        if j == 0:
            s_prev = jnp.where(has_prev, s_prev, NEG)
        s = jnp.where(own, s_own, s_prev)
        sink = jnp.full((1, cols), sinks_ref[kvh * group] * LOG2E, F32)
        for g in range(1, group):
            sink = jnp.where(lane_head == g, sinks_ref[kvh * group + g] * LOG2E, sink)
        m = jnp.maximum(jnp.max(s, axis=0, keepdims=True), sink)
        e = jnp.exp2(s - m).astype(BF16)
        p_own = e * mask_ref[...]
        p = jnp.concatenate([p_own, e - p_own], axis=0)
        head_rows = slice(kvh * hd, (kvh + 1) * hd)
        vt_own = vtc_ref[0, head_rows, j * blk:(j + 1) * blk]
        vt_prev = vtp_ref[0, head_rows, :] if j == 0 else vtc_ref[0, head_rows, (j - 1) * blk:j * blk]
        vt = jnp.concatenate([jnp.concatenate([vt_own, ones_rows], axis=0),
                              jnp.concatenate([vt_prev, ones_rows], axis=0)], axis=1)
        acc = _dot(vt, p)
        denom = acc[hd:hd + 1, :] + jnp.exp2(sink - m)
        ot = acc[:hd] / denom
        for g in range(group):
            hq = kvh * group + g
            ot_ref[0, hq * hd:(hq + 1) * hd, j * blk:(j + 1) * blk] = ot[:, g * blk:(g + 1) * blk].astype(ot_ref.dtype)

    units = [(j, kvh) for j in range(nblk) for kvh in range(kvh_n)]
    pending = scores(*units[0])
    for idx, unit in enumerate(units):
        upcoming = scores(*units[idx + 1]) if idx + 1 < len(units) else None
        finish(*unit, *pending)
        pending = upcoming


def _attn_prompt(sinks, qt, k, vt):
    batch, qw, seq = qt.shape
    kw = k.shape[1]
    blk = WINDOW
    nblk = ATTN_BLOCKS_PER_STEP if seq % (ATTN_BLOCKS_PER_STEP * blk) == 0 else 1
    span = nblk * blk
    steps = seq // span
    group = qw // kw
    cols = group * blk
    own = (jnp.arange(blk)[:, None] <= (jnp.arange(cols) % blk)[None, :]).astype(BF16)
    cur_t = lambda b, i: (b, 0, i)
    prev_t = lambda b, i: (b, 0, jnp.maximum(i * nblk - 1, 0))
    cur = lambda b, i: (b * steps + i, 0)
    prev = lambda b, i: (b * steps * nblk + jnp.maximum(i * nblk - 1, 0), 0)
    vmem = (4 * _nbytes((qw, span), BF16) + 6 * _nbytes((span, kw), BF16) + 16 * _nbytes((blk, cols), F32))
    return pl.pallas_call(
        functools.partial(_attn_prompt_body, group=group),
        grid=(batch, steps),
        in_specs=[pl.BlockSpec(memory_space=pltpu.SMEM), _resident((blk, cols)), pl.BlockSpec((1, qw, span), cur_t),
                  pl.BlockSpec((blk, kw), prev), pl.BlockSpec((span, kw), cur),
                  pl.BlockSpec((1, kw, blk), prev_t), pl.BlockSpec((1, kw, span), cur_t)],
        out_specs=pl.BlockSpec((1, qw, span), cur_t),
        out_shape=jax.ShapeDtypeStruct((batch, qw, seq), BF16),
        compiler_params=_params(("parallel", "parallel"), vmem),
        name="attn_prompt",
    )(sinks, own, qt, k, k, vt, vt)


def _attn_sample_body(sinks_ref, q_ref, kn_ref, vn_ref, kc_ref, vc_ref, o_ref, kw_ref, vw_ref, *, seq, group, q_start):
    rows = q_ref.shape[0]
    bgroup = rows // seq
    win = kc_ref.shape[1]
    hd = SWA_HEAD_DIM
    kvh_n = kn_ref.shape[1] // hd
    srows = group * rows
    kn = kn_ref[...]
    vn = vn_ref[...]
    r_c = lax.broadcasted_iota(jnp.int32, (srows, win), 0) % rows
    c_c = lax.broadcasted_iota(jnp.int32, (srows, win), 1)
    rel_c = (r_c % seq) + win - c_c
    ok_c = (rel_c >= 0) & (rel_c < WINDOW) & (q_start - win + c_c >= 0)
    r_n = lax.broadcasted_iota(jnp.int32, (srows, rows), 0) % rows
    c_n = lax.broadcasted_iota(jnp.int32, (srows, rows), 1)
    rel_n = (r_n % seq) - (c_n % seq)
    ok_n = (rel_n >= 0) & (rel_n < WINDOW) & ((r_n // seq) == (c_n // seq))
    owner = (lax.broadcasted_iota(jnp.int32, (srows, hd), 0) % rows) // seq
    for kvh in range(kvh_n):
        cols = slice(kvh * hd, (kvh + 1) * hd)
        q = jnp.concatenate([q_ref[:, (kvh * group + g) * hd:(kvh * group + g + 1) * hd] for g in range(group)],
                            axis=0).astype(BF16)
        sink = _sink_rows(sinks_ref, kvh, group, rows, (srows, 1))
        s_n = jnp.where(ok_n, _dot_nt(q, kn[:, cols].astype(BF16)), NEG)
        m_n = jnp.max(s_n, axis=-1, keepdims=True)
        o = jnp.zeros((srows, hd), F32)
        for b in range(bgroup):
            s_c = jnp.where(ok_c, _dot_nt(q, kc_ref[b, :, cols].astype(BF16)), NEG)
            m = jnp.maximum(jnp.maximum(jnp.max(s_c, axis=-1, keepdims=True), m_n), sink)
            e_c = jnp.exp(s_c - m)
            e_n = jnp.exp(s_n - m)
            denom = jnp.sum(e_c, axis=-1, keepdims=True) + jnp.sum(e_n, axis=-1, keepdims=True) + jnp.exp(sink - m)
            o_b = (_dot(e_c.astype(BF16), vc_ref[b, :, cols].astype(BF16))
                   + _dot(e_n.astype(BF16), vn[:, cols].astype(BF16))) / denom
            o = jnp.where(owner == b, o_b, o)
        for g in range(group):
            hq = kvh * group + g
            o_ref[:, hq * hd:(hq + 1) * hd] = o[g * rows:(g + 1) * rows].astype(o_ref.dtype)
    for b in range(bgroup):
        kw_ref[b, 0:win - seq, :] = kc_ref[b, seq:win, :]
        kw_ref[b, win - seq:win, :] = kn[b * seq:(b + 1) * seq, :]
        vw_ref[b, 0:win - seq, :] = vc_ref[b, seq:win, :]
        vw_ref[b, win - seq:win, :] = vn[b * seq:(b + 1) * seq, :]


def _attn_sample(sinks, q, k_new, v_new, k_cache, v_cache, *, seq, q_start):
    n, qw = q.shape
    kw = k_new.shape[1]
    batch, win, _ = k_cache.shape
    group = qw // kw
    bgroup = SAMPLE_GROUP if batch % SAMPLE_GROUP == 0 else batch
    rows = bgroup * seq
    row = lambda i: (i, 0)
    cache = lambda i: (i, 0, 0)
    vmem = 8 * _nbytes((bgroup, win, kw), F32) + 8 * _nbytes((rows, qw), F32) + 16 * _nbytes((group * rows, win), F32)
    return pl.pallas_call(
        functools.partial(_attn_sample_body, seq=seq, group=group, q_start=q_start),
        grid=(batch // bgroup,),
        in_specs=[pl.BlockSpec(memory_space=pltpu.SMEM), pl.BlockSpec((rows, qw), row),
                  pl.BlockSpec((rows, kw), row), pl.BlockSpec((rows, kw), row),
                  pl.BlockSpec((bgroup, win, kw), cache), pl.BlockSpec((bgroup, win, kw), cache)],
        out_specs=[pl.BlockSpec((rows, qw), row), pl.BlockSpec((bgroup, win, kw), cache),
                   pl.BlockSpec((bgroup, win, kw), cache)],
        out_shape=[jax.ShapeDtypeStruct((n, qw), F32), jax.ShapeDtypeStruct((batch, win, kw), F32),
                   jax.ShapeDtypeStruct((batch, win, kw), F32)],
        compiler_params=_params(("parallel",), vmem),
        name="attn_sample",
    )(sinks, q, k_new, v_new, k_cache, v_cache)


def _ret_rope_tables(pos, dk):
    inv = 1.0 / (RET_ROPE_THETA ** jnp.linspace(0.0, 1.0, dk // 2, dtype=F32))
    ang = pos[:, None] * inv[None, :]
    return jnp.cos(ang), jnp.sin(ang)


def _ret_key_scale(log_g, seq, n, dk):
    chunk = _ret_chunk(seq)
    tm = min(TOKEN_TILE, n)
    assert tm % chunk == 0
    left = (chunk - 1 - jnp.arange(tm) % chunk).astype(F32)
    per_head = jnp.exp(log_g[None, :] * left[:, None]) * dk ** -0.5
    return jnp.repeat(per_head, dk, axis=1)


def _swa_cos_sin(pos):
    half = ROT_DIM // 2
    inv = ROPE_THETA ** (-jnp.arange(half, dtype=F32) / half)
    ang = pos[:, None] * inv[None, :]
    return jnp.cos(ang), jnp.sin(ang)


def _swa_rope_tables(pos):
    half = ROT_DIM // 2
    cos, sin = _swa_cos_sin(pos)
    n = pos.shape[0]
    pad = jnp.zeros((n, SWA_HEAD_DIM - 2 * half), F32)
    c_head = jnp.concatenate([cos, cos, pad + 1.0], axis=1)
    sa_head = jnp.concatenate([-sin, jnp.zeros_like(sin), pad], axis=1)
    sb_head = jnp.concatenate([jnp.zeros_like(sin), sin, pad], axis=1)
    reps = LANES // SWA_HEAD_DIM
    return tuple(jnp.tile(t, (1, reps)) for t in (c_head, sa_head, sb_head))


def _tile_rows(tab, seq, n):
    tm = min(TOKEN_TILE, n)
    return tab if seq >= tm else jnp.tile(tab, (tm // seq, 1))


def kernel(x_prompt, x_sample, state_ret, cache_k_win, cache_v_win, ret_norm_pre, ret_w_in, ret_w_out, ret_norm_post, kv_norm, w_kv, swa_norm_pre, swa_w_q, swa_sinks, swa_w_o, swa_norm_post, ffn_norm_pre, ffn_w1, ffn_w2, ffn_norm_post):
    n_a = DEPTH // 2
    assert n_a == 1 and DEPTH == 2, "one retention layer followed by one sliding-window layer"
    d = x_prompt.shape[-1]
    heads = RET_HEADS
    dk = ret_w_out.shape[-1] // heads
    dv = ret_w_out.shape[-2] // heads
    kvh, hd = SWA_KV_HEADS, SWA_HEAD_DIM
    row2 = lambda g: g.reshape(1, d)
    log_g = jnp.log1p(-jnp.exp2(-5.0 - jnp.arange(heads, dtype=F32)))

    w_in = ret_w_in[0].astype(BF16)
    w_out = ret_w_out[0].astype(BF16)
    wq = swa_w_q[0].astype(BF16)
    wkv = w_kv.astype(BF16)
    wo = swa_w_o[0].astype(BF16)
    w1 = ffn_w1.astype(BF16)
    w2 = ffn_w2.astype(BF16)
    sinks = swa_sinks[0]

    def trunk(x, pos, ret_core, swa_mixer, act_dtype):
        b, t, _ = x.shape
        n = b * t
        h = x.reshape(n, d)
        cos, sin = (_tile_rows(tab, t, n) for tab in _ret_rope_tables(pos, dk))
        q, kd, v, sg = _ret_in(h, row2(ret_norm_pre[0]), w_in, cos, sin, _ret_key_scale(log_g, t, n, dk),
                               heads=heads, dk=dk, dv=dv, out_dtype=act_dtype)
        o, state = ret_core(q, kd, v, sg)
        h = _out_ffn(o, h, w_out, row2(ret_norm_post[0]), row2(ffn_norm_pre[0]), w1[0], w2[0], row2(ffn_norm_post[0]),
                     o_layout="heads")
        o, o_layout, k_win, v_win = swa_mixer(h, pos, b, t)
        h = _out_ffn(o, h, wo, row2(swa_norm_post[0]), row2(ffn_norm_pre[1]), w1[1], w2[1], row2(ffn_norm_post[1]),
                     o_layout=o_layout)
        return h.reshape(b, t, d), state, k_win, v_win

    b_p, t_p, _ = x_prompt.shape
    w_p = min(WINDOW, t_p)

    def swa_prompt(h, pos, b, t):
        cos, sin = _swa_cos_sin(pos)
        qt, k, vt, k_win, v_win = _swa_in_t(h, row2(swa_norm_pre[0]), row2(kv_norm), wq, wkv, cos.T, sin.T,
                                            *_swa_rope_tables(pos), batch=b, win=w_p)
        return _attn_prompt(sinks, qt, k, vt), "features", k_win, v_win

    y_prompt, state_p, k_win_p, v_win_p = trunk(
        x_prompt, jnp.arange(t_p, dtype=F32),
        lambda q, kd, v, sg: _ret_prompt(log_g, q, kd, v, sg, batch=b_p),
        swa_prompt, BF16)

    b_s, t_s, _ = x_sample.shape
    w_s = cache_k_win.shape[1]
    kc = cache_k_win.reshape(b_s, w_s, kvh * hd)
    vc = cache_v_win.reshape(b_s, w_s, kvh * hd)

    def swa_sample(h, pos, b, t):
        tabs = tuple(_tile_rows(tab, t, b * t) for tab in _swa_rope_tables(pos))
        q, k, v = _swa_in(h, row2(swa_norm_pre[0]), row2(kv_norm), wq, wkv, *tabs, q_dtype=F32)
        o, k_win, v_win = _attn_sample(sinks, q, k, v, kc, vc, seq=t, q_start=PAST_LEN)
        return o, "tokens", k_win, v_win

    y_sample, state_s, k_win_s, v_win_s = trunk(
        x_sample, PAST_LEN + jnp.arange(t_s, dtype=F32),
        lambda q, kd, v, sg: _ret_sample(log_g, q, kd, v, sg, state_ret[0], seq=t_s),
        swa_sample, F32)

    return (y_prompt, y_sample, state_p[None], state_s[None],
            k_win_p.reshape(b_p, w_p, kvh, hd), v_win_p.reshape(b_p, w_p, kvh, hd),
            k_win_s.reshape(b_s, w_s, kvh, hd), v_win_s.reshape(b_s, w_s, kvh, hd))
```

```python
import functools

import jax
import jax.numpy as jnp
from jax import lax
from jax.experimental import pallas as pl
from jax.experimental.pallas import tpu as pltpu

DEPTH = 2
PAST_LEN = 16384
RET_HEADS = 4
RET_ROPE_THETA = 10000.0
SWA_HEAD_DIM = 64
SWA_KV_HEADS = 4
WINDOW = 128
ROPE_THETA = 500000.0
ROT_DIM = SWA_HEAD_DIM // 4
EPS = 1e-6
NEG = -1e30
LOG2E = 1.4426950408889634

LANES = 128
SUBLANES = 8
BF16_SUBLANES = 16
VMEM_CAP_BYTES = 64 * 1024 * 1024
VMEM_BUDGET_BYTES = VMEM_CAP_BYTES - 8 * 1024 * 1024

TOKEN_TILE = 512
COL_CHUNK = 1024
RET_KERNEL_CHUNK = 256
SAMPLE_GROUP = 2
ATTN_BLOCKS_PER_STEP = 4

F32 = jnp.float32
BF16 = jnp.bfloat16


def _params(semantics, vmem_bytes):
    limit = int(min(max(vmem_bytes, 16 * 1024 * 1024), VMEM_BUDGET_BYTES))
    return pltpu.CompilerParams(dimension_semantics=semantics, vmem_limit_bytes=limit)


def _resident(shape):
    nd = len(shape)
    return pl.BlockSpec(shape, lambda *_: (0,) * nd, pipeline_mode=pl.Buffered(1))


def _nbytes(shape, dtype):
    n = 1
    for s in shape:
        n *= s
    return n * jnp.dtype(dtype).itemsize


def _rms_rows(x):
    return x * lax.rsqrt(jnp.mean(x * x, axis=-1, keepdims=True) + EPS)


def _dot(a, b):
    return jnp.dot(a, b, preferred_element_type=F32)


def _dot_nt(a, b):
    return lax.dot_general(a, b, (((1,), (1,)), ((), ())), preferred_element_type=F32)


def _dot_tn(a, b):
    return lax.dot_general(a, b, (((0,), (0,)), ((), ())), preferred_element_type=F32)


def _ret_in_body(h_ref, g_ref, w_ref, cos_ref, sin_ref, kscale_ref, q_ref, kd_ref, v_ref, sg_ref, *, heads, dk, dv):
    xn = (_rms_rows(h_ref[...]) * g_ref[...]).astype(BF16)
    cos = cos_ref[...]
    sin = sin_ref[...]
    half = dk // 2
    qk_w = heads * dk
    v_w = heads * dv

    def proj(lo, width):
        return _dot(xn, w_ref[:, lo:lo + width])

    for base, ref, scale_ref in ((0, q_ref, None), (qk_w, kd_ref, kscale_ref)):
        p = proj(base, qk_w)
        for hh in range(heads):
            lo, mid, hi = hh * dk, hh * dk + half, (hh + 1) * dk
            x1 = p[:, lo:mid]
            x2 = p[:, mid:hi]
            o1 = x1 * cos - x2 * sin
            o2 = x2 * cos + x1 * sin
            if scale_ref is not None:
                o1 = o1 * scale_ref[:, lo:mid]
                o2 = o2 * scale_ref[:, mid:hi]
            ref[hh, :, :half] = o1.astype(ref.dtype)
            ref[hh, :, half:] = o2.astype(ref.dtype)
    cw = min(COL_CHUNK, v_w)
    per_chunk = cw // dv
    for c in range(v_w // cw):
        v = proj(2 * qk_w + c * cw, cw)
        for j in range(per_chunk):
            v_ref[c * per_chunk + j] = v[:, j * dv:(j + 1) * dv].astype(v_ref.dtype)
    for c in range(v_w // cw):
        gate = proj(2 * qk_w + v_w + c * cw, cw)
        sg = gate * jax.nn.sigmoid(gate)
        for j in range(per_chunk):
            sg_ref[c * per_chunk + j] = sg[:, j * dv:(j + 1) * dv].astype(sg_ref.dtype)


def _ret_in(h, g, w_in, cos, sin, kscale, *, heads, dk, dv, out_dtype):
    n, d = h.shape
    tm = min(TOKEN_TILE, n)
    qk_w, v_w = heads * dk, heads * dv
    pos_tiles = cos.shape[0] // tm
    row = lambda i: (i, 0)
    tab = lambda i: (i % pos_tiles, 0)
    by_head = lambda i: (0, i, 0)
    vmem = (2 * _nbytes((tm, d), F32) + _nbytes(w_in.shape, BF16) + 4 * _nbytes((tm, dk // 2), F32)
            + _nbytes((tm, qk_w), F32) + 2 * _nbytes((tm, 2 * qk_w + 2 * v_w), out_dtype)
            + 4 * _nbytes((tm, COL_CHUNK), F32))
    return pl.pallas_call(
        functools.partial(_ret_in_body, heads=heads, dk=dk, dv=dv),
        grid=(n // tm,),
        in_specs=[pl.BlockSpec((tm, d), row), _resident((1, d)), _resident(w_in.shape),
                  pl.BlockSpec((tm, dk // 2), tab), pl.BlockSpec((tm, dk // 2), tab), _resident((tm, qk_w))],
        out_specs=[pl.BlockSpec((heads, tm, dk), by_head), pl.BlockSpec((heads, tm, dk), by_head),
                   pl.BlockSpec((heads, tm, dv), by_head), pl.BlockSpec((heads, tm, dv), by_head)],
        out_shape=[jax.ShapeDtypeStruct((heads, n, dk), out_dtype), jax.ShapeDtypeStruct((heads, n, dk), out_dtype),
                   jax.ShapeDtypeStruct((heads, n, dv), out_dtype), jax.ShapeDtypeStruct((heads, n, dv), out_dtype)],
        compiler_params=_params(("parallel",), vmem),
        name="ret_in",
    )(h, g, w_in, cos, sin, kscale)


def _ret_prompt_body(lg_ref, h_ref, g_ref, w_ref, cos_ref, sin_ref, kscale_ref, o_ref, s_out_ref,
                     q_s, kd_s, v_s, sg_s, s_ref, *, heads, dk, dv, chunk):
    t = pl.program_id(1)

    @pl.when(t == 0)
    def _():
        s_ref[...] = jnp.zeros_like(s_ref)

    _ret_in_body(h_ref, g_ref, w_ref, cos_ref, sin_ref, kscale_ref, q_s, kd_s, v_s, sg_s, heads=heads, dk=dk, dv=dv)

    tm = h_ref.shape[0]
    ri = lax.broadcasted_iota(jnp.int32, (chunk, chunk), 0)
    ci = lax.broadcasted_iota(jnp.int32, (chunk, chunk), 1)
    lower = (ri >= ci).astype(F32)
    row_v = lax.broadcasted_iota(jnp.int32, (chunk, dv), 0).astype(F32)
    causal, q_decay, chunk_decay = [], [], []
    for hh in range(heads):
        lg = lg_ref[hh]
        causal.append(lower * jnp.exp(jnp.full((1, chunk), -lg * chunk, F32)))
        q_decay.append(jnp.exp(lg * (row_v + 1.0)))
        chunk_decay.append(jnp.exp(jnp.full((1, dv), lg * chunk, F32)))

    for c in range(tm // chunk):
        rows = pl.ds(c * chunk, chunk)
        qk = [_dot_nt(q_s[hh, rows, :], kd_s[hh, rows, :]) for hh in range(heads)]
        grow = [_dot_tn(kd_s[hh, rows, :], v_s[hh, rows, :]) for hh in range(heads)]
        for hh in range(heads):
            s_prev = s_ref[hh]
            lhs = jnp.concatenate([(qk[hh] * causal[hh]).astype(BF16), q_s[hh, rows, :]], axis=1)
            rhs = jnp.concatenate([v_s[hh, rows, :], s_prev.astype(BF16)], axis=0)
            o = _rms_rows(q_decay[hh] * _dot(lhs, rhs))
            s_ref[hh] = chunk_decay[hh] * s_prev + grow[hh]
            o_ref[rows, hh * dv:(hh + 1) * dv] = (o * sg_s[hh, rows, :].astype(F32)).astype(o_ref.dtype)

    @pl.when(t == pl.num_programs(1) - 1)
    def _():
        s_out_ref[0] = s_ref[...]


def _ret_chunk(seq):
    return RET_KERNEL_CHUNK if seq % RET_KERNEL_CHUNK == 0 else seq


def _ret_prompt(log_g, h, g, w_in, cos, sin, kscale, *, batch, heads, dk, dv):
    n, d = h.shape
    seq = n // batch
    tm = min(TOKEN_TILE, seq)
    chunk = _ret_chunk(seq)
    assert tm % chunk == 0 and seq % tm == 0
    nt = seq // tm
    qk_w, v_w = heads * dk, heads * dv
    row = lambda b, t: (b * nt + t, 0)
    tab = lambda b, t: (t, 0)
    vmem = (2 * _nbytes((tm, d), F32) + _nbytes(w_in.shape, BF16) + 4 * _nbytes((tm, dk // 2), F32)
            + _nbytes((tm, qk_w), F32) + 2 * _nbytes((tm, v_w), BF16) + _nbytes((tm, 2 * qk_w + 2 * v_w), BF16)
            + 3 * _nbytes((heads, dk, dv), F32) + 4 * _nbytes((tm, COL_CHUNK), F32)
            + 2 * heads * (_nbytes((chunk, chunk), F32) + _nbytes((dk, dv), F32) + _nbytes((chunk, dv), F32)))
    return pl.pallas_call(
        functools.partial(_ret_prompt_body, heads=heads, dk=dk, dv=dv, chunk=chunk),
        grid=(batch, nt),
        in_specs=[pl.BlockSpec(memory_space=pltpu.SMEM),
                  pl.BlockSpec((tm, d), row), _resident((1, d)), _resident(w_in.shape),
                  pl.BlockSpec((tm, dk // 2), tab), pl.BlockSpec((tm, dk // 2), tab), _resident((tm, qk_w))],
        out_specs=[pl.BlockSpec((tm, v_w), row),
                   pl.BlockSpec((1, heads, dk, dv), lambda b, t: (b, 0, 0, 0))],
        out_shape=[jax.ShapeDtypeStruct((n, v_w), BF16),
                   jax.ShapeDtypeStruct((batch, heads, dk, dv), F32)],
        scratch_shapes=[pltpu.VMEM((heads, tm, dk), BF16), pltpu.VMEM((heads, tm, dk), BF16),
                        pltpu.VMEM((heads, tm, dv), BF16), pltpu.VMEM((heads, tm, dv), BF16),
                        pltpu.VMEM((heads, dk, dv), F32)],
        compiler_params=_params(("parallel", "arbitrary"), vmem),
        name="ret_prompt",
    )(log_g, h, g, w_in, cos, sin, kscale)


def _ret_sample_body(lg_ref, q_ref, kd_ref, v_ref, sg_ref, s_in_ref, o_ref, s_out_ref, *, seq):
    heads, rows, dk = q_ref.shape
    dv = v_ref.shape[2]
    group = rows // seq
    ri = lax.broadcasted_iota(jnp.int32, (rows, rows), 0)
    ci = lax.broadcasted_iota(jnp.int32, (rows, rows), 1)
    visible = ((ri // seq) == (ci // seq)) & (ri >= ci)
    row_v = lax.broadcasted_iota(jnp.int32, (rows, dv), 0)
    row_k = lax.broadcasted_iota(jnp.int32, (rows, dk), 0)
    for hh in range(heads):
        lg = lg_ref[hh]
        causal = jnp.where(visible, jnp.exp(jnp.full((rows, rows), -lg * seq, F32)), 0.0)
        q_decay = jnp.exp(lg * ((row_v % seq).astype(F32) + 1.0))
        chunk_decay = jnp.exp(jnp.full((1, dv), lg * seq, F32))
        q = q_ref[hh].astype(BF16)
        kd = kd_ref[hh]
        v = v_ref[hh].astype(BF16)
        o = _dot((_dot_nt(q, kd.astype(BF16)) * causal).astype(BF16), v)
        for g in range(group):
            s_prev = s_in_ref[g, hh]
            o = jnp.where((row_v // seq) == g, o + _dot(q, s_prev.astype(BF16)), o)
            kd_g = jnp.where((row_k // seq) == g, kd, 0.0).astype(BF16)
            s_out_ref[g, hh] = chunk_decay * s_prev + _dot_tn(kd_g, v)
        o = _rms_rows(q_decay * o)
        o_ref[hh] = (o * sg_ref[hh]).astype(o_ref.dtype)


def _ret_sample(log_g, q, kd, v, sg, state, *, seq):
    heads, n, dk = q.shape
    dv = v.shape[2]
    batch = n // seq
    group = SAMPLE_GROUP if batch % SAMPLE_GROUP == 0 else batch
    rows = group * seq
    by_head = lambda i: (0, i, 0)
    st = lambda i: (i, 0, 0, 0)
    vmem = (4 * _nbytes((group, heads, dk, dv), F32) + 8 * _nbytes((rows, heads * dv), F32)
            + 4 * _nbytes((dk, dv), F32))
    return pl.pallas_call(
        functools.partial(_ret_sample_body, seq=seq),
        grid=(batch // group,),
        in_specs=[pl.BlockSpec(memory_space=pltpu.SMEM),
                  pl.BlockSpec((heads, rows, dk), by_head), pl.BlockSpec((heads, rows, dk), by_head),
                  pl.BlockSpec((heads, rows, dv), by_head), pl.BlockSpec((heads, rows, dv), by_head),
                  pl.BlockSpec((group, heads, dk, dv), st)],
        out_specs=[pl.BlockSpec((heads, rows, dv), by_head), pl.BlockSpec((group, heads, dk, dv), st)],
        out_shape=[jax.ShapeDtypeStruct((heads, n, dv), F32),
                   jax.ShapeDtypeStruct((batch, heads, dk, dv), F32)],
        compiler_params=_params(("parallel",), vmem),
        name="ret_sample",
    )(log_g, q, kd, v, sg, state)


def _out_ffn_body(o_ref, h_ref, wo_ref, g_post_ref, g_pre_ref, w1_ref, w2_ref, g_ffn_ref, y_ref, *, o_layout):
    if o_layout == "features":
        a = _dot_tn(o_ref[0], wo_ref[...])
    elif o_layout == "heads":
        o = jnp.concatenate([o_ref[hh] for hh in range(o_ref.shape[0])], axis=1)
        a = _dot(o.astype(BF16), wo_ref[...])
    else:
        a = _dot(o_ref[...].astype(BF16), wo_ref[...])
    h1 = h_ref[...] + _rms_rows(a) * g_post_ref[...]
    x = (_rms_rows(h1) * g_pre_ref[...]).astype(BF16)
    d_ff = w1_ref.shape[1]
    fc = min(COL_CHUNK, d_ff)
    acc = jnp.zeros(h1.shape, F32)
    for c in range(d_ff // fc):
        u = jnp.maximum(_dot(x, w1_ref[:, c * fc:(c + 1) * fc]), 0.0)
        acc = acc + _dot((u * u).astype(BF16), w2_ref[c * fc:(c + 1) * fc, :])
    y_ref[...] = h1 + _rms_rows(acc) * g_ffn_ref[...]


def _out_ffn(o, h, w_o, g_post, g_pre, w1, w2, g_ffn, *, o_layout):
    n, d = h.shape
    kdim = w_o.shape[0]
    tm = min(TOKEN_TILE, n)
    row = lambda i: (i, 0)
    if o_layout == "features":
        tiles = o.shape[2] // tm
        o_spec = pl.BlockSpec((1, kdim, tm), lambda i: (i // tiles, 0, i % tiles))
    elif o_layout == "heads":
        o_spec = pl.BlockSpec((o.shape[0], tm, o.shape[2]), lambda i: (0, i, 0))
    else:
        o_spec = pl.BlockSpec((tm, kdim), row)
    vmem = (2 * _nbytes((tm, kdim), o.dtype) + 4 * _nbytes((tm, d), F32) + _nbytes(w_o.shape, BF16)
            + _nbytes(w1.shape, BF16) + _nbytes(w2.shape, BF16) + 6 * _nbytes((tm, COL_CHUNK), F32))
    return pl.pallas_call(
        functools.partial(_out_ffn_body, o_layout=o_layout),
        grid=(n // tm,),
        in_specs=[o_spec, pl.BlockSpec((tm, d), row), _resident(w_o.shape),
                  _resident((1, d)), _resident((1, d)), _resident(w1.shape), _resident(w2.shape),
                  _resident((1, d))],
        out_specs=pl.BlockSpec((tm, d), row),
        out_shape=jax.ShapeDtypeStruct((n, d), F32),
        compiler_params=_params(("parallel",), vmem),
        name="out_ffn",
    )(o, h, w_o, g_post, g_pre, w1, w2, g_ffn)


def _partial_rope(x, c_tab, sa_tab, sb_tab):
    half = ROT_DIM // 2
    outs = []
    for j in range(x.shape[1] // LANES):
        s = x[:, j * LANES:(j + 1) * LANES]
        outs.append(s * c_tab + pltpu.roll(s, LANES - half, axis=1) * sa_tab + pltpu.roll(s, half, axis=1) * sb_tab)
    return outs


def _swa_in_body(h_ref, g_q_ref, g_kv_ref, wq_ref, wkv_ref, c_ref, sa_ref, sb_ref, q_ref, k_ref, v_ref):
    y = _rms_rows(h_ref[...])
    xq = (y * g_q_ref[...]).astype(BF16)
    xkv = (y * g_kv_ref[...]).astype(BF16)
    c_tab, sa_tab, sb_tab = c_ref[...], sa_ref[...], sb_ref[...]
    q = _dot(xq, wq_ref[...]) * (SWA_HEAD_DIM ** -0.5)
    for j, s in enumerate(_partial_rope(q, c_tab, sa_tab, sb_tab)):
        q_ref[:, j * LANES:(j + 1) * LANES] = s.astype(q_ref.dtype)
    kv = _dot(xkv, wkv_ref[...])
    kw = k_ref.shape[1]
    for j, s in enumerate(_partial_rope(kv[:, :kw], c_tab, sa_tab, sb_tab)):
        k_ref[:, j * LANES:(j + 1) * LANES] = s
    v_ref[...] = kv[:, kw:]


def _swa_in(h, g_q, g_kv, w_q, w_kv, c_tab, sa_tab, sb_tab, *, q_dtype):
    n, d = h.shape
    tm = min(TOKEN_TILE, n)
    qw = w_q.shape[1]
    kw = w_kv.shape[1] // 2
    pos_tiles = c_tab.shape[0] // tm
    row = lambda i: (i, 0)
    tab = lambda i: (i % pos_tiles, 0)
    vmem = (2 * _nbytes((tm, d), F32) + _nbytes(w_q.shape, BF16) + _nbytes(w_kv.shape, BF16)
            + 6 * _nbytes((tm, LANES), F32) + 2 * _nbytes((tm, qw), q_dtype) + 4 * _nbytes((tm, kw), F32)
            + 6 * _nbytes((tm, qw), F32))
    return pl.pallas_call(
        _swa_in_body,
        grid=(n // tm,),
        in_specs=[pl.BlockSpec((tm, d), row), _resident((1, d)), _resident((1, d)),
                  _resident(w_q.shape), _resident(w_kv.shape),
                  pl.BlockSpec((tm, LANES), tab), pl.BlockSpec((tm, LANES), tab), pl.BlockSpec((tm, LANES), tab)],
        out_specs=[pl.BlockSpec((tm, qw), row), pl.BlockSpec((tm, kw), row), pl.BlockSpec((tm, kw), row)],
        out_shape=[jax.ShapeDtypeStruct((n, qw), q_dtype), jax.ShapeDtypeStruct((n, kw), F32),
                   jax.ShapeDtypeStruct((n, kw), F32)],
        compiler_params=_params(("parallel",), vmem),
        name="swa_in",
    )(h, g_q, g_kv, w_q, w_kv, c_tab, sa_tab, sb_tab)


def _swa_in_t_body(h_ref, g_q_ref, g_kv_ref, wqt_ref, wk_ref, wvt_ref, wv_ref, cos_t_ref, sin_t_ref,
                   c_ref, sa_ref, sb_ref, qt_ref, k_ref, vt_ref, kwin_ref, vwin_ref, *, tiles):
    y = _rms_rows(h_ref[...])
    xq = (y * g_q_ref[...]).astype(BF16)
    xkv = (y * g_kv_ref[...]).astype(BF16)
    tm = xq.shape[0]
    hd = SWA_HEAD_DIM
    half = ROT_DIM // 2
    cos_t, sin_t = cos_t_ref[...], sin_t_ref[...]
    qt = _dot_nt(wqt_ref[...], xq) * (hd ** -0.5 * LOG2E)
    for hq in range(qt.shape[0] // hd):
        base = hq * hd
        x1 = qt[base:base + half]
        x2 = qt[base + half:base + 2 * half]
        rot = jnp.concatenate([x1 * cos_t - x2 * sin_t, x2 * cos_t + x1 * sin_t], axis=0)
        qt_ref[0, base:base + 2 * half, :] = rot.astype(qt_ref.dtype)
        qt_ref[0, base + 2 * half:base + hd, :] = qt[base + 2 * half:base + hd].astype(qt_ref.dtype)
    k_rot = _partial_rope(_dot(xkv, wk_ref[...]), c_ref[...], sa_ref[...], sb_ref[...])
    for j, s in enumerate(k_rot):
        k_ref[:, j * LANES:(j + 1) * LANES] = s.astype(k_ref.dtype)
    vt_ref[0] = _dot_nt(wvt_ref[...], xkv).astype(vt_ref.dtype)
    win = kwin_ref.shape[0]

    @pl.when(pl.program_id(0) % tiles == tiles - 1)
    def _():
        for j, s in enumerate(k_rot):
            kwin_ref[:, j * LANES:(j + 1) * LANES] = s[tm - win:, :]
        vwin_ref[...] = _dot(xkv[tm - win:, :], wv_ref[...])


def _swa_in_t(h, g_q, g_kv, w_q, w_kv, cos_t, sin_t, c_tab, sa_tab, sb_tab, *, batch, win):
    n, d = h.shape
    seq = n // batch
    tm = min(TOKEN_TILE, seq)
    tiles = seq // tm
    qw = w_q.shape[1]
    kw = w_kv.shape[1] // 2
    wqt = w_q.T
    wk, wv = w_kv[:, :kw], w_kv[:, kw:]
    wvt = wv.T
    row = lambda i: (i, 0)
    tab = lambda i: (i % tiles, 0)
    tab_t = lambda i: (0, i % tiles)
    feat = lambda i: (i // tiles, 0, i % tiles)
    per_seq = lambda i: (i // tiles, 0)
    half = ROT_DIM // 2
    vmem = (2 * _nbytes((tm, d), F32) + 2 * _nbytes(w_q.shape, BF16) + 3 * _nbytes(w_kv.shape, BF16)
            + 8 * _nbytes((tm, LANES), F32) + 2 * _nbytes((tm, qw + 2 * kw), BF16) + 4 * _nbytes((win, kw), F32)
            + 4 * _nbytes((tm, qw), F32))
    return pl.pallas_call(
        functools.partial(_swa_in_t_body, tiles=tiles),
        grid=(n // tm,),
        in_specs=[pl.BlockSpec((tm, d), row), _resident((1, d)), _resident((1, d)),
                  _resident(wqt.shape), _resident(wk.shape), _resident(wvt.shape), _resident(wv.shape),
                  pl.BlockSpec((half, tm), tab_t), pl.BlockSpec((half, tm), tab_t),
                  pl.BlockSpec((tm, LANES), tab), pl.BlockSpec((tm, LANES), tab), pl.BlockSpec((tm, LANES), tab)],
        out_specs=[pl.BlockSpec((1, qw, tm), feat), pl.BlockSpec((tm, kw), row), pl.BlockSpec((1, kw, tm), feat),
                   pl.BlockSpec((win, kw), per_seq), pl.BlockSpec((win, kw), per_seq)],
        out_shape=[jax.ShapeDtypeStruct((batch, qw, seq), BF16), jax.ShapeDtypeStruct((n, kw), BF16),
                   jax.ShapeDtypeStruct((batch, kw, seq), BF16),
                   jax.ShapeDtypeStruct((batch * win, kw), F32), jax.ShapeDtypeStruct((batch * win, kw), F32)],
        compiler_params=_params(("arbitrary",), vmem),
        name="swa_in_t",
    )(h, g_q, g_kv, wqt, wk, wvt, wv, cos_t, sin_t, c_tab, sa_tab, sb_tab)


def _sink_rows(sinks_ref, kvh, group, rows_per_head, shape):
    r = lax.broadcasted_iota(jnp.int32, shape, 0) // rows_per_head
    col = jnp.full(shape, sinks_ref[kvh * group], F32)
    for g in range(1, group):
        col = jnp.where(r == g, sinks_ref[kvh * group + g], col)
    return col


def _attn_prompt_body(sinks_ref, mask_ref, qt_ref, kp_ref, kc_ref, vtp_ref, vtc_ref, ot_ref, *, group):
    blk = kp_ref.shape[0]
    nblk = kc_ref.shape[0] // blk
    hd = SWA_HEAD_DIM
    kvh_n = kc_ref.shape[1] // hd
    cols = group * blk
    kj = lax.broadcasted_iota(jnp.int32, (blk, cols), 0)
    qi = lax.broadcasted_iota(jnp.int32, (blk, cols), 1) % blk
    own = kj <= qi
    lane_head = lax.broadcasted_iota(jnp.int32, (1, cols), 1) // blk
    ones_rows = jnp.ones((BF16_SUBLANES, blk), BF16)
    has_prev = pl.program_id(1) > 0

    def scores(j, kvh):
        tile, lo = divmod(kvh * hd, LANES)
        q4t = jnp.concatenate([qt_ref[0, (kvh * group + g) * hd:(kvh * group + g + 1) * hd, j * blk:(j + 1) * blk]
                               for g in range(group)], axis=1)
        rhs = jnp.concatenate([q4t if part * hd == lo else jnp.zeros_like(q4t) for part in range(LANES // hd)], axis=0)
        k_tile = slice(tile * LANES, (tile + 1) * LANES)
        k_prev = kp_ref[:, k_tile] if j == 0 else kc_ref[(j - 1) * blk:j * blk, k_tile]
        return _dot(kc_ref[j * blk:(j + 1) * blk, k_tile], rhs), _dot(k_prev, rhs)

    def finish(j, kvh, s_own, s_prev):
        if j == 0:
            s_prev = jnp.where(has_prev, s_prev, NEG)
        s = jnp.where(own, s_own, s_prev)
        sink = jnp.full((1, cols), sinks_ref[kvh * group] * LOG2E, F32)
        for g in range(1, group):
            sink = jnp.where(lane_head == g, sinks_ref[kvh * group + g] * LOG2E, sink)
        m = jnp.maximum(jnp.max(s, axis=0, keepdims=True), sink)
        e = jnp.exp2(s - m).astype(BF16)
        p_own = e * mask_ref[...]
        p = jnp.concatenate([p_own, e - p_own], axis=0)
        head_rows = slice(kvh * hd, (kvh + 1) * hd)
        vt_own = vtc_ref[0, head_rows, j * blk:(j + 1) * blk]
        vt_prev = vtp_ref[0, head_rows, :] if j == 0 else vtc_ref[0, head_rows, (j - 1) * blk:j * blk]
        vt = jnp.concatenate([jnp.concatenate([vt_own, ones_rows], axis=0),
                              jnp.concatenate([vt_prev, ones_rows], axis=0)], axis=1)
        acc = _dot(vt, p)
        denom = acc[hd:hd + 1, :] + jnp.exp2(sink - m)
        ot = acc[:hd] / denom
        for g in range(group):
            hq = kvh * group + g
            ot_ref[0, hq * hd:(hq + 1) * hd, j * blk:(j + 1) * blk] = ot[:, g * blk:(g + 1) * blk].astype(ot_ref.dtype)

    units = [(j, kvh) for j in range(nblk) for kvh in range(kvh_n)]
    pending = scores(*units[0])
    for idx, unit in enumerate(units):
        upcoming = scores(*units[idx + 1]) if idx + 1 < len(units) else None
        finish(*unit, *pending)
        pending = upcoming


def _attn_prompt(sinks, qt, k, vt):
    batch, qw, seq = qt.shape
    kw = k.shape[1]
    blk = WINDOW
    nblk = ATTN_BLOCKS_PER_STEP if seq % (ATTN_BLOCKS_PER_STEP * blk) == 0 else 1
    span = nblk * blk
    steps = seq // span
    group = qw // kw
    cols = group * blk
    own = (jnp.arange(blk)[:, None] <= (jnp.arange(cols) % blk)[None, :]).astype(BF16)
    cur_t = lambda b, i: (b, 0, i)
    prev_t = lambda b, i: (b, 0, jnp.maximum(i * nblk - 1, 0))
    cur = lambda b, i: (b * steps + i, 0)
    prev = lambda b, i: (b * steps * nblk + jnp.maximum(i * nblk - 1, 0), 0)
    vmem = (4 * _nbytes((qw, span), BF16) + 6 * _nbytes((span, kw), BF16) + 16 * _nbytes((blk, cols), F32))
    return pl.pallas_call(
        functools.partial(_attn_prompt_body, group=group),
        grid=(batch, steps),
        in_specs=[pl.BlockSpec(memory_space=pltpu.SMEM), _resident((blk, cols)), pl.BlockSpec((1, qw, span), cur_t),
                  pl.BlockSpec((blk, kw), prev), pl.BlockSpec((span, kw), cur),
                  pl.BlockSpec((1, kw, blk), prev_t), pl.BlockSpec((1, kw, span), cur_t)],
        out_specs=pl.BlockSpec((1, qw, span), cur_t),
        out_shape=jax.ShapeDtypeStruct((batch, qw, seq), BF16),
        compiler_params=_params(("parallel", "parallel"), vmem),
        name="attn_prompt",
    )(sinks, own, qt, k, k, vt, vt)


def _attn_sample_body(sinks_ref, q_ref, kn_ref, vn_ref, kc_ref, vc_ref, o_ref, kw_ref, vw_ref, *, seq, group, q_start):
    rows = q_ref.shape[0]
    bgroup = rows // seq
    win = kc_ref.shape[1]
    hd = SWA_HEAD_DIM
    kvh_n = kn_ref.shape[1] // hd
    srows = group * rows
    kn = kn_ref[...]
    vn = vn_ref[...]
    r_c = lax.broadcasted_iota(jnp.int32, (srows, win), 0) % rows
    c_c = lax.broadcasted_iota(jnp.int32, (srows, win), 1)
    rel_c = (r_c % seq) + win - c_c
    ok_c = (rel_c >= 0) & (rel_c < WINDOW) & (q_start - win + c_c >= 0)
    r_n = lax.broadcasted_iota(jnp.int32, (srows, rows), 0) % rows
    c_n = lax.broadcasted_iota(jnp.int32, (srows, rows), 1)
    rel_n = (r_n % seq) - (c_n % seq)
    ok_n = (rel_n >= 0) & (rel_n < WINDOW) & ((r_n // seq) == (c_n // seq))
    owner = (lax.broadcasted_iota(jnp.int32, (srows, hd), 0) % rows) // seq
    for kvh in range(kvh_n):
        cols = slice(kvh * hd, (kvh + 1) * hd)
        q = jnp.concatenate([q_ref[:, (kvh * group + g) * hd:(kvh * group + g + 1) * hd] for g in range(group)],
                            axis=0).astype(BF16)
        sink = _sink_rows(sinks_ref, kvh, group, rows, (srows, 1))
        s_n = jnp.where(ok_n, _dot_nt(q, kn[:, cols].astype(BF16)), NEG)
        m_n = jnp.max(s_n, axis=-1, keepdims=True)
        o = jnp.zeros((srows, hd), F32)
        for b in range(bgroup):
            s_c = jnp.where(ok_c, _dot_nt(q, kc_ref[b, :, cols].astype(BF16)), NEG)
            m = jnp.maximum(jnp.maximum(jnp.max(s_c, axis=-1, keepdims=True), m_n), sink)
            e_c = jnp.exp(s_c - m)
            e_n = jnp.exp(s_n - m)
            denom = jnp.sum(e_c, axis=-1, keepdims=True) + jnp.sum(e_n, axis=-1, keepdims=True) + jnp.exp(sink - m)
            o_b = (_dot(e_c.astype(BF16), vc_ref[b, :, cols].astype(BF16))
                   + _dot(e_n.astype(BF16), vn[:, cols].astype(BF16))) / denom
            o = jnp.where(owner == b, o_b, o)
        for g in range(group):
            hq = kvh * group + g
            o_ref[:, hq * hd:(hq + 1) * hd] = o[g * rows:(g + 1) * rows].astype(o_ref.dtype)
    for b in range(bgroup):
        kw_ref[b, 0:win - seq, :] = kc_ref[b, seq:win, :]
        kw_ref[b, win - seq:win, :] = kn[b * seq:(b + 1) * seq, :]
        vw_ref[b, 0:win - seq, :] = vc_ref[b, seq:win, :]
        vw_ref[b, win - seq:win, :] = vn[b * seq:(b + 1) * seq, :]


def _attn_sample(sinks, q, k_new, v_new, k_cache, v_cache, *, seq, q_start):
    n, qw = q.shape
    kw = k_new.shape[1]
    batch, win, _ = k_cache.shape
    group = qw // kw
    bgroup = SAMPLE_GROUP if batch % SAMPLE_GROUP == 0 else batch
    rows = bgroup * seq
    row = lambda i: (i, 0)
    cache = lambda i: (i, 0, 0)
    vmem = 8 * _nbytes((bgroup, win, kw), F32) + 8 * _nbytes((rows, qw), F32) + 16 * _nbytes((group * rows, win), F32)
    return pl.pallas_call(
        functools.partial(_attn_sample_body, seq=seq, group=group, q_start=q_start),
        grid=(batch // bgroup,),
        in_specs=[pl.BlockSpec(memory_space=pltpu.SMEM), pl.BlockSpec((rows, qw), row),
                  pl.BlockSpec((rows, kw), row), pl.BlockSpec((rows, kw), row),
                  pl.BlockSpec((bgroup, win, kw), cache), pl.BlockSpec((bgroup, win, kw), cache)],
        out_specs=[pl.BlockSpec((rows, qw), row), pl.BlockSpec((bgroup, win, kw), cache),
                   pl.BlockSpec((bgroup, win, kw), cache)],
        out_shape=[jax.ShapeDtypeStruct((n, qw), F32), jax.ShapeDtypeStruct((batch, win, kw), F32),
                   jax.ShapeDtypeStruct((batch, win, kw), F32)],
        compiler_params=_params(("parallel",), vmem),
        name="attn_sample",
    )(sinks, q, k_new, v_new, k_cache, v_cache)


def _ret_rope_tables(pos, dk):
    inv = 1.0 / (RET_ROPE_THETA ** jnp.linspace(0.0, 1.0, dk // 2, dtype=F32))
    ang = pos[:, None] * inv[None, :]
    return jnp.cos(ang), jnp.sin(ang)


def _ret_key_scale(log_g, seq, n, dk):
    chunk = _ret_chunk(seq)
    tm = min(TOKEN_TILE, n)
    assert tm % chunk == 0
    left = (chunk - 1 - jnp.arange(tm) % chunk).astype(F32)
    per_head = jnp.exp(log_g[None, :] * left[:, None]) * dk ** -0.5
    return jnp.repeat(per_head, dk, axis=1)


def _swa_cos_sin(pos):
    half = ROT_DIM // 2
    inv = ROPE_THETA ** (-jnp.arange(half, dtype=F32) / half)
    ang = pos[:, None] * inv[None, :]
    return jnp.cos(ang), jnp.sin(ang)


def _swa_rope_tables(pos):
    half = ROT_DIM // 2
    cos, sin = _swa_cos_sin(pos)
    n = pos.shape[0]
    pad = jnp.zeros((n, SWA_HEAD_DIM - 2 * half), F32)
    c_head = jnp.concatenate([cos, cos, pad + 1.0], axis=1)
    sa_head = jnp.concatenate([-sin, jnp.zeros_like(sin), pad], axis=1)
    sb_head = jnp.concatenate([jnp.zeros_like(sin), sin, pad], axis=1)
    reps = LANES // SWA_HEAD_DIM
    return tuple(jnp.tile(t, (1, reps)) for t in (c_head, sa_head, sb_head))


def _tile_rows(tab, seq, n):
    tm = min(TOKEN_TILE, n)
    return tab if seq >= tm else jnp.tile(tab, (tm // seq, 1))


def kernel(x_prompt, x_sample, state_ret, cache_k_win, cache_v_win, ret_norm_pre, ret_w_in, ret_w_out, ret_norm_post, kv_norm, w_kv, swa_norm_pre, swa_w_q, swa_sinks, swa_w_o, swa_norm_post, ffn_norm_pre, ffn_w1, ffn_w2, ffn_norm_post):
    n_a = DEPTH // 2
    assert n_a == 1 and DEPTH == 2, "one retention layer followed by one sliding-window layer"
    d = x_prompt.shape[-1]
    heads = RET_HEADS
    dk = ret_w_out.shape[-1] // heads
    dv = ret_w_out.shape[-2] // heads
    kvh, hd = SWA_KV_HEADS, SWA_HEAD_DIM
    row2 = lambda g: g.reshape(1, d)
    log_g = jnp.log1p(-jnp.exp2(-5.0 - jnp.arange(heads, dtype=F32)))

    w_in = ret_w_in[0].astype(BF16)
    w_out = ret_w_out[0].astype(BF16)
    wq = swa_w_q[0].astype(BF16)
    wkv = w_kv.astype(BF16)
    wo = swa_w_o[0].astype(BF16)
    w1 = ffn_w1.astype(BF16)
    w2 = ffn_w2.astype(BF16)
    sinks = swa_sinks[0]

    def trunk(x, pos, ret_mixer, swa_mixer):
        b, t, _ = x.shape
        n = b * t
        h = x.reshape(n, d)
        cos, sin = (_tile_rows(tab, t, n) for tab in _ret_rope_tables(pos, dk))
        o, o_layout, state = ret_mixer(h, cos, sin, _ret_key_scale(log_g, t, n, dk))
        h = _out_ffn(o, h, w_out, row2(ret_norm_post[0]), row2(ffn_norm_pre[0]), w1[0], w2[0], row2(ffn_norm_post[0]),
                     o_layout=o_layout)
        o, o_layout, k_win, v_win = swa_mixer(h, pos, b, t)
        h = _out_ffn(o, h, wo, row2(swa_norm_post[0]), row2(ffn_norm_pre[1]), w1[1], w2[1], row2(ffn_norm_post[1]),
                     o_layout=o_layout)
        return h.reshape(b, t, d), state, k_win, v_win

    b_p, t_p, _ = x_prompt.shape
    w_p = min(WINDOW, t_p)

    def swa_prompt(h, pos, b, t):
        cos, sin = _swa_cos_sin(pos)
        qt, k, vt, k_win, v_win = _swa_in_t(h, row2(swa_norm_pre[0]), row2(kv_norm), wq, wkv, cos.T, sin.T,
                                            *_swa_rope_tables(pos), batch=b, win=w_p)
        return _attn_prompt(sinks, qt, k, vt), "features", k_win, v_win

    def ret_prompt(h, cos, sin, kscale):
        o, state = _ret_prompt(log_g, h, row2(ret_norm_pre[0]), w_in, cos, sin, kscale,
                               batch=b_p, heads=heads, dk=dk, dv=dv)
        return o, "tokens", state

    y_prompt, state_p, k_win_p, v_win_p = trunk(x_prompt, jnp.arange(t_p, dtype=F32), ret_prompt, swa_prompt)

    b_s, t_s, _ = x_sample.shape
    w_s = cache_k_win.shape[1]
    kc = cache_k_win.reshape(b_s, w_s, kvh * hd)
    vc = cache_v_win.reshape(b_s, w_s, kvh * hd)

    def swa_sample(h, pos, b, t):
        tabs = tuple(_tile_rows(tab, t, b * t) for tab in _swa_rope_tables(pos))
        q, k, v = _swa_in(h, row2(swa_norm_pre[0]), row2(kv_norm), wq, wkv, *tabs, q_dtype=F32)
        o, k_win, v_win = _attn_sample(sinks, q, k, v, kc, vc, seq=t, q_start=PAST_LEN)
        return o, "tokens", k_win, v_win

    def ret_sample(h, cos, sin, kscale):
        q, kd, v, sg = _ret_in(h, row2(ret_norm_pre[0]), w_in, cos, sin, kscale,
                               heads=heads, dk=dk, dv=dv, out_dtype=F32)
        o, state = _ret_sample(log_g, q, kd, v, sg, state_ret[0], seq=t_s)
        return o, "heads", state

    y_sample, state_s, k_win_s, v_win_s = trunk(x_sample, PAST_LEN + jnp.arange(t_s, dtype=F32), ret_sample, swa_sample)

    return (y_prompt, y_sample, state_p[None], state_s[None],
            k_win_p.reshape(b_p, w_p, kvh, hd), v_win_p.reshape(b_p, w_p, kvh, hd),
            k_win_s.reshape(b_s, w_s, kvh, hd), v_win_s.reshape(b_s, w_s, kvh, hd))
```

```python
import functools

import jax
import jax.numpy as jnp
from jax import lax
from jax.experimental import pallas as pl
from jax.experimental.pallas import tpu as pltpu

DEPTH = 2
PAST_LEN = 16384
RET_HEADS = 4
RET_ROPE_THETA = 10000.0
SWA_HEAD_DIM = 64
SWA_KV_HEADS = 4
WINDOW = 128
ROPE_THETA = 500000.0
ROT_DIM = SWA_HEAD_DIM // 4
EPS = 1e-6
NEG = -1e30
LOG2E = 1.4426950408889634

LANES = 128
SUBLANES = 8
BF16_SUBLANES = 16
VMEM_CAP_BYTES = 64 * 1024 * 1024
VMEM_BUDGET_BYTES = VMEM_CAP_BYTES - 8 * 1024 * 1024

TOKEN_TILE = 512
COL_CHUNK = 1024
RET_KERNEL_CHUNK = 256
SAMPLE_GROUP = 2
ATTN_SAMPLE_SEQS = 4
ATTN_BLOCKS_PER_STEP = 4
OUT_FFN_ROW_PARTS = 2

F32 = jnp.float32
BF16 = jnp.bfloat16


def _params(semantics, vmem_bytes):
    limit = int(min(max(vmem_bytes, 16 * 1024 * 1024), VMEM_BUDGET_BYTES))
    return pltpu.CompilerParams(dimension_semantics=semantics, vmem_limit_bytes=limit)


def _resident(shape):
    nd = len(shape)
    return pl.BlockSpec(shape, lambda *_: (0,) * nd, pipeline_mode=pl.Buffered(1))


def _nbytes(shape, dtype):
    n = 1
    for s in shape:
        n *= s
    return n * jnp.dtype(dtype).itemsize


def _rms_rows(x):
    return x * lax.rsqrt(jnp.mean(x * x, axis=-1, keepdims=True) + EPS)


def _dot(a, b):
    return jnp.dot(a, b, preferred_element_type=F32)


def _dot_nt(a, b):
    return lax.dot_general(a, b, (((1,), (1,)), ((), ())), preferred_element_type=F32)


def _dot_tn(a, b):
    return lax.dot_general(a, b, (((0,), (0,)), ((), ())), preferred_element_type=F32)


def _ret_in_body(h_ref, g_ref, w_ref, cos_ref, sin_ref, kscale_ref, q_ref, kd_ref, v_ref, sg_ref, *, heads, dk, dv):
    xn = (_rms_rows(h_ref[...]) * g_ref[...]).astype(BF16)
    cos = cos_ref[...]
    sin = sin_ref[...]
    half = dk // 2
    qk_w = heads * dk
    v_w = heads * dv

    def proj(lo, width):
        return _dot(xn, w_ref[:, lo:lo + width])

    for base, ref, scale_ref in ((0, q_ref, None), (qk_w, kd_ref, kscale_ref)):
        p = proj(base, qk_w)
        for hh in range(heads):
            lo, mid, hi = hh * dk, hh * dk + half, (hh + 1) * dk
            x1 = p[:, lo:mid]
            x2 = p[:, mid:hi]
            o1 = x1 * cos - x2 * sin
            o2 = x2 * cos + x1 * sin
            if scale_ref is not None:
                o1 = o1 * scale_ref[:, lo:mid]
                o2 = o2 * scale_ref[:, mid:hi]
            ref[hh, :, :half] = o1.astype(ref.dtype)
            ref[hh, :, half:] = o2.astype(ref.dtype)
    cw = min(COL_CHUNK, v_w)
    per_chunk = cw // dv
    for c in range(v_w // cw):
        v = proj(2 * qk_w + c * cw, cw)
        for j in range(per_chunk):
            v_ref[c * per_chunk + j] = v[:, j * dv:(j + 1) * dv].astype(v_ref.dtype)
    for c in range(v_w // cw):
        gate = proj(2 * qk_w + v_w + c * cw, cw)
        sg = gate * jax.nn.sigmoid(gate)
        for j in range(per_chunk):
            sg_ref[c * per_chunk + j] = sg[:, j * dv:(j + 1) * dv].astype(sg_ref.dtype)


def _ret_in(h, g, w_in, cos, sin, kscale, *, heads, dk, dv, out_dtype):
    n, d = h.shape
    tm = min(TOKEN_TILE, n)
    qk_w, v_w = heads * dk, heads * dv
    pos_tiles = cos.shape[0] // tm
    row = lambda i: (i, 0)
    tab = lambda i: (i % pos_tiles, 0)
    by_head = lambda i: (0, i, 0)
    vmem = (2 * _nbytes((tm, d), F32) + _nbytes(w_in.shape, BF16) + 4 * _nbytes((tm, dk // 2), F32)
            + _nbytes((tm, qk_w), F32) + 2 * _nbytes((tm, 2 * qk_w + 2 * v_w), out_dtype)
            + 4 * _nbytes((tm, COL_CHUNK), F32))
    return pl.pallas_call(
        functools.partial(_ret_in_body, heads=heads, dk=dk, dv=dv),
        grid=(n // tm,),
        in_specs=[pl.BlockSpec((tm, d), row), _resident((1, d)), _resident(w_in.shape),
                  pl.BlockSpec((tm, dk // 2), tab), pl.BlockSpec((tm, dk // 2), tab), _resident((tm, qk_w))],
        out_specs=[pl.BlockSpec((heads, tm, dk), by_head), pl.BlockSpec((heads, tm, dk), by_head),
                   pl.BlockSpec((heads, tm, dv), by_head), pl.BlockSpec((heads, tm, dv), by_head)],
        out_shape=[jax.ShapeDtypeStruct((heads, n, dk), out_dtype), jax.ShapeDtypeStruct((heads, n, dk), out_dtype),
                   jax.ShapeDtypeStruct((heads, n, dv), out_dtype), jax.ShapeDtypeStruct((heads, n, dv), out_dtype)],
        compiler_params=_params(("parallel",), vmem),
        name="ret_in",
    )(h, g, w_in, cos, sin, kscale)


def _ret_prompt_body(lg_ref, h_ref, g_ref, w_ref, cos_ref, sin_ref, kscale_ref, o_ref, s_out_ref,
                     q_s, kd_s, v_s, sg_s, s_ref, *, heads, dk, dv, chunk):
    t = pl.program_id(1)

    @pl.when(t == 0)
    def _():
        s_ref[...] = jnp.zeros_like(s_ref)

    _ret_in_body(h_ref, g_ref, w_ref, cos_ref, sin_ref, kscale_ref, q_s, kd_s, v_s, sg_s, heads=heads, dk=dk, dv=dv)

    tm = h_ref.shape[0]
    ri = lax.broadcasted_iota(jnp.int32, (chunk, chunk), 0)
    ci = lax.broadcasted_iota(jnp.int32, (chunk, chunk), 1)
    lower = (ri >= ci).astype(F32)
    row_v = lax.broadcasted_iota(jnp.int32, (chunk, dv), 0).astype(F32)
    causal, q_decay, chunk_decay = [], [], []
    for hh in range(heads):
        lg = lg_ref[hh]
        causal.append(lower * jnp.exp(jnp.full((1, chunk), -lg * chunk, F32)))
        q_decay.append(jnp.exp(lg * (row_v + 1.0)))
        chunk_decay.append(jnp.exp(jnp.full((1, dv), lg * chunk, F32)))

    for c in range(tm // chunk):
        rows = pl.ds(c * chunk, chunk)
        qk = [_dot_nt(q_s[hh, rows, :], kd_s[hh, rows, :]) for hh in range(heads)]
        grow = [_dot_tn(kd_s[hh, rows, :], v_s[hh, rows, :]) for hh in range(heads)]
        for hh in range(heads):
            s_prev = s_ref[hh]
            lhs = jnp.concatenate([(qk[hh] * causal[hh]).astype(BF16), q_s[hh, rows, :]], axis=1)
            rhs = jnp.concatenate([v_s[hh, rows, :], s_prev.astype(BF16)], axis=0)
            o = _rms_rows(q_decay[hh] * _dot(lhs, rhs))
            s_ref[hh] = chunk_decay[hh] * s_prev + grow[hh]
            o_ref[rows, hh * dv:(hh + 1) * dv] = (o * sg_s[hh, rows, :].astype(F32)).astype(o_ref.dtype)

    @pl.when(t == pl.num_programs(1) - 1)
    def _():
        s_out_ref[0] = s_ref[...]


def _ret_chunk(seq):
    return RET_KERNEL_CHUNK if seq % RET_KERNEL_CHUNK == 0 else seq


def _ret_prompt(log_g, h, g, w_in, cos, sin, kscale, *, batch, heads, dk, dv):
    n, d = h.shape
    seq = n // batch
    tm = min(TOKEN_TILE, seq)
    chunk = _ret_chunk(seq)
    assert tm % chunk == 0 and seq % tm == 0
    nt = seq // tm
    qk_w, v_w = heads * dk, heads * dv
    row = lambda b, t: (b * nt + t, 0)
    tab = lambda b, t: (t, 0)
    vmem = (2 * _nbytes((tm, d), F32) + _nbytes(w_in.shape, BF16) + 4 * _nbytes((tm, dk // 2), F32)
            + _nbytes((tm, qk_w), F32) + 2 * _nbytes((tm, v_w), BF16) + _nbytes((tm, 2 * qk_w + 2 * v_w), BF16)
            + 3 * _nbytes((heads, dk, dv), F32) + 4 * _nbytes((tm, COL_CHUNK), F32)
            + 2 * heads * (_nbytes((chunk, chunk), F32) + _nbytes((dk, dv), F32) + _nbytes((chunk, dv), F32)))
    return pl.pallas_call(
        functools.partial(_ret_prompt_body, heads=heads, dk=dk, dv=dv, chunk=chunk),
        grid=(batch, nt),
        in_specs=[pl.BlockSpec(memory_space=pltpu.SMEM),
                  pl.BlockSpec((tm, d), row), _resident((1, d)), _resident(w_in.shape),
                  pl.BlockSpec((tm, dk // 2), tab), pl.BlockSpec((tm, dk // 2), tab), _resident((tm, qk_w))],
        out_specs=[pl.BlockSpec((tm, v_w), row),
                   pl.BlockSpec((1, heads, dk, dv), lambda b, t: (b, 0, 0, 0))],
        out_shape=[jax.ShapeDtypeStruct((n, v_w), BF16),
                   jax.ShapeDtypeStruct((batch, heads, dk, dv), F32)],
        scratch_shapes=[pltpu.VMEM((heads, tm, dk), BF16), pltpu.VMEM((heads, tm, dk), BF16),
                        pltpu.VMEM((heads, tm, dv), BF16), pltpu.VMEM((heads, tm, dv), BF16),
                        pltpu.VMEM((heads, dk, dv), F32)],
        compiler_params=_params(("parallel", "arbitrary"), vmem),
        name="ret_prompt",
    )(log_g, h, g, w_in, cos, sin, kscale)


def _ret_sample_body(lg_ref, q_ref, kd_ref, v_ref, sg_ref, s_in_ref, o_ref, s_out_ref, *, seq):
    heads, rows, dk = q_ref.shape
    dv = v_ref.shape[2]
    group = rows // seq
    ri = lax.broadcasted_iota(jnp.int32, (rows, rows), 0)
    ci = lax.broadcasted_iota(jnp.int32, (rows, rows), 1)
    visible = ((ri // seq) == (ci // seq)) & (ri >= ci)
    row_v = lax.broadcasted_iota(jnp.int32, (rows, dv), 0)
    row_k = lax.broadcasted_iota(jnp.int32, (rows, dk), 0)
    for hh in range(heads):
        lg = lg_ref[hh]
        causal = jnp.where(visible, jnp.exp(jnp.full((rows, rows), -lg * seq, F32)), 0.0)
        q_decay = jnp.exp(lg * ((row_v % seq).astype(F32) + 1.0))
        chunk_decay = jnp.exp(jnp.full((1, dv), lg * seq, F32))
        q = q_ref[hh].astype(BF16)
        kd = kd_ref[hh]
        v = v_ref[hh].astype(BF16)
        o = _dot((_dot_nt(q, kd.astype(BF16)) * causal).astype(BF16), v)
        for g in range(group):
            s_prev = s_in_ref[g, hh]
            o = jnp.where((row_v // seq) == g, o + _dot(q, s_prev.astype(BF16)), o)
            kd_g = jnp.where((row_k // seq) == g, kd, 0.0).astype(BF16)
            s_out_ref[g, hh] = chunk_decay * s_prev + _dot_tn(kd_g, v)
        o = _rms_rows(q_decay * o)
        o_ref[hh] = (o * sg_ref[hh]).astype(o_ref.dtype)


def _ret_sample(log_g, q, kd, v, sg, state, *, seq):
    heads, n, dk = q.shape
    dv = v.shape[2]
    batch = n // seq
    group = SAMPLE_GROUP if batch % SAMPLE_GROUP == 0 else batch
    rows = group * seq
    by_head = lambda i: (0, i, 0)
    st = lambda i: (i, 0, 0, 0)
    vmem = (4 * _nbytes((group, heads, dk, dv), F32) + 8 * _nbytes((rows, heads * dv), F32)
            + 4 * _nbytes((dk, dv), F32))
    return pl.pallas_call(
        functools.partial(_ret_sample_body, seq=seq),
        grid=(batch // group,),
        in_specs=[pl.BlockSpec(memory_space=pltpu.SMEM),
                  pl.BlockSpec((heads, rows, dk), by_head), pl.BlockSpec((heads, rows, dk), by_head),
                  pl.BlockSpec((heads, rows, dv), by_head), pl.BlockSpec((heads, rows, dv), by_head),
                  pl.BlockSpec((group, heads, dk, dv), st)],
        out_specs=[pl.BlockSpec((heads, rows, dv), by_head), pl.BlockSpec((group, heads, dk, dv), st)],
        out_shape=[jax.ShapeDtypeStruct((heads, n, dv), F32),
                   jax.ShapeDtypeStruct((batch, heads, dk, dv), F32)],
        compiler_params=_params(("parallel",), vmem),
        name="ret_sample",
    )(log_g, q, kd, v, sg, state)


def _out_ffn_body(o_ref, h_ref, wo_ref, g_post_ref, g_pre_ref, w1_ref, w2_ref, g_ffn_ref, y_ref, *, o_layout):
    tm = h_ref.shape[0]
    parts = OUT_FFN_ROW_PARTS if tm % (OUT_FFN_ROW_PARTS * BF16_SUBLANES) == 0 else 1
    rp = tm // parts
    d_ff = w1_ref.shape[1]
    fc = min(COL_CHUNK, d_ff)

    def mixer_out(p):
        rows = slice(p * rp, (p + 1) * rp)
        if o_layout == "features":
            return _dot_tn(o_ref[0, :, rows], wo_ref[...])
        if o_layout == "heads":
            o = jnp.concatenate([o_ref[hh, rows, :] for hh in range(o_ref.shape[0])], axis=1)
            return _dot(o.astype(BF16), wo_ref[...])
        return _dot(o_ref[rows, :].astype(BF16), wo_ref[...])

    a = [mixer_out(p) for p in range(parts)]
    h1, x = [], []
    for p in range(parts):
        rows = slice(p * rp, (p + 1) * rp)
        h1.append(h_ref[rows, :] + _rms_rows(a[p]) * g_post_ref[...])
        x.append((_rms_rows(h1[p]) * g_pre_ref[...]).astype(BF16))
    acc = [jnp.zeros((rp, h_ref.shape[1]), F32) for _ in range(parts)]
    for c in range(d_ff // fc):
        for p in range(parts):
            u = jnp.maximum(_dot(x[p], w1_ref[:, c * fc:(c + 1) * fc]), 0.0)
            acc[p] = acc[p] + _dot((u * u).astype(BF16), w2_ref[c * fc:(c + 1) * fc, :])
    for p in range(parts):
        rows = slice(p * rp, (p + 1) * rp)
        y_ref[rows, :] = h1[p] + _rms_rows(acc[p]) * g_ffn_ref[...]


def _out_ffn(o, h, w_o, g_post, g_pre, w1, w2, g_ffn, *, o_layout):
    n, d = h.shape
    kdim = w_o.shape[0]
    tm = min(TOKEN_TILE, n)
    row = lambda i: (i, 0)
    if o_layout == "features":
        tiles = o.shape[2] // tm
        o_spec = pl.BlockSpec((1, kdim, tm), lambda i: (i // tiles, 0, i % tiles))
    elif o_layout == "heads":
        o_spec = pl.BlockSpec((o.shape[0], tm, o.shape[2]), lambda i: (0, i, 0))
    else:
        o_spec = pl.BlockSpec((tm, kdim), row)
    vmem = (2 * _nbytes((tm, kdim), o.dtype) + 4 * _nbytes((tm, d), F32) + _nbytes(w_o.shape, BF16)
            + _nbytes(w1.shape, BF16) + _nbytes(w2.shape, BF16) + 6 * _nbytes((tm, COL_CHUNK), F32))
    return pl.pallas_call(
        functools.partial(_out_ffn_body, o_layout=o_layout),
        grid=(n // tm,),
        in_specs=[o_spec, pl.BlockSpec((tm, d), row), _resident(w_o.shape),
                  _resident((1, d)), _resident((1, d)), _resident(w1.shape), _resident(w2.shape),
                  _resident((1, d))],
        out_specs=pl.BlockSpec((tm, d), row),
        out_shape=jax.ShapeDtypeStruct((n, d), F32),
        compiler_params=_params(("parallel",), vmem),
        name="out_ffn",
    )(o, h, w_o, g_post, g_pre, w1, w2, g_ffn)


def _partial_rope(x, c_tab, sa_tab, sb_tab):
    half = ROT_DIM // 2
    outs = []
    for j in range(x.shape[1] // LANES):
        s = x[:, j * LANES:(j + 1) * LANES]
        outs.append(s * c_tab + pltpu.roll(s, LANES - half, axis=1) * sa_tab + pltpu.roll(s, half, axis=1) * sb_tab)
    return outs


def _swa_in_body(h_ref, g_q_ref, g_kv_ref, wq_ref, wkv_ref, c_ref, sa_ref, sb_ref, q_ref, k_ref, v_ref):
    y = _rms_rows(h_ref[...])
    xq = (y * g_q_ref[...]).astype(BF16)
    xkv = (y * g_kv_ref[...]).astype(BF16)
    c_tab, sa_tab, sb_tab = c_ref[...], sa_ref[...], sb_ref[...]
    q = _dot(xq, wq_ref[...]) * (SWA_HEAD_DIM ** -0.5 * LOG2E)
    for j, s in enumerate(_partial_rope(q, c_tab, sa_tab, sb_tab)):
        q_ref[:, j * LANES:(j + 1) * LANES] = s.astype(q_ref.dtype)
    kv = _dot(xkv, wkv_ref[...])
    kw = k_ref.shape[1]
    for j, s in enumerate(_partial_rope(kv[:, :kw], c_tab, sa_tab, sb_tab)):
        k_ref[:, j * LANES:(j + 1) * LANES] = s
    v_ref[...] = kv[:, kw:]


def _swa_in(h, g_q, g_kv, w_q, w_kv, c_tab, sa_tab, sb_tab, *, q_dtype):
    n, d = h.shape
    tm = min(TOKEN_TILE, n)
    qw = w_q.shape[1]
    kw = w_kv.shape[1] // 2
    pos_tiles = c_tab.shape[0] // tm
    row = lambda i: (i, 0)
    tab = lambda i: (i % pos_tiles, 0)
    vmem = (2 * _nbytes((tm, d), F32) + _nbytes(w_q.shape, BF16) + _nbytes(w_kv.shape, BF16)
            + 6 * _nbytes((tm, LANES), F32) + 2 * _nbytes((tm, qw), q_dtype) + 4 * _nbytes((tm, kw), F32)
            + 6 * _nbytes((tm, qw), F32))
    return pl.pallas_call(
        _swa_in_body,
        grid=(n // tm,),
        in_specs=[pl.BlockSpec((tm, d), row), _resident((1, d)), _resident((1, d)),
                  _resident(w_q.shape), _resident(w_kv.shape),
                  pl.BlockSpec((tm, LANES), tab), pl.BlockSpec((tm, LANES), tab), pl.BlockSpec((tm, LANES), tab)],
        out_specs=[pl.BlockSpec((tm, qw), row), pl.BlockSpec((tm, kw), row), pl.BlockSpec((tm, kw), row)],
        out_shape=[jax.ShapeDtypeStruct((n, qw), q_dtype), jax.ShapeDtypeStruct((n, kw), F32),
                   jax.ShapeDtypeStruct((n, kw), F32)],
        compiler_params=_params(("parallel",), vmem),
        name="swa_in",
    )(h, g_q, g_kv, w_q, w_kv, c_tab, sa_tab, sb_tab)


def _swa_in_t_body(h_ref, g_q_ref, g_kv_ref, wqt_ref, wk_ref, wvt_ref, wv_ref, cos_t_ref, sin_t_ref,
                   c_ref, sa_ref, sb_ref, qt_ref, k_ref, vt_ref, kwin_ref, vwin_ref, *, tiles):
    y = _rms_rows(h_ref[...])
    xq = (y * g_q_ref[...]).astype(BF16)
    xkv = (y * g_kv_ref[...]).astype(BF16)
    tm = xq.shape[0]
    hd = SWA_HEAD_DIM
    half = ROT_DIM // 2
    cos_t, sin_t = cos_t_ref[...], sin_t_ref[...]
    qt = _dot_nt(wqt_ref[...], xq) * (hd ** -0.5 * LOG2E)
    for hq in range(qt.shape[0] // hd):
        base = hq * hd
        x1 = qt[base:base + half]
        x2 = qt[base + half:base + 2 * half]
        rot = jnp.concatenate([x1 * cos_t - x2 * sin_t, x2 * cos_t + x1 * sin_t], axis=0)
        qt_ref[0, base:base + 2 * half, :] = rot.astype(qt_ref.dtype)
        qt_ref[0, base + 2 * half:base + hd, :] = qt[base + 2 * half:base + hd].astype(qt_ref.dtype)
    k_rot = _partial_rope(_dot(xkv, wk_ref[...]), c_ref[...], sa_ref[...], sb_ref[...])
    for j, s in enumerate(k_rot):
        k_ref[:, j * LANES:(j + 1) * LANES] = s.astype(k_ref.dtype)
    vt_ref[0] = _dot_nt(wvt_ref[...], xkv).astype(vt_ref.dtype)
    win = kwin_ref.shape[0]

    @pl.when(pl.program_id(0) % tiles == tiles - 1)
    def _():
        for j, s in enumerate(k_rot):
            kwin_ref[:, j * LANES:(j + 1) * LANES] = s[tm - win:, :]
        vwin_ref[...] = _dot(xkv[tm - win:, :], wv_ref[...])


def _swa_in_t(h, g_q, g_kv, w_q, w_kv, cos_t, sin_t, c_tab, sa_tab, sb_tab, *, batch, win):
    n, d = h.shape
    seq = n // batch
    tm = min(TOKEN_TILE, seq)
    tiles = seq // tm
    qw = w_q.shape[1]
    kw = w_kv.shape[1] // 2
    wqt = w_q.T
    wk, wv = w_kv[:, :kw], w_kv[:, kw:]
    wvt = wv.T
    row = lambda i: (i, 0)
    tab = lambda i: (i % tiles, 0)
    tab_t = lambda i: (0, i % tiles)
    feat = lambda i: (i // tiles, 0, i % tiles)
    per_seq = lambda i: (i // tiles, 0)
    half = ROT_DIM // 2
    vmem = (2 * _nbytes((tm, d), F32) + 2 * _nbytes(w_q.shape, BF16) + 3 * _nbytes(w_kv.shape, BF16)
            + 8 * _nbytes((tm, LANES), F32) + 2 * _nbytes((tm, qw + 2 * kw), BF16) + 4 * _nbytes((win, kw), F32)
            + 4 * _nbytes((tm, qw), F32))
    return pl.pallas_call(
        functools.partial(_swa_in_t_body, tiles=tiles),
        grid=(n // tm,),
        in_specs=[pl.BlockSpec((tm, d), row), _resident((1, d)), _resident((1, d)),
                  _resident(wqt.shape), _resident(wk.shape), _resident(wvt.shape), _resident(wv.shape),
                  pl.BlockSpec((half, tm), tab_t), pl.BlockSpec((half, tm), tab_t),
                  pl.BlockSpec((tm, LANES), tab), pl.BlockSpec((tm, LANES), tab), pl.BlockSpec((tm, LANES), tab)],
        out_specs=[pl.BlockSpec((1, qw, tm), feat), pl.BlockSpec((tm, kw), row), pl.BlockSpec((1, kw, tm), feat),
                   pl.BlockSpec((win, kw), per_seq), pl.BlockSpec((win, kw), per_seq)],
        out_shape=[jax.ShapeDtypeStruct((batch, qw, seq), BF16), jax.ShapeDtypeStruct((n, kw), BF16),
                   jax.ShapeDtypeStruct((batch, kw, seq), BF16),
                   jax.ShapeDtypeStruct((batch * win, kw), F32), jax.ShapeDtypeStruct((batch * win, kw), F32)],
        compiler_params=_params(("arbitrary",), vmem),
        name="swa_in_t",
    )(h, g_q, g_kv, wqt, wk, wvt, wv, cos_t, sin_t, c_tab, sa_tab, sb_tab)


def _attn_prompt_body(sinks_ref, mask_ref, qt_ref, kp_ref, kc_ref, vtp_ref, vtc_ref, ot_ref, *, group):
    blk = kp_ref.shape[0]
    nblk = kc_ref.shape[0] // blk
    hd = SWA_HEAD_DIM
    kvh_n = kc_ref.shape[1] // hd
    cols = group * blk
    kj = lax.broadcasted_iota(jnp.int32, (blk, cols), 0)
    qi = lax.broadcasted_iota(jnp.int32, (blk, cols), 1) % blk
    own = kj <= qi
    lane_head = lax.broadcasted_iota(jnp.int32, (1, cols), 1) // blk
    ones_rows = jnp.ones((BF16_SUBLANES, blk), BF16)
    has_prev = pl.program_id(1) > 0

    def scores(j, kvh):
        tile, lo = divmod(kvh * hd, LANES)
        q4t = jnp.concatenate([qt_ref[0, (kvh * group + g) * hd:(kvh * group + g + 1) * hd, j * blk:(j + 1) * blk]
                               for g in range(group)], axis=1)
        rhs = jnp.concatenate([q4t if part * hd == lo else jnp.zeros_like(q4t) for part in range(LANES // hd)], axis=0)
        k_tile = slice(tile * LANES, (tile + 1) * LANES)
        k_prev = kp_ref[:, k_tile] if j == 0 else kc_ref[(j - 1) * blk:j * blk, k_tile]
        return _dot(kc_ref[j * blk:(j + 1) * blk, k_tile], rhs), _dot(k_prev, rhs)

    def finish(j, kvh, s_own, s_prev):
        if j == 0:
            s_prev = jnp.where(has_prev, s_prev, NEG)
        s = jnp.where(own, s_own, s_prev)
        sink = jnp.full((1, cols), sinks_ref[kvh * group] * LOG2E, F32)
        for g in range(1, group):
            sink = jnp.where(lane_head == g, sinks_ref[kvh * group + g] * LOG2E, sink)
        m = jnp.maximum(jnp.max(s, axis=0, keepdims=True), sink)
        e = jnp.exp2(s - m).astype(BF16)
        p_own = e * mask_ref[...]
        p = jnp.concatenate([p_own, e - p_own], axis=0)
        head_rows = slice(kvh * hd, (kvh + 1) * hd)
        vt_own = vtc_ref[0, head_rows, j * blk:(j + 1) * blk]
        vt_prev = vtp_ref[0, head_rows, :] if j == 0 else vtc_ref[0, head_rows, (j - 1) * blk:j * blk]
        vt = jnp.concatenate([jnp.concatenate([vt_own, ones_rows], axis=0),
                              jnp.concatenate([vt_prev, ones_rows], axis=0)], axis=1)
        acc = _dot(vt, p)
        denom = acc[hd:hd + 1, :] + jnp.exp2(sink - m)
        ot = acc[:hd] / denom
        for g in range(group):
            hq = kvh * group + g
            ot_ref[0, hq * hd:(hq + 1) * hd, j * blk:(j + 1) * blk] = ot[:, g * blk:(g + 1) * blk].astype(ot_ref.dtype)

    units = [(j, kvh) for j in range(nblk) for kvh in range(kvh_n)]
    pending = scores(*units[0])
    for idx, unit in enumerate(units):
        upcoming = scores(*units[idx + 1]) if idx + 1 < len(units) else None
        finish(*unit, *pending)
        pending = upcoming


def _attn_prompt(sinks, qt, k, vt):
    batch, qw, seq = qt.shape
    kw = k.shape[1]
    blk = WINDOW
    nblk = ATTN_BLOCKS_PER_STEP if seq % (ATTN_BLOCKS_PER_STEP * blk) == 0 else 1
    span = nblk * blk
    steps = seq // span
    group = qw // kw
    cols = group * blk
    own = (jnp.arange(blk)[:, None] <= (jnp.arange(cols) % blk)[None, :]).astype(BF16)
    cur_t = lambda b, i: (b, 0, i)
    prev_t = lambda b, i: (b, 0, jnp.maximum(i * nblk - 1, 0))
    cur = lambda b, i: (b * steps + i, 0)
    prev = lambda b, i: (b * steps * nblk + jnp.maximum(i * nblk - 1, 0), 0)
    vmem = (4 * _nbytes((qw, span), BF16) + 6 * _nbytes((span, kw), BF16) + 16 * _nbytes((blk, cols), F32))
    return pl.pallas_call(
        functools.partial(_attn_prompt_body, group=group),
        grid=(batch, steps),
        in_specs=[pl.BlockSpec(memory_space=pltpu.SMEM), _resident((blk, cols)), pl.BlockSpec((1, qw, span), cur_t),
                  pl.BlockSpec((blk, kw), prev), pl.BlockSpec((span, kw), cur),
                  pl.BlockSpec((1, kw, blk), prev_t), pl.BlockSpec((1, kw, span), cur_t)],
        out_specs=pl.BlockSpec((1, qw, span), cur_t),
        out_shape=jax.ShapeDtypeStruct((batch, qw, seq), BF16),
        compiler_params=_params(("parallel", "parallel"), vmem),
        name="attn_prompt",
    )(sinks, own, qt, k, k, vt, vt)


def _attn_sample_body(sink_ref, q_ref, kn_ref, vn_ref, kc_ref, vc_ref, o_ref, kw_ref, vw_ref, *, seq, group, q_start):
    rows = q_ref.shape[0]
    nseq = rows // seq
    win = kc_ref.shape[1]
    hd = SWA_HEAD_DIM
    kw = kn_ref.shape[1]
    heads = q_ref.shape[1] // kw
    srows = heads * rows
    lhs = jnp.concatenate([q_ref[:, hq * kw:(hq + 1) * kw] for hq in range(heads)], axis=0)
    kn = kn_ref[...]
    vn = vn_ref[...]
    sink = sink_ref[...]

    r_c = lax.broadcasted_iota(jnp.int32, (srows, win), 0) % rows
    c_c = lax.broadcasted_iota(jnp.int32, (srows, win), 1)
    seq_c = r_c // seq
    rel_c = (r_c % seq) + win - c_c
    ok_c = (rel_c >= 0) & (rel_c < WINDOW) & (q_start - win + c_c >= 0)
    s_c = _dot_nt(lhs, kc_ref[0].astype(BF16))
    for b in range(1, nseq):
        s_c = jnp.where(seq_c == b, _dot_nt(lhs, kc_ref[b].astype(BF16)), s_c)
    s_c = jnp.where(ok_c, s_c, NEG)

    r_n = lax.broadcasted_iota(jnp.int32, (srows, rows), 0) % rows
    c_n = lax.broadcasted_iota(jnp.int32, (srows, rows), 1)
    rel_n = (r_n % seq) - (c_n % seq)
    ok_n = (rel_n >= 0) & (rel_n < WINDOW) & ((r_n // seq) == (c_n // seq))
    s_n = jnp.where(ok_n, _dot_nt(lhs, kn.astype(BF16)), NEG)

    m = jnp.maximum(jnp.maximum(jnp.max(s_c, axis=-1, keepdims=True), jnp.max(s_n, axis=-1, keepdims=True)), sink)
    e_c = jnp.exp2(s_c - m)
    e_n = jnp.exp2(s_n - m)
    denom = jnp.sum(e_c, axis=-1, keepdims=True) + jnp.sum(e_n, axis=-1, keepdims=True) + jnp.exp2(sink - m)
    acc = _dot(e_n.astype(BF16), vn.astype(BF16))
    for b in range(nseq):
        acc = acc + _dot(jnp.where(seq_c == b, e_c, 0.0).astype(BF16), vc_ref[b].astype(BF16))
    o = acc / denom
    for hq in range(heads):
        kvh = hq // group
        o_ref[:, hq * hd:(hq + 1) * hd] = o[hq * rows:(hq + 1) * rows, kvh * hd:(kvh + 1) * hd].astype(o_ref.dtype)
    for b in range(nseq):
        kw_ref[b, 0:win - seq, :] = kc_ref[b, seq:win, :]
        kw_ref[b, win - seq:win, :] = kn[b * seq:(b + 1) * seq, :]
        vw_ref[b, 0:win - seq, :] = vc_ref[b, seq:win, :]
        vw_ref[b, win - seq:win, :] = vn[b * seq:(b + 1) * seq, :]


def _attn_sample(sinks, q_wide, k_new, v_new, k_cache, v_cache, *, seq, q_start):
    n, qww = q_wide.shape
    kw = k_new.shape[1]
    heads = qww // kw
    hd = SWA_HEAD_DIM
    group = heads // (kw // hd)
    batch, win, _ = k_cache.shape
    nseq = ATTN_SAMPLE_SEQS if batch % ATTN_SAMPLE_SEQS == 0 else batch
    rows = nseq * seq
    sink_rows = jnp.repeat(sinks * LOG2E, rows)[:, None]
    row = lambda i: (i, 0)
    cache = lambda i: (i, 0, 0)
    vmem = (8 * _nbytes((nseq, win, kw), F32) + 4 * _nbytes((rows, qww), BF16)
            + 16 * _nbytes((heads * rows, win + kw), F32))
    return pl.pallas_call(
        functools.partial(_attn_sample_body, seq=seq, group=group, q_start=q_start),
        grid=(batch // nseq,),
        in_specs=[_resident((heads * rows, 1)), pl.BlockSpec((rows, qww), row),
                  pl.BlockSpec((rows, kw), row), pl.BlockSpec((rows, kw), row),
                  pl.BlockSpec((nseq, win, kw), cache), pl.BlockSpec((nseq, win, kw), cache)],
        out_specs=[pl.BlockSpec((rows, heads * hd), row), pl.BlockSpec((nseq, win, kw), cache),
                   pl.BlockSpec((nseq, win, kw), cache)],
        out_shape=[jax.ShapeDtypeStruct((n, heads * hd), BF16), jax.ShapeDtypeStruct((batch, win, kw), F32),
                   jax.ShapeDtypeStruct((batch, win, kw), F32)],
        compiler_params=_params(("parallel",), vmem),
        name="attn_sample",
    )(sink_rows, q_wide, k_new, v_new, k_cache, v_cache)


def _ret_rope_tables(pos, dk):
    inv = 1.0 / (RET_ROPE_THETA ** jnp.linspace(0.0, 1.0, dk // 2, dtype=F32))
    ang = pos[:, None] * inv[None, :]
    return jnp.cos(ang), jnp.sin(ang)


def _ret_key_scale(log_g, seq, n, dk):
    chunk = _ret_chunk(seq)
    tm = min(TOKEN_TILE, n)
    assert tm % chunk == 0
    left = (chunk - 1 - jnp.arange(tm) % chunk).astype(F32)
    per_head = jnp.exp(log_g[None, :] * left[:, None]) * dk ** -0.5
    return jnp.repeat(per_head, dk, axis=1)


def _swa_cos_sin(pos):
    half = ROT_DIM // 2
    inv = ROPE_THETA ** (-jnp.arange(half, dtype=F32) / half)
    ang = pos[:, None] * inv[None, :]
    return jnp.cos(ang), jnp.sin(ang)


def _swa_rope_tables(pos):
    half = ROT_DIM // 2
    cos, sin = _swa_cos_sin(pos)
    n = pos.shape[0]
    pad = jnp.zeros((n, SWA_HEAD_DIM - 2 * half), F32)
    c_head = jnp.concatenate([cos, cos, pad + 1.0], axis=1)
    sa_head = jnp.concatenate([-sin, jnp.zeros_like(sin), pad], axis=1)
    sb_head = jnp.concatenate([jnp.zeros_like(sin), sin, pad], axis=1)
    reps = LANES // SWA_HEAD_DIM
    return tuple(jnp.tile(t, (1, reps)) for t in (c_head, sa_head, sb_head))


def _tile_rows(tab, seq, n):
    tm = min(TOKEN_TILE, n)
    return tab if seq >= tm else jnp.tile(tab, (tm // seq, 1))


def kernel(x_prompt, x_sample, state_ret, cache_k_win, cache_v_win, ret_norm_pre, ret_w_in, ret_w_out, ret_norm_post, kv_norm, w_kv, swa_norm_pre, swa_w_q, swa_sinks, swa_w_o, swa_norm_post, ffn_norm_pre, ffn_w1, ffn_w2, ffn_norm_post):
    n_a = DEPTH // 2
    assert n_a == 1 and DEPTH == 2, "one retention layer followed by one sliding-window layer"
    d = x_prompt.shape[-1]
    heads = RET_HEADS
    dk = ret_w_out.shape[-1] // heads
    dv = ret_w_out.shape[-2] // heads
    kvh, hd = SWA_KV_HEADS, SWA_HEAD_DIM
    row2 = lambda g: g.reshape(1, d)
    log_g = jnp.log1p(-jnp.exp2(-5.0 - jnp.arange(heads, dtype=F32)))

    w_in = ret_w_in[0].astype(BF16)
    w_out = ret_w_out[0].astype(BF16)
    wq = swa_w_q[0].astype(BF16)
    wkv = w_kv.astype(BF16)
    wo = swa_w_o[0].astype(BF16)
    w1 = ffn_w1.astype(BF16)
    w2 = ffn_w2.astype(BF16)
    sinks = swa_sinks[0]
    q_heads = wq.shape[1] // hd
    on_kv_head = (jnp.arange(q_heads)[:, None] // (q_heads // kvh) == jnp.arange(kvh)[None, :]).astype(BF16)
    wq_wide = (wq.reshape(d, q_heads, 1, hd) * on_kv_head[None, :, :, None]).reshape(d, q_heads * kvh * hd)

    def trunk(x, pos, ret_mixer, swa_mixer):
        b, t, _ = x.shape
        n = b * t
        h = x.reshape(n, d)
        cos, sin = (_tile_rows(tab, t, n) for tab in _ret_rope_tables(pos, dk))
        o, o_layout, state = ret_mixer(h, cos, sin, _ret_key_scale(log_g, t, n, dk))
        h = _out_ffn(o, h, w_out, row2(ret_norm_post[0]), row2(ffn_norm_pre[0]), w1[0], w2[0], row2(ffn_norm_post[0]),
                     o_layout=o_layout)
        o, o_layout, k_win, v_win = swa_mixer(h, pos, b, t)
        h = _out_ffn(o, h, wo, row2(swa_norm_post[0]), row2(ffn_norm_pre[1]), w1[1], w2[1], row2(ffn_norm_post[1]),
                     o_layout=o_layout)
        return h.reshape(b, t, d), state, k_win, v_win

    b_p, t_p, _ = x_prompt.shape
    w_p = min(WINDOW, t_p)

    def swa_prompt(h, pos, b, t):
        cos, sin = _swa_cos_sin(pos)
        qt, k, vt, k_win, v_win = _swa_in_t(h, row2(swa_norm_pre[0]), row2(kv_norm), wq, wkv, cos.T, sin.T,
                                            *_swa_rope_tables(pos), batch=b, win=w_p)
        return _attn_prompt(sinks, qt, k, vt), "features", k_win, v_win

    def ret_prompt(h, cos, sin, kscale):
        o, state = _ret_prompt(log_g, h, row2(ret_norm_pre[0]), w_in, cos, sin, kscale,
                               batch=b_p, heads=heads, dk=dk, dv=dv)
        return o, "tokens", state

    y_prompt, state_p, k_win_p, v_win_p = trunk(x_prompt, jnp.arange(t_p, dtype=F32), ret_prompt, swa_prompt)

    b_s, t_s, _ = x_sample.shape
    w_s = cache_k_win.shape[1]
    kc = cache_k_win.reshape(b_s, w_s, kvh * hd)
    vc = cache_v_win.reshape(b_s, w_s, kvh * hd)

    def swa_sample(h, pos, b, t):
        tabs = tuple(_tile_rows(tab, t, b * t) for tab in _swa_rope_tables(pos))
        q, k, v = _swa_in(h, row2(swa_norm_pre[0]), row2(kv_norm), wq_wide, wkv, *tabs, q_dtype=BF16)
        o, k_win, v_win = _attn_sample(sinks, q, k, v, kc, vc, seq=t, q_start=PAST_LEN)
        return o, "tokens", k_win, v_win

    def ret_sample(h, cos, sin, kscale):
        q, kd, v, sg = _ret_in(h, row2(ret_norm_pre[0]), w_in, cos, sin, kscale,
                               heads=heads, dk=dk, dv=dv, out_dtype=F32)
        o, state = _ret_sample(log_g, q, kd, v, sg, state_ret[0], seq=t_s)
        return o, "heads", state

    y_sample, state_s, k_win_s, v_win_s = trunk(x_sample, PAST_LEN + jnp.arange(t_s, dtype=F32), ret_sample, swa_sample)

    return (y_prompt, y_sample, state_p[None], state_s[None],
            k_win_p.reshape(b_p, w_p, kvh, hd), v_win_p.reshape(b_p, w_p, kvh, hd),
            k_win_s.reshape(b_s, w_s, kvh, hd), v_win_s.reshape(b_s, w_s, kvh, hd))
```

```python
import functools

import jax
import jax.numpy as jnp
from jax import lax
from jax.experimental import pallas as pl
from jax.experimental.pallas import tpu as pltpu

DEPTH = 2
PAST_LEN = 16384
RET_HEADS = 4
RET_ROPE_THETA = 10000.0
SWA_HEAD_DIM = 64
SWA_KV_HEADS = 4
WINDOW = 128
ROPE_THETA = 500000.0
ROT_DIM = SWA_HEAD_DIM // 4
EPS = 1e-6
NEG = -1e30
LOG2E = 1.4426950408889634

LANES = 128
SUBLANES = 8
BF16_SUBLANES = 16
VMEM_CAP_BYTES = 64 * 1024 * 1024
VMEM_BUDGET_BYTES = VMEM_CAP_BYTES - 8 * 1024 * 1024

TOKEN_TILE = 512
COL_CHUNK = 1024
RET_KERNEL_CHUNK = 256
SAMPLE_GROUP = 2
ATTN_SAMPLE_SEQS = 4
ATTN_BLOCKS_PER_STEP = 4
OUT_FFN_ROW_PARTS = 2

F32 = jnp.float32
BF16 = jnp.bfloat16


def _params(semantics, vmem_bytes):
    limit = int(min(max(vmem_bytes, 16 * 1024 * 1024), VMEM_BUDGET_BYTES))
    return pltpu.CompilerParams(dimension_semantics=semantics, vmem_limit_bytes=limit)


def _resident(shape):
    nd = len(shape)
    return pl.BlockSpec(shape, lambda *_: (0,) * nd, pipeline_mode=pl.Buffered(1))


def _nbytes(shape, dtype):
    n = 1
    for s in shape:
        n *= s
    return n * jnp.dtype(dtype).itemsize


def _rms_rows(x):
    return x * lax.rsqrt(jnp.mean(x * x, axis=-1, keepdims=True) + EPS)


def _dot(a, b):
    return jnp.dot(a, b, preferred_element_type=F32)


def _dot_nt(a, b):
    return lax.dot_general(a, b, (((1,), (1,)), ((), ())), preferred_element_type=F32)


def _dot_tn(a, b):
    return lax.dot_general(a, b, (((0,), (0,)), ((), ())), preferred_element_type=F32)


def _ret_in_body(h_ref, g_ref, w_ref, cos_ref, sin_ref, kscale_ref, q_ref, kd_ref, v_ref, sg_ref, *, heads, dk, dv,
                 rows=slice(None)):
    xn = (_rms_rows(h_ref[rows, :]) * g_ref[...]).astype(BF16)
    cos = cos_ref[rows, :]
    sin = sin_ref[rows, :]
    half = dk // 2
    qk_w = heads * dk
    v_w = heads * dv

    def proj(lo, width):
        return _dot(xn, w_ref[:, lo:lo + width])

    for base, ref, scale_ref in ((0, q_ref, None), (qk_w, kd_ref, kscale_ref)):
        p = proj(base, qk_w)
        for hh in range(heads):
            lo, mid, hi = hh * dk, hh * dk + half, (hh + 1) * dk
            x1 = p[:, lo:mid]
            x2 = p[:, mid:hi]
            o1 = x1 * cos - x2 * sin
            o2 = x2 * cos + x1 * sin
            if scale_ref is not None:
                o1 = o1 * scale_ref[rows, lo:mid]
                o2 = o2 * scale_ref[rows, mid:hi]
            ref[hh, rows, :half] = o1.astype(ref.dtype)
            ref[hh, rows, half:] = o2.astype(ref.dtype)
    cw = min(COL_CHUNK, v_w)
    per_chunk = cw // dv
    for c in range(v_w // cw):
        v = proj(2 * qk_w + c * cw, cw)
        for j in range(per_chunk):
            v_ref[c * per_chunk + j, rows, :] = v[:, j * dv:(j + 1) * dv].astype(v_ref.dtype)
    for c in range(v_w // cw):
        gate = proj(2 * qk_w + v_w + c * cw, cw)
        sg = gate * jax.nn.sigmoid(gate)
        for j in range(per_chunk):
            sg_ref[c * per_chunk + j, rows, :] = sg[:, j * dv:(j + 1) * dv].astype(sg_ref.dtype)


def _ret_in(h, g, w_in, cos, sin, kscale, *, heads, dk, dv, out_dtype):
    n, d = h.shape
    tm = min(TOKEN_TILE, n)
    qk_w, v_w = heads * dk, heads * dv
    pos_tiles = cos.shape[0] // tm
    row = lambda i: (i, 0)
    tab = lambda i: (i % pos_tiles, 0)
    by_head = lambda i: (0, i, 0)
    vmem = (2 * _nbytes((tm, d), F32) + _nbytes(w_in.shape, BF16) + 4 * _nbytes((tm, dk // 2), F32)
            + _nbytes((tm, qk_w), F32) + 2 * _nbytes((tm, 2 * qk_w + 2 * v_w), out_dtype)
            + 4 * _nbytes((tm, COL_CHUNK), F32))
    return pl.pallas_call(
        functools.partial(_ret_in_body, heads=heads, dk=dk, dv=dv),
        grid=(n // tm,),
        in_specs=[pl.BlockSpec((tm, d), row), _resident((1, d)), _resident(w_in.shape),
                  pl.BlockSpec((tm, dk // 2), tab), pl.BlockSpec((tm, dk // 2), tab), _resident((tm, qk_w))],
        out_specs=[pl.BlockSpec((heads, tm, dk), by_head), pl.BlockSpec((heads, tm, dk), by_head),
                   pl.BlockSpec((heads, tm, dv), by_head), pl.BlockSpec((heads, tm, dv), by_head)],
        out_shape=[jax.ShapeDtypeStruct((heads, n, dk), out_dtype), jax.ShapeDtypeStruct((heads, n, dk), out_dtype),
                   jax.ShapeDtypeStruct((heads, n, dv), out_dtype), jax.ShapeDtypeStruct((heads, n, dv), out_dtype)],
        compiler_params=_params(("parallel",), vmem),
        name="ret_in",
    )(h, g, w_in, cos, sin, kscale)


def _ret_prompt_body(lg_ref, h_ref, g_ref, w_ref, cos_ref, sin_ref, kscale_ref, o_ref, s_out_ref,
                     q_s, kd_s, v_s, sg_s, s_ref, *, heads, dk, dv, chunk):
    t = pl.program_id(1)

    @pl.when(t == 0)
    def _():
        s_ref[...] = jnp.zeros_like(s_ref)

    tm = h_ref.shape[0]
    for c in range(tm // chunk):
        _ret_in_body(h_ref, g_ref, w_ref, cos_ref, sin_ref, kscale_ref, q_s, kd_s, v_s, sg_s,
                     heads=heads, dk=dk, dv=dv, rows=pl.ds(c * chunk, chunk))

    ri = lax.broadcasted_iota(jnp.int32, (chunk, chunk), 0)
    ci = lax.broadcasted_iota(jnp.int32, (chunk, chunk), 1)
    lower = (ri >= ci).astype(F32)
    row_v = lax.broadcasted_iota(jnp.int32, (chunk, dv), 0).astype(F32)
    causal, q_decay, chunk_decay = [], [], []
    for hh in range(heads):
        lg = lg_ref[hh]
        causal.append(lower * jnp.exp(jnp.full((1, chunk), -lg * chunk, F32)))
        q_decay.append(jnp.exp(lg * (row_v + 1.0)))
        chunk_decay.append(jnp.exp(jnp.full((1, dv), lg * chunk, F32)))

    for c in range(tm // chunk):
        rows = pl.ds(c * chunk, chunk)
        qk = [_dot_nt(q_s[hh, rows, :], kd_s[hh, rows, :]) for hh in range(heads)]
        grow = [_dot_tn(kd_s[hh, rows, :], v_s[hh, rows, :]) for hh in range(heads)]
        for hh in range(heads):
            s_prev = s_ref[hh]
            lhs = jnp.concatenate([(qk[hh] * causal[hh]).astype(BF16), q_s[hh, rows, :]], axis=1)
            rhs = jnp.concatenate([v_s[hh, rows, :], s_prev.astype(BF16)], axis=0)
            o = _rms_rows(q_decay[hh] * _dot(lhs, rhs))
            s_ref[hh] = chunk_decay[hh] * s_prev + grow[hh]
            o_ref[rows, hh * dv:(hh + 1) * dv] = (o * sg_s[hh, rows, :].astype(F32)).astype(o_ref.dtype)

    @pl.when(t == pl.num_programs(1) - 1)
    def _():
        s_out_ref[0] = s_ref[...]


def _ret_chunk(seq):
    return RET_KERNEL_CHUNK if seq % RET_KERNEL_CHUNK == 0 else seq


def _ret_prompt(log_g, h, g, w_in, cos, sin, kscale, *, batch, heads, dk, dv):
    n, d = h.shape
    seq = n // batch
    tm = min(TOKEN_TILE, seq)
    chunk = _ret_chunk(seq)
    assert tm % chunk == 0 and seq % tm == 0
    nt = seq // tm
    qk_w, v_w = heads * dk, heads * dv
    row = lambda b, t: (b * nt + t, 0)
    tab = lambda b, t: (t, 0)
    vmem = (2 * _nbytes((tm, d), F32) + _nbytes(w_in.shape, BF16) + 4 * _nbytes((tm, dk // 2), F32)
            + _nbytes((tm, qk_w), F32) + 2 * _nbytes((tm, v_w), BF16) + _nbytes((tm, 2 * qk_w + 2 * v_w), BF16)
            + 3 * _nbytes((heads, dk, dv), F32) + 4 * _nbytes((tm, COL_CHUNK), F32)
            + 2 * heads * (_nbytes((chunk, chunk), F32) + _nbytes((dk, dv), F32) + _nbytes((chunk, dv), F32)))
    return pl.pallas_call(
        functools.partial(_ret_prompt_body, heads=heads, dk=dk, dv=dv, chunk=chunk),
        grid=(batch, nt),
        in_specs=[pl.BlockSpec(memory_space=pltpu.SMEM),
                  pl.BlockSpec((tm, d), row), _resident((1, d)), _resident(w_in.shape),
                  pl.BlockSpec((tm, dk // 2), tab), pl.BlockSpec((tm, dk // 2), tab), _resident((tm, qk_w))],
        out_specs=[pl.BlockSpec((tm, v_w), row),
                   pl.BlockSpec((1, heads, dk, dv), lambda b, t: (b, 0, 0, 0))],
        out_shape=[jax.ShapeDtypeStruct((n, v_w), BF16),
                   jax.ShapeDtypeStruct((batch, heads, dk, dv), F32)],
        scratch_shapes=[pltpu.VMEM((heads, tm, dk), BF16), pltpu.VMEM((heads, tm, dk), BF16),
                        pltpu.VMEM((heads, tm, dv), BF16), pltpu.VMEM((heads, tm, dv), BF16),
                        pltpu.VMEM((heads, dk, dv), F32)],
        compiler_params=_params(("parallel", "arbitrary"), vmem),
        name="ret_prompt",
    )(log_g, h, g, w_in, cos, sin, kscale)


def _ret_sample_body(lg_ref, q_ref, kd_ref, v_ref, sg_ref, s_in_ref, o_ref, s_out_ref, *, seq):
    heads, rows, dk = q_ref.shape
    dv = v_ref.shape[2]
    group = rows // seq
    ri = lax.broadcasted_iota(jnp.int32, (rows, rows), 0)
    ci = lax.broadcasted_iota(jnp.int32, (rows, rows), 1)
    visible = ((ri // seq) == (ci // seq)) & (ri >= ci)
    row_v = lax.broadcasted_iota(jnp.int32, (rows, dv), 0)
    row_k = lax.broadcasted_iota(jnp.int32, (rows, dk), 0)
    for hh in range(heads):
        lg = lg_ref[hh]
        causal = jnp.where(visible, jnp.exp(jnp.full((rows, rows), -lg * seq, F32)), 0.0)
        q_decay = jnp.exp(lg * ((row_v % seq).astype(F32) + 1.0))
        chunk_decay = jnp.exp(jnp.full((1, dv), lg * seq, F32))
        q = q_ref[hh].astype(BF16)
        kd = kd_ref[hh]
        v = v_ref[hh].astype(BF16)
        o = _dot((_dot_nt(q, kd.astype(BF16)) * causal).astype(BF16), v)
        for g in range(group):
            s_prev = s_in_ref[g, hh]
            o = jnp.where((row_v // seq) == g, o + _dot(q, s_prev.astype(BF16)), o)
            kd_g = jnp.where((row_k // seq) == g, kd, 0.0).astype(BF16)
            s_out_ref[g, hh] = chunk_decay * s_prev + _dot_tn(kd_g, v)
        o = _rms_rows(q_decay * o)
        o_ref[hh] = (o * sg_ref[hh]).astype(o_ref.dtype)


def _ret_sample(log_g, q, kd, v, sg, state, *, seq):
    heads, n, dk = q.shape
    dv = v.shape[2]
    batch = n // seq
    group = SAMPLE_GROUP if batch % SAMPLE_GROUP == 0 else batch
    rows = group * seq
    by_head = lambda i: (0, i, 0)
    st = lambda i: (i, 0, 0, 0)
    vmem = (4 * _nbytes((group, heads, dk, dv), F32) + 8 * _nbytes((rows, heads * dv), F32)
            + 4 * _nbytes((dk, dv), F32))
    return pl.pallas_call(
        functools.partial(_ret_sample_body, seq=seq),
        grid=(batch // group,),
        in_specs=[pl.BlockSpec(memory_space=pltpu.SMEM),
                  pl.BlockSpec((heads, rows, dk), by_head), pl.BlockSpec((heads, rows, dk), by_head),
                  pl.BlockSpec((heads, rows, dv), by_head), pl.BlockSpec((heads, rows, dv), by_head),
                  pl.BlockSpec((group, heads, dk, dv), st)],
        out_specs=[pl.BlockSpec((heads, rows, dv), by_head), pl.BlockSpec((group, heads, dk, dv), st)],
        out_shape=[jax.ShapeDtypeStruct((heads, n, dv), F32),
                   jax.ShapeDtypeStruct((batch, heads, dk, dv), F32)],
        compiler_params=_params(("parallel",), vmem),
        name="ret_sample",
    )(log_g, q, kd, v, sg, state)


def _out_ffn_body(o_ref, h_ref, wo_ref, g_post_ref, g_pre_ref, w1_ref, w2_ref, g_ffn_ref, y_ref, *, o_layout):
    tm = h_ref.shape[0]
    parts = OUT_FFN_ROW_PARTS if tm % (OUT_FFN_ROW_PARTS * BF16_SUBLANES) == 0 else 1
    rp = tm // parts
    d_ff = w1_ref.shape[1]
    fc = min(COL_CHUNK, d_ff)

    def mixer_out(p):
        rows = slice(p * rp, (p + 1) * rp)
        if o_layout == "features":
            return _dot_tn(o_ref[0, :, rows], wo_ref[...])
        if o_layout == "heads":
            o = jnp.concatenate([o_ref[hh, rows, :] for hh in range(o_ref.shape[0])], axis=1)
            return _dot(o.astype(BF16), wo_ref[...])
        return _dot(o_ref[rows, :].astype(BF16), wo_ref[...])

    a = [mixer_out(p) for p in range(parts)]
    h1, x = [], []
    for p in range(parts):
        rows = slice(p * rp, (p + 1) * rp)
        h1.append(h_ref[rows, :] + _rms_rows(a[p]) * g_post_ref[...])
        x.append((_rms_rows(h1[p]) * g_pre_ref[...]).astype(BF16))
    acc = [jnp.zeros((rp, h_ref.shape[1]), F32) for _ in range(parts)]
    for c in range(d_ff // fc):
        for p in range(parts):
            u = jnp.maximum(_dot(x[p], w1_ref[:, c * fc:(c + 1) * fc]), 0.0)
            acc[p] = acc[p] + _dot((u * u).astype(BF16), w2_ref[c * fc:(c + 1) * fc, :])
    for p in range(parts):
        rows = slice(p * rp, (p + 1) * rp)
        y_ref[rows, :] = h1[p] + _rms_rows(acc[p]) * g_ffn_ref[...]


def _out_ffn(o, h, w_o, g_post, g_pre, w1, w2, g_ffn, *, o_layout):
    n, d = h.shape
    kdim = w_o.shape[0]
    tm = min(TOKEN_TILE, n)
    row = lambda i: (i, 0)
    if o_layout == "features":
        tiles = o.shape[2] // tm
        o_spec = pl.BlockSpec((1, kdim, tm), lambda i: (i // tiles, 0, i % tiles))
    elif o_layout == "heads":
        o_spec = pl.BlockSpec((o.shape[0], tm, o.shape[2]), lambda i: (0, i, 0))
    else:
        o_spec = pl.BlockSpec((tm, kdim), row)
    vmem = (2 * _nbytes((tm, kdim), o.dtype) + 4 * _nbytes((tm, d), F32) + _nbytes(w_o.shape, BF16)
            + _nbytes(w1.shape, BF16) + _nbytes(w2.shape, BF16) + 6 * _nbytes((tm, COL_CHUNK), F32))
    return pl.pallas_call(
        functools.partial(_out_ffn_body, o_layout=o_layout),
        grid=(n // tm,),
        in_specs=[o_spec, pl.BlockSpec((tm, d), row), _resident(w_o.shape),
                  _resident((1, d)), _resident((1, d)), _resident(w1.shape), _resident(w2.shape),
                  _resident((1, d))],
        out_specs=pl.BlockSpec((tm, d), row),
        out_shape=jax.ShapeDtypeStruct((n, d), F32),
        compiler_params=_params(("parallel",), vmem),
        name="out_ffn",
    )(o, h, w_o, g_post, g_pre, w1, w2, g_ffn)


def _partial_rope(x, c_tab, sa_tab, sb_tab):
    half = ROT_DIM // 2
    outs = []
    for j in range(x.shape[1] // LANES):
        s = x[:, j * LANES:(j + 1) * LANES]
        outs.append(s * c_tab + pltpu.roll(s, LANES - half, axis=1) * sa_tab + pltpu.roll(s, half, axis=1) * sb_tab)
    return outs


def _swa_in_body(h_ref, g_q_ref, g_kv_ref, wq_ref, wkv_ref, c_ref, sa_ref, sb_ref, q_ref, k_ref, v_ref):
    y = _rms_rows(h_ref[...])
    xq = (y * g_q_ref[...]).astype(BF16)
    xkv = (y * g_kv_ref[...]).astype(BF16)
    c_tab, sa_tab, sb_tab = c_ref[...], sa_ref[...], sb_ref[...]
    q = _dot(xq, wq_ref[...]) * (SWA_HEAD_DIM ** -0.5 * LOG2E)
    for j, s in enumerate(_partial_rope(q, c_tab, sa_tab, sb_tab)):
        q_ref[:, j * LANES:(j + 1) * LANES] = s.astype(q_ref.dtype)
    kv = _dot(xkv, wkv_ref[...])
    kw = k_ref.shape[1]
    for j, s in enumerate(_partial_rope(kv[:, :kw], c_tab, sa_tab, sb_tab)):
        k_ref[:, j * LANES:(j + 1) * LANES] = s
    v_ref[...] = kv[:, kw:]


def _swa_in(h, g_q, g_kv, w_q, w_kv, c_tab, sa_tab, sb_tab, *, q_dtype):
    n, d = h.shape
    tm = min(TOKEN_TILE, n)
    qw = w_q.shape[1]
    kw = w_kv.shape[1] // 2
    pos_tiles = c_tab.shape[0] // tm
    row = lambda i: (i, 0)
    tab = lambda i: (i % pos_tiles, 0)
    vmem = (2 * _nbytes((tm, d), F32) + _nbytes(w_q.shape, BF16) + _nbytes(w_kv.shape, BF16)
            + 6 * _nbytes((tm, LANES), F32) + 2 * _nbytes((tm, qw), q_dtype) + 4 * _nbytes((tm, kw), F32)
            + 6 * _nbytes((tm, qw), F32))
    return pl.pallas_call(
        _swa_in_body,
        grid=(n // tm,),
        in_specs=[pl.BlockSpec((tm, d), row), _resident((1, d)), _resident((1, d)),
                  _resident(w_q.shape), _resident(w_kv.shape),
                  pl.BlockSpec((tm, LANES), tab), pl.BlockSpec((tm, LANES), tab), pl.BlockSpec((tm, LANES), tab)],
        out_specs=[pl.BlockSpec((tm, qw), row), pl.BlockSpec((tm, kw), row), pl.BlockSpec((tm, kw), row)],
        out_shape=[jax.ShapeDtypeStruct((n, qw), q_dtype), jax.ShapeDtypeStruct((n, kw), F32),
                   jax.ShapeDtypeStruct((n, kw), F32)],
        compiler_params=_params(("parallel",), vmem),
        name="swa_in",
    )(h, g_q, g_kv, w_q, w_kv, c_tab, sa_tab, sb_tab)


def _swa_in_t_body(h_ref, g_q_ref, g_kv_ref, wqt_ref, wk_ref, wvt_ref, wv_ref, cos_t_ref, sin_t_ref,
                   c_ref, sa_ref, sb_ref, qt_ref, k_ref, vt_ref, kwin_ref, vwin_ref, *, tiles):
    y = _rms_rows(h_ref[...])
    xq = (y * g_q_ref[...]).astype(BF16)
    xkv = (y * g_kv_ref[...]).astype(BF16)
    tm = xq.shape[0]
    hd = SWA_HEAD_DIM
    half = ROT_DIM // 2
    cos_t, sin_t = cos_t_ref[...], sin_t_ref[...]
    qt = _dot_nt(wqt_ref[...], xq) * (hd ** -0.5 * LOG2E)
    for hq in range(qt.shape[0] // hd):
        base = hq * hd
        x1 = qt[base:base + half]
        x2 = qt[base + half:base + 2 * half]
        rot = jnp.concatenate([x1 * cos_t - x2 * sin_t, x2 * cos_t + x1 * sin_t], axis=0)
        qt_ref[0, base:base + 2 * half, :] = rot.astype(qt_ref.dtype)
        qt_ref[0, base + 2 * half:base + hd, :] = qt[base + 2 * half:base + hd].astype(qt_ref.dtype)
    k_rot = _partial_rope(_dot(xkv, wk_ref[...]), c_ref[...], sa_ref[...], sb_ref[...])
    for j, s in enumerate(k_rot):
        k_ref[:, j * LANES:(j + 1) * LANES] = s.astype(k_ref.dtype)
    vt_ref[0] = _dot_nt(wvt_ref[...], xkv).astype(vt_ref.dtype)
    win = kwin_ref.shape[0]

    @pl.when(pl.program_id(0) % tiles == tiles - 1)
    def _():
        for j, s in enumerate(k_rot):
            kwin_ref[:, j * LANES:(j + 1) * LANES] = s[tm - win:, :]
        vwin_ref[...] = _dot(xkv[tm - win:, :], wv_ref[...])


def _swa_in_t(h, g_q, g_kv, w_q, w_kv, cos_t, sin_t, c_tab, sa_tab, sb_tab, *, batch, win):
    n, d = h.shape
    seq = n // batch
    tm = min(TOKEN_TILE, seq)
    tiles = seq // tm
    qw = w_q.shape[1]
    kw = w_kv.shape[1] // 2
    wqt = w_q.T
    wk, wv = w_kv[:, :kw], w_kv[:, kw:]
    wvt = wv.T
    row = lambda i: (i, 0)
    tab = lambda i: (i % tiles, 0)
    tab_t = lambda i: (0, i % tiles)
    feat = lambda i: (i // tiles, 0, i % tiles)
    per_seq = lambda i: (i // tiles, 0)
    half = ROT_DIM // 2
    vmem = (2 * _nbytes((tm, d), F32) + 2 * _nbytes(w_q.shape, BF16) + 3 * _nbytes(w_kv.shape, BF16)
            + 8 * _nbytes((tm, LANES), F32) + 2 * _nbytes((tm, qw + 2 * kw), BF16) + 4 * _nbytes((win, kw), F32)
            + 4 * _nbytes((tm, qw), F32))
    return pl.pallas_call(
        functools.partial(_swa_in_t_body, tiles=tiles),
        grid=(n // tm,),
        in_specs=[pl.BlockSpec((tm, d), row), _resident((1, d)), _resident((1, d)),
                  _resident(wqt.shape), _resident(wk.shape), _resident(wvt.shape), _resident(wv.shape),
                  pl.BlockSpec((half, tm), tab_t), pl.BlockSpec((half, tm), tab_t),
                  pl.BlockSpec((tm, LANES), tab), pl.BlockSpec((tm, LANES), tab), pl.BlockSpec((tm, LANES), tab)],
        out_specs=[pl.BlockSpec((1, qw, tm), feat), pl.BlockSpec((tm, kw), row), pl.BlockSpec((1, kw, tm), feat),
                   pl.BlockSpec((win, kw), per_seq), pl.BlockSpec((win, kw), per_seq)],
        out_shape=[jax.ShapeDtypeStruct((batch, qw, seq), BF16), jax.ShapeDtypeStruct((n, kw), BF16),
                   jax.ShapeDtypeStruct((batch, kw, seq), BF16),
                   jax.ShapeDtypeStruct((batch * win, kw), F32), jax.ShapeDtypeStruct((batch * win, kw), F32)],
        compiler_params=_params(("arbitrary",), vmem),
        name="swa_in_t",
    )(h, g_q, g_kv, wqt, wk, wvt, wv, cos_t, sin_t, c_tab, sa_tab, sb_tab)


def _attn_prompt_body(sinks_ref, mask_ref, qt_ref, kp_ref, kc_ref, vtp_ref, vtc_ref, ot_ref, *, group):
    blk = kp_ref.shape[0]
    nblk = kc_ref.shape[0] // blk
    hd = SWA_HEAD_DIM
    kvh_n = kc_ref.shape[1] // hd
    cols = group * blk
    kj = lax.broadcasted_iota(jnp.int32, (blk, cols), 0)
    qi = lax.broadcasted_iota(jnp.int32, (blk, cols), 1) % blk
    own = kj <= qi
    lane_head = lax.broadcasted_iota(jnp.int32, (1, cols), 1) // blk
    ones_rows = jnp.ones((BF16_SUBLANES, blk), BF16)
    has_prev = pl.program_id(1) > 0

    def scores(j, kvh):
        tile, lo = divmod(kvh * hd, LANES)
        q4t = jnp.concatenate([qt_ref[0, (kvh * group + g) * hd:(kvh * group + g + 1) * hd, j * blk:(j + 1) * blk]
                               for g in range(group)], axis=1)
        rhs = jnp.concatenate([q4t if part * hd == lo else jnp.zeros_like(q4t) for part in range(LANES // hd)], axis=0)
        k_tile = slice(tile * LANES, (tile + 1) * LANES)
        k_prev = kp_ref[:, k_tile] if j == 0 else kc_ref[(j - 1) * blk:j * blk, k_tile]
        return _dot(kc_ref[j * blk:(j + 1) * blk, k_tile], rhs), _dot(k_prev, rhs)

    def finish(j, kvh, s_own, s_prev):
        if j == 0:
            s_prev = jnp.where(has_prev, s_prev, NEG)
        s = jnp.where(own, s_own, s_prev)
        sink = jnp.full((1, cols), sinks_ref[kvh * group] * LOG2E, F32)
        for g in range(1, group):
            sink = jnp.where(lane_head == g, sinks_ref[kvh * group + g] * LOG2E, sink)
        m = jnp.maximum(jnp.max(s, axis=0, keepdims=True), sink)
        e = jnp.exp2(s - m).astype(BF16)
        p_own = e * mask_ref[...]
        p = jnp.concatenate([p_own, e - p_own], axis=0)
        head_rows = slice(kvh * hd, (kvh + 1) * hd)
        vt_own = vtc_ref[0, head_rows, j * blk:(j + 1) * blk]
        vt_prev = vtp_ref[0, head_rows, :] if j == 0 else vtc_ref[0, head_rows, (j - 1) * blk:j * blk]
        vt = jnp.concatenate([jnp.concatenate([vt_own, ones_rows], axis=0),
                              jnp.concatenate([vt_prev, ones_rows], axis=0)], axis=1)
        acc = _dot(vt, p)
        denom = acc[hd:hd + 1, :] + jnp.exp2(sink - m)
        ot = acc[:hd] / denom
        for g in range(group):
            hq = kvh * group + g
            ot_ref[0, hq * hd:(hq + 1) * hd, j * blk:(j + 1) * blk] = ot[:, g * blk:(g + 1) * blk].astype(ot_ref.dtype)

    units = [(j, kvh) for j in range(nblk) for kvh in range(kvh_n)]
    pending = scores(*units[0])
    for idx, unit in enumerate(units):
        upcoming = scores(*units[idx + 1]) if idx + 1 < len(units) else None
        finish(*unit, *pending)
        pending = upcoming


def _attn_prompt(sinks, qt, k, vt):
    batch, qw, seq = qt.shape
    kw = k.shape[1]
    blk = WINDOW
    nblk = ATTN_BLOCKS_PER_STEP if seq % (ATTN_BLOCKS_PER_STEP * blk) == 0 else 1
    span = nblk * blk
    steps = seq // span
    group = qw // kw
    cols = group * blk
    own = (jnp.arange(blk)[:, None] <= (jnp.arange(cols) % blk)[None, :]).astype(BF16)
    cur_t = lambda b, i: (b, 0, i)
    prev_t = lambda b, i: (b, 0, jnp.maximum(i * nblk - 1, 0))
    cur = lambda b, i: (b * steps + i, 0)
    prev = lambda b, i: (b * steps * nblk + jnp.maximum(i * nblk - 1, 0), 0)
    vmem = (4 * _nbytes((qw, span), BF16) + 6 * _nbytes((span, kw), BF16) + 16 * _nbytes((blk, cols), F32))
    return pl.pallas_call(
        functools.partial(_attn_prompt_body, group=group),
        grid=(batch, steps),
        in_specs=[pl.BlockSpec(memory_space=pltpu.SMEM), _resident((blk, cols)), pl.BlockSpec((1, qw, span), cur_t),
                  pl.BlockSpec((blk, kw), prev), pl.BlockSpec((span, kw), cur),
                  pl.BlockSpec((1, kw, blk), prev_t), pl.BlockSpec((1, kw, span), cur_t)],
        out_specs=pl.BlockSpec((1, qw, span), cur_t),
        out_shape=jax.ShapeDtypeStruct((batch, qw, seq), BF16),
        compiler_params=_params(("parallel", "parallel"), vmem),
        name="attn_prompt",
    )(sinks, own, qt, k, k, vt, vt)


def _attn_sample_body(sink_ref, q_ref, kn_ref, vn_ref, kc_ref, vc_ref, o_ref, kw_ref, vw_ref, *, seq, group, q_start):
    rows = q_ref.shape[0]
    nseq = rows // seq
    win = kc_ref.shape[1]
    hd = SWA_HEAD_DIM
    kw = kn_ref.shape[1]
    heads = q_ref.shape[1] // kw
    srows = heads * rows
    lhs = jnp.concatenate([q_ref[:, hq * kw:(hq + 1) * kw] for hq in range(heads)], axis=0)
    kn = kn_ref[...]
    vn = vn_ref[...]
    sink = sink_ref[...]

    r_c = lax.broadcasted_iota(jnp.int32, (srows, win), 0) % rows
    c_c = lax.broadcasted_iota(jnp.int32, (srows, win), 1)
    seq_c = r_c // seq
    rel_c = (r_c % seq) + win - c_c
    ok_c = (rel_c >= 0) & (rel_c < WINDOW) & (q_start - win + c_c >= 0)
    s_c = _dot_nt(lhs, kc_ref[0].astype(BF16))
    for b in range(1, nseq):
        s_c = jnp.where(seq_c == b, _dot_nt(lhs, kc_ref[b].astype(BF16)), s_c)
    s_c = jnp.where(ok_c, s_c, NEG)

    r_n = lax.broadcasted_iota(jnp.int32, (srows, rows), 0) % rows
    c_n = lax.broadcasted_iota(jnp.int32, (srows, rows), 1)
    rel_n = (r_n % seq) - (c_n % seq)
    ok_n = (rel_n >= 0) & (rel_n < WINDOW) & ((r_n // seq) == (c_n // seq))
    s_n = jnp.where(ok_n, _dot_nt(lhs, kn.astype(BF16)), NEG)

    m = jnp.maximum(jnp.maximum(jnp.max(s_c, axis=-1, keepdims=True), jnp.max(s_n, axis=-1, keepdims=True)), sink)
    e_c = jnp.exp2(s_c - m)
    e_n = jnp.exp2(s_n - m)
    denom = jnp.sum(e_c, axis=-1, keepdims=True) + jnp.sum(e_n, axis=-1, keepdims=True) + jnp.exp2(sink - m)
    acc = _dot(e_n.astype(BF16), vn.astype(BF16))
    for b in range(nseq):
        acc = acc + _dot(jnp.where(seq_c == b, e_c, 0.0).astype(BF16), vc_ref[b].astype(BF16))
    o = acc / denom
    for hq in range(heads):
        kvh = hq // group
        o_ref[:, hq * hd:(hq + 1) * hd] = o[hq * rows:(hq + 1) * rows, kvh * hd:(kvh + 1) * hd].astype(o_ref.dtype)
    for b in range(nseq):
        kw_ref[b, 0:win - seq, :] = kc_ref[b, seq:win, :]
        kw_ref[b, win - seq:win, :] = kn[b * seq:(b + 1) * seq, :]
        vw_ref[b, 0:win - seq, :] = vc_ref[b, seq:win, :]
        vw_ref[b, win - seq:win, :] = vn[b * seq:(b + 1) * seq, :]


def _attn_sample(sinks, q_wide, k_new, v_new, k_cache, v_cache, *, seq, q_start):
    n, qww = q_wide.shape
    kw = k_new.shape[1]
    heads = qww // kw
    hd = SWA_HEAD_DIM
    group = heads // (kw // hd)
    batch, win, _ = k_cache.shape
    nseq = ATTN_SAMPLE_SEQS if batch % ATTN_SAMPLE_SEQS == 0 else batch
    rows = nseq * seq
    sink_rows = jnp.repeat(sinks * LOG2E, rows)[:, None]
    row = lambda i: (i, 0)
    cache = lambda i: (i, 0, 0)
    vmem = (8 * _nbytes((nseq, win, kw), F32) + 4 * _nbytes((rows, qww), BF16)
            + 16 * _nbytes((heads * rows, win + kw), F32))
    return pl.pallas_call(
        functools.partial(_attn_sample_body, seq=seq, group=group, q_start=q_start),
        grid=(batch // nseq,),
        in_specs=[_resident((heads * rows, 1)), pl.BlockSpec((rows, qww), row),
                  pl.BlockSpec((rows, kw), row), pl.BlockSpec((rows, kw), row),
                  pl.BlockSpec((nseq, win, kw), cache), pl.BlockSpec((nseq, win, kw), cache)],
        out_specs=[pl.BlockSpec((rows, heads * hd), row), pl.BlockSpec((nseq, win, kw), cache),
                   pl.BlockSpec((nseq, win, kw), cache)],
        out_shape=[jax.ShapeDtypeStruct((n, heads * hd), BF16), jax.ShapeDtypeStruct((batch, win, kw), F32),
                   jax.ShapeDtypeStruct((batch, win, kw), F32)],
        compiler_params=_params(("parallel",), vmem),
        name="attn_sample",
    )(sink_rows, q_wide, k_new, v_new, k_cache, v_cache)


def _ret_rope_tables(pos, dk):
    inv = 1.0 / (RET_ROPE_THETA ** jnp.linspace(0.0, 1.0, dk // 2, dtype=F32))
    ang = pos[:, None] * inv[None, :]
    return jnp.cos(ang), jnp.sin(ang)


def _ret_key_scale(log_g, seq, n, dk):
    chunk = _ret_chunk(seq)
    tm = min(TOKEN_TILE, n)
    assert tm % chunk == 0
    left = (chunk - 1 - jnp.arange(tm) % chunk).astype(F32)
    per_head = jnp.exp(log_g[None, :] * left[:, None]) * dk ** -0.5
    return jnp.repeat(per_head, dk, axis=1)


def _swa_cos_sin(pos):
    half = ROT_DIM // 2
    inv = ROPE_THETA ** (-jnp.arange(half, dtype=F32) / half)
    ang = pos[:, None] * inv[None, :]
    return jnp.cos(ang), jnp.sin(ang)


def _swa_rope_tables(pos):
    half = ROT_DIM // 2
    cos, sin = _swa_cos_sin(pos)
    n = pos.shape[0]
    pad = jnp.zeros((n, SWA_HEAD_DIM - 2 * half), F32)
    c_head = jnp.concatenate([cos, cos, pad + 1.0], axis=1)
    sa_head = jnp.concatenate([-sin, jnp.zeros_like(sin), pad], axis=1)
    sb_head = jnp.concatenate([jnp.zeros_like(sin), sin, pad], axis=1)
    reps = LANES // SWA_HEAD_DIM
    return tuple(jnp.tile(t, (1, reps)) for t in (c_head, sa_head, sb_head))


def _tile_rows(tab, seq, n):
    tm = min(TOKEN_TILE, n)
    return tab if seq >= tm else jnp.tile(tab, (tm // seq, 1))


def kernel(x_prompt, x_sample, state_ret, cache_k_win, cache_v_win, ret_norm_pre, ret_w_in, ret_w_out, ret_norm_post, kv_norm, w_kv, swa_norm_pre, swa_w_q, swa_sinks, swa_w_o, swa_norm_post, ffn_norm_pre, ffn_w1, ffn_w2, ffn_norm_post):
    n_a = DEPTH // 2
    assert n_a == 1 and DEPTH == 2, "one retention layer followed by one sliding-window layer"
    d = x_prompt.shape[-1]
    heads = RET_HEADS
    dk = ret_w_out.shape[-1] // heads
    dv = ret_w_out.shape[-2] // heads
    kvh, hd = SWA_KV_HEADS, SWA_HEAD_DIM
    row2 = lambda g: g.reshape(1, d)
    log_g = jnp.log1p(-jnp.exp2(-5.0 - jnp.arange(heads, dtype=F32)))

    w_in = ret_w_in[0].astype(BF16)
    w_out = ret_w_out[0].astype(BF16)
    wq = swa_w_q[0].astype(BF16)
    wkv = w_kv.astype(BF16)
    wo = swa_w_o[0].astype(BF16)
    w1 = [ffn_w1[l].astype(BF16) for l in range(DEPTH)]
    w2 = [ffn_w2[l].astype(BF16) for l in range(DEPTH)]
    sinks = swa_sinks[0]
    q_heads = wq.shape[1] // hd
    on_kv_head = (jnp.arange(q_heads)[:, None] // (q_heads // kvh) == jnp.arange(kvh)[None, :]).astype(BF16)
    wq_wide = (wq.reshape(d, q_heads, 1, hd) * on_kv_head[None, :, :, None]).reshape(d, q_heads * kvh * hd)

    def trunk(x, pos, ret_mixer, swa_mixer):
        b, t, _ = x.shape
        n = b * t
        h = x.reshape(n, d)
        cos, sin = (_tile_rows(tab, t, n) for tab in _ret_rope_tables(pos, dk))
        o, o_layout, state = ret_mixer(h, cos, sin, _ret_key_scale(log_g, t, n, dk))
        h = _out_ffn(o, h, w_out, row2(ret_norm_post[0]), row2(ffn_norm_pre[0]), w1[0], w2[0], row2(ffn_norm_post[0]),
                     o_layout=o_layout)
        o, o_layout, k_win, v_win = swa_mixer(h, pos, b, t)
        h = _out_ffn(o, h, wo, row2(swa_norm_post[0]), row2(ffn_norm_pre[1]), w1[1], w2[1], row2(ffn_norm_post[1]),
                     o_layout=o_layout)
        return h.reshape(b, t, d), state, k_win, v_win

    b_p, t_p, _ = x_prompt.shape
    w_p = min(WINDOW, t_p)

    def swa_prompt(h, pos, b, t):
        cos, sin = _swa_cos_sin(pos)
        qt, k, vt, k_win, v_win = _swa_in_t(h, row2(swa_norm_pre[0]), row2(kv_norm), wq, wkv, cos.T, sin.T,
                                            *_swa_rope_tables(pos), batch=b, win=w_p)
        return _attn_prompt(sinks, qt, k, vt), "features", k_win, v_win

    def ret_prompt(h, cos, sin, kscale):
        o, state = _ret_prompt(log_g, h, row2(ret_norm_pre[0]), w_in, cos, sin, kscale,
                               batch=b_p, heads=heads, dk=dk, dv=dv)
        return o, "tokens", state

    y_prompt, state_p, k_win_p, v_win_p = trunk(x_prompt, jnp.arange(t_p, dtype=F32), ret_prompt, swa_prompt)

    b_s, t_s, _ = x_sample.shape
    w_s = cache_k_win.shape[1]
    kc = cache_k_win.reshape(b_s, w_s, kvh * hd)
    vc = cache_v_win.reshape(b_s, w_s, kvh * hd)

    def swa_sample(h, pos, b, t):
        tabs = tuple(_tile_rows(tab, t, b * t) for tab in _swa_rope_tables(pos))
        q, k, v = _swa_in(h, row2(swa_norm_pre[0]), row2(kv_norm), wq_wide, wkv, *tabs, q_dtype=BF16)
        o, k_win, v_win = _attn_sample(sinks, q, k, v, kc, vc, seq=t, q_start=PAST_LEN)
        return o, "tokens", k_win, v_win

    def ret_sample(h, cos, sin, kscale):
        q, kd, v, sg = _ret_in(h, row2(ret_norm_pre[0]), w_in, cos, sin, kscale,
                               heads=heads, dk=dk, dv=dv, out_dtype=F32)
        o, state = _ret_sample(log_g, q, kd, v, sg, state_ret[0], seq=t_s)
        return o, "heads", state

    y_sample, state_s, k_win_s, v_win_s = trunk(x_sample, PAST_LEN + jnp.arange(t_s, dtype=F32), ret_sample, swa_sample)

    return (y_prompt, y_sample, state_p[None], state_s[None],
            k_win_p.reshape(b_p, w_p, kvh, hd), v_win_p.reshape(b_p, w_p, kvh, hd),
            k_win_s.reshape(b_s, w_s, kvh, hd), v_win_s.reshape(b_s, w_s, kvh, hd))
```

```python
import functools

import jax
import jax.numpy as jnp
from jax import lax
from jax.experimental import pallas as pl
from jax.experimental.pallas import tpu as pltpu

DEPTH = 2
PAST_LEN = 16384
RET_HEADS = 4
RET_ROPE_THETA = 10000.0
SWA_HEAD_DIM = 64
SWA_KV_HEADS = 4
WINDOW = 128
ROPE_THETA = 500000.0
ROT_DIM = SWA_HEAD_DIM // 4
EPS = 1e-6
NEG = -1e30
LOG2E = 1.4426950408889634

LANES = 128
SUBLANES = 8
BF16_SUBLANES = 16
VMEM_CAP_BYTES = 64 * 1024 * 1024
VMEM_BUDGET_BYTES = VMEM_CAP_BYTES - 8 * 1024 * 1024

TOKEN_TILE = 512
COL_CHUNK = 1024
RET_KERNEL_CHUNK = 256
SAMPLE_GROUP = 2
ATTN_SAMPLE_SEQS = 4
ATTN_BLOCKS_PER_STEP = 8
ATTN_LOOKAHEAD = 4
OUT_FFN_ROW_PARTS = 2

F32 = jnp.float32
BF16 = jnp.bfloat16


def _params(semantics, vmem_bytes):
    limit = int(min(max(vmem_bytes, 16 * 1024 * 1024), VMEM_BUDGET_BYTES))
    return pltpu.CompilerParams(dimension_semantics=semantics, vmem_limit_bytes=limit)


def _resident(shape):
    nd = len(shape)
    return pl.BlockSpec(shape, lambda *_: (0,) * nd, pipeline_mode=pl.Buffered(1))


def _nbytes(shape, dtype):
    n = 1
    for s in shape:
        n *= s
    return n * jnp.dtype(dtype).itemsize


def _rms_rows(x):
    return x * lax.rsqrt(jnp.mean(x * x, axis=-1, keepdims=True) + EPS)


def _dot(a, b):
    return jnp.dot(a, b, preferred_element_type=F32)


def _dot_nt(a, b):
    return lax.dot_general(a, b, (((1,), (1,)), ((), ())), preferred_element_type=F32)


def _dot_tn(a, b):
    return lax.dot_general(a, b, (((0,), (0,)), ((), ())), preferred_element_type=F32)


def _ret_in_body(h_ref, g_ref, w_ref, cos_ref, sin_ref, kscale_ref, q_ref, kd_ref, v_ref, sg_ref, *, heads, dk, dv,
                 rows=slice(None)):
    xn = (_rms_rows(h_ref[rows, :]) * g_ref[...]).astype(BF16)
    cos = cos_ref[rows, :]
    sin = sin_ref[rows, :]
    half = dk // 2
    qk_w = heads * dk
    v_w = heads * dv

    def proj(lo, width):
        return _dot(xn, w_ref[:, lo:lo + width])

    for base, ref, scale_ref in ((0, q_ref, None), (qk_w, kd_ref, kscale_ref)):
        p = proj(base, qk_w)
        for hh in range(heads):
            lo, mid, hi = hh * dk, hh * dk + half, (hh + 1) * dk
            x1 = p[:, lo:mid]
            x2 = p[:, mid:hi]
            o1 = x1 * cos - x2 * sin
            o2 = x2 * cos + x1 * sin
            if scale_ref is not None:
                o1 = o1 * scale_ref[rows, lo:mid]
                o2 = o2 * scale_ref[rows, mid:hi]
            ref[hh, rows, :half] = o1.astype(ref.dtype)
            ref[hh, rows, half:] = o2.astype(ref.dtype)
    cw = min(COL_CHUNK, v_w)
    per_chunk = cw // dv
    for c in range(v_w // cw):
        v = proj(2 * qk_w + c * cw, cw)
        for j in range(per_chunk):
            v_ref[c * per_chunk + j, rows, :] = v[:, j * dv:(j + 1) * dv].astype(v_ref.dtype)
    for c in range(v_w // cw):
        gate = proj(2 * qk_w + v_w + c * cw, cw)
        sg = gate * jax.nn.sigmoid(gate)
        for j in range(per_chunk):
            sg_ref[c * per_chunk + j, rows, :] = sg[:, j * dv:(j + 1) * dv].astype(sg_ref.dtype)


def _ret_in(h, g, w_in, cos, sin, kscale, *, heads, dk, dv, out_dtype):
    n, d = h.shape
    tm = min(TOKEN_TILE, n)
    qk_w, v_w = heads * dk, heads * dv
    pos_tiles = cos.shape[0] // tm
    row = lambda i: (i, 0)
    tab = lambda i: (i % pos_tiles, 0)
    by_head = lambda i: (0, i, 0)
    vmem = (2 * _nbytes((tm, d), F32) + _nbytes(w_in.shape, BF16) + 4 * _nbytes((tm, dk // 2), F32)
            + _nbytes((tm, qk_w), F32) + 2 * _nbytes((tm, 2 * qk_w + 2 * v_w), out_dtype)
            + 4 * _nbytes((tm, COL_CHUNK), F32))
    return pl.pallas_call(
        functools.partial(_ret_in_body, heads=heads, dk=dk, dv=dv),
        grid=(n // tm,),
        in_specs=[pl.BlockSpec((tm, d), row), _resident((1, d)), _resident(w_in.shape),
                  pl.BlockSpec((tm, dk // 2), tab), pl.BlockSpec((tm, dk // 2), tab), _resident((tm, qk_w))],
        out_specs=[pl.BlockSpec((heads, tm, dk), by_head), pl.BlockSpec((heads, tm, dk), by_head),
                   pl.BlockSpec((heads, tm, dv), by_head), pl.BlockSpec((heads, tm, dv), by_head)],
        out_shape=[jax.ShapeDtypeStruct((heads, n, dk), out_dtype), jax.ShapeDtypeStruct((heads, n, dk), out_dtype),
                   jax.ShapeDtypeStruct((heads, n, dv), out_dtype), jax.ShapeDtypeStruct((heads, n, dv), out_dtype)],
        compiler_params=_params(("parallel",), vmem),
        name="ret_in",
    )(h, g, w_in, cos, sin, kscale)


def _ret_prompt_body(lg_ref, h_ref, g_ref, w_ref, cos_ref, sin_ref, kscale_ref, o_ref, s_out_ref,
                     q_s, kd_s, v_s, sg_s, s_ref, *, heads, dk, dv, chunk):
    t = pl.program_id(1)

    @pl.when(t == 0)
    def _():
        s_ref[...] = jnp.zeros_like(s_ref)

    tm = h_ref.shape[0]
    for c in range(tm // chunk):
        _ret_in_body(h_ref, g_ref, w_ref, cos_ref, sin_ref, kscale_ref, q_s, kd_s, v_s, sg_s,
                     heads=heads, dk=dk, dv=dv, rows=pl.ds(c * chunk, chunk))

    ri = lax.broadcasted_iota(jnp.int32, (chunk, chunk), 0)
    ci = lax.broadcasted_iota(jnp.int32, (chunk, chunk), 1)
    lower = (ri >= ci).astype(F32)
    row_v = lax.broadcasted_iota(jnp.int32, (chunk, dv), 0).astype(F32)
    causal, q_decay, chunk_decay = [], [], []
    for hh in range(heads):
        lg = lg_ref[hh]
        causal.append(lower * jnp.exp(jnp.full((1, chunk), -lg * chunk, F32)))
        q_decay.append(jnp.exp(lg * (row_v + 1.0)))
        chunk_decay.append(jnp.exp(jnp.full((1, dv), lg * chunk, F32)))

    for c in range(tm // chunk):
        rows = pl.ds(c * chunk, chunk)
        qk = [_dot_nt(q_s[hh, rows, :], kd_s[hh, rows, :]) for hh in range(heads)]
        grow = [_dot_tn(kd_s[hh, rows, :], v_s[hh, rows, :]) for hh in range(heads)]
        for hh in range(heads):
            s_prev = s_ref[hh]
            lhs = jnp.concatenate([(qk[hh] * causal[hh]).astype(BF16), q_s[hh, rows, :]], axis=1)
            rhs = jnp.concatenate([v_s[hh, rows, :], s_prev.astype(BF16)], axis=0)
            o = _rms_rows(q_decay[hh] * _dot(lhs, rhs))
            s_ref[hh] = chunk_decay[hh] * s_prev + grow[hh]
            o_ref[rows, hh * dv:(hh + 1) * dv] = (o * sg_s[hh, rows, :].astype(F32)).astype(o_ref.dtype)

    @pl.when(t == pl.num_programs(1) - 1)
    def _():
        s_out_ref[0] = s_ref[...]


def _ret_chunk(seq):
    return RET_KERNEL_CHUNK if seq % RET_KERNEL_CHUNK == 0 else seq


def _ret_prompt(log_g, h, g, w_in, cos, sin, kscale, *, batch, heads, dk, dv):
    n, d = h.shape
    seq = n // batch
    tm = min(TOKEN_TILE, seq)
    chunk = _ret_chunk(seq)
    assert tm % chunk == 0 and seq % tm == 0
    nt = seq // tm
    qk_w, v_w = heads * dk, heads * dv
    row = lambda b, t: (b * nt + t, 0)
    tab = lambda b, t: (t, 0)
    vmem = (2 * _nbytes((tm, d), F32) + _nbytes(w_in.shape, BF16) + 4 * _nbytes((tm, dk // 2), F32)
            + _nbytes((tm, qk_w), F32) + 2 * _nbytes((tm, v_w), BF16) + _nbytes((tm, 2 * qk_w + 2 * v_w), BF16)
            + 3 * _nbytes((heads, dk, dv), F32) + 4 * _nbytes((tm, COL_CHUNK), F32)
            + 2 * heads * (_nbytes((chunk, chunk), F32) + _nbytes((dk, dv), F32) + _nbytes((chunk, dv), F32)))
    return pl.pallas_call(
        functools.partial(_ret_prompt_body, heads=heads, dk=dk, dv=dv, chunk=chunk),
        grid=(batch, nt),
        in_specs=[pl.BlockSpec(memory_space=pltpu.SMEM),
                  pl.BlockSpec((tm, d), row), _resident((1, d)), _resident(w_in.shape),
                  pl.BlockSpec((tm, dk // 2), tab), pl.BlockSpec((tm, dk // 2), tab), _resident((tm, qk_w))],
        out_specs=[pl.BlockSpec((tm, v_w), row),
                   pl.BlockSpec((1, heads, dk, dv), lambda b, t: (b, 0, 0, 0))],
        out_shape=[jax.ShapeDtypeStruct((n, v_w), BF16),
                   jax.ShapeDtypeStruct((batch, heads, dk, dv), F32)],
        scratch_shapes=[pltpu.VMEM((heads, tm, dk), BF16), pltpu.VMEM((heads, tm, dk), BF16),
                        pltpu.VMEM((heads, tm, dv), BF16), pltpu.VMEM((heads, tm, dv), BF16),
                        pltpu.VMEM((heads, dk, dv), F32)],
        compiler_params=_params(("parallel", "arbitrary"), vmem),
        name="ret_prompt",
    )(log_g, h, g, w_in, cos, sin, kscale)


def _ret_sample_body(lg_ref, q_ref, kd_ref, v_ref, sg_ref, s_in_ref, o_ref, s_out_ref, *, seq):
    heads, rows, dk = q_ref.shape
    dv = v_ref.shape[2]
    group = rows // seq
    ri = lax.broadcasted_iota(jnp.int32, (rows, rows), 0)
    ci = lax.broadcasted_iota(jnp.int32, (rows, rows), 1)
    visible = ((ri // seq) == (ci // seq)) & (ri >= ci)
    row_v = lax.broadcasted_iota(jnp.int32, (rows, dv), 0)
    row_k = lax.broadcasted_iota(jnp.int32, (rows, dk), 0)
    for hh in range(heads):
        lg = lg_ref[hh]
        causal = jnp.where(visible, jnp.exp(jnp.full((rows, rows), -lg * seq, F32)), 0.0)
        q_decay = jnp.exp(lg * ((row_v % seq).astype(F32) + 1.0))
        chunk_decay = jnp.exp(jnp.full((1, dv), lg * seq, F32))
        q = q_ref[hh].astype(BF16)
        kd = kd_ref[hh]
        v = v_ref[hh].astype(BF16)
        o = _dot((_dot_nt(q, kd.astype(BF16)) * causal).astype(BF16), v)
        for g in range(group):
            s_prev = s_in_ref[g, hh]
            o = jnp.where((row_v // seq) == g, o + _dot(q, s_prev.astype(BF16)), o)
            kd_g = jnp.where((row_k // seq) == g, kd, 0.0).astype(BF16)
            s_out_ref[g, hh] = chunk_decay * s_prev + _dot_tn(kd_g, v)
        o = _rms_rows(q_decay * o)
        o_ref[hh] = (o * sg_ref[hh]).astype(o_ref.dtype)


def _ret_sample(log_g, q, kd, v, sg, state, *, seq):
    heads, n, dk = q.shape
    dv = v.shape[2]
    batch = n // seq
    group = SAMPLE_GROUP if batch % SAMPLE_GROUP == 0 else batch
    rows = group * seq
    by_head = lambda i: (0, i, 0)
    st = lambda i: (i, 0, 0, 0)
    vmem = (4 * _nbytes((group, heads, dk, dv), F32) + 8 * _nbytes((rows, heads * dv), F32)
            + 4 * _nbytes((dk, dv), F32))
    return pl.pallas_call(
        functools.partial(_ret_sample_body, seq=seq),
        grid=(batch // group,),
        in_specs=[pl.BlockSpec(memory_space=pltpu.SMEM),
                  pl.BlockSpec((heads, rows, dk), by_head), pl.BlockSpec((heads, rows, dk), by_head),
                  pl.BlockSpec((heads, rows, dv), by_head), pl.BlockSpec((heads, rows, dv), by_head),
                  pl.BlockSpec((group, heads, dk, dv), st)],
        out_specs=[pl.BlockSpec((heads, rows, dv), by_head), pl.BlockSpec((group, heads, dk, dv), st)],
        out_shape=[jax.ShapeDtypeStruct((heads, n, dv), F32),
                   jax.ShapeDtypeStruct((batch, heads, dk, dv), F32)],
        compiler_params=_params(("parallel",), vmem),
        name="ret_sample",
    )(log_g, q, kd, v, sg, state)


def _out_ffn_body(o_ref, h_ref, wo_ref, g_post_ref, g_pre_ref, w1_ref, w2_ref, g_ffn_ref, y_ref, *, o_layout):
    tm = h_ref.shape[0]
    parts = OUT_FFN_ROW_PARTS if tm % (OUT_FFN_ROW_PARTS * BF16_SUBLANES) == 0 else 1
    rp = tm // parts
    d_ff = w1_ref.shape[1]
    fc = min(COL_CHUNK, d_ff)

    def mixer_out(p):
        rows = slice(p * rp, (p + 1) * rp)
        if o_layout == "features":
            return _dot_tn(o_ref[0, :, rows], wo_ref[...])
        if o_layout == "heads":
            o = jnp.concatenate([o_ref[hh, rows, :] for hh in range(o_ref.shape[0])], axis=1)
            return _dot(o.astype(BF16), wo_ref[...])
        return _dot(o_ref[rows, :].astype(BF16), wo_ref[...])

    a = [mixer_out(p) for p in range(parts)]
    h1, x = [], []
    for p in range(parts):
        rows = slice(p * rp, (p + 1) * rp)
        h1.append(h_ref[rows, :] + _rms_rows(a[p]) * g_post_ref[...])
        x.append((_rms_rows(h1[p]) * g_pre_ref[...]).astype(BF16))
    acc = [jnp.zeros((rp, h_ref.shape[1]), F32) for _ in range(parts)]
    for c in range(d_ff // fc):
        for p in range(parts):
            u = jnp.maximum(_dot(x[p], w1_ref[:, c * fc:(c + 1) * fc]), 0.0)
            acc[p] = acc[p] + _dot((u * u).astype(BF16), w2_ref[c * fc:(c + 1) * fc, :])
    for p in range(parts):
        rows = slice(p * rp, (p + 1) * rp)
        y_ref[rows, :] = h1[p] + _rms_rows(acc[p]) * g_ffn_ref[...]


def _out_ffn(o, h, w_o, g_post, g_pre, w1, w2, g_ffn, *, o_layout):
    n, d = h.shape
    kdim = w_o.shape[0]
    tm = min(TOKEN_TILE, n)
    row = lambda i: (i, 0)
    if o_layout == "features":
        tiles = o.shape[2] // tm
        o_spec = pl.BlockSpec((1, kdim, tm), lambda i: (i // tiles, 0, i % tiles))
    elif o_layout == "heads":
        o_spec = pl.BlockSpec((o.shape[0], tm, o.shape[2]), lambda i: (0, i, 0))
    else:
        o_spec = pl.BlockSpec((tm, kdim), row)
    vmem = (2 * _nbytes((tm, kdim), o.dtype) + 4 * _nbytes((tm, d), F32) + _nbytes(w_o.shape, BF16)
            + _nbytes(w1.shape, BF16) + _nbytes(w2.shape, BF16) + 6 * _nbytes((tm, COL_CHUNK), F32))
    return pl.pallas_call(
        functools.partial(_out_ffn_body, o_layout=o_layout),
        grid=(n // tm,),
        in_specs=[o_spec, pl.BlockSpec((tm, d), row), _resident(w_o.shape),
                  _resident((1, d)), _resident((1, d)), _resident(w1.shape), _resident(w2.shape),
                  _resident((1, d))],
        out_specs=pl.BlockSpec((tm, d), row),
        out_shape=jax.ShapeDtypeStruct((n, d), F32),
        compiler_params=_params(("parallel",), vmem),
        name="out_ffn",
    )(o, h, w_o, g_post, g_pre, w1, w2, g_ffn)


def _partial_rope(x, c_tab, sa_tab, sb_tab):
    half = ROT_DIM // 2
    outs = []
    for j in range(x.shape[1] // LANES):
        s = x[:, j * LANES:(j + 1) * LANES]
        outs.append(s * c_tab + pltpu.roll(s, LANES - half, axis=1) * sa_tab + pltpu.roll(s, half, axis=1) * sb_tab)
    return outs


def _swa_in_body(h_ref, g_q_ref, g_kv_ref, wq_ref, wkv_ref, c_ref, sa_ref, sb_ref, q_ref, k_ref, v_ref):
    y = _rms_rows(h_ref[...])
    xq = (y * g_q_ref[...]).astype(BF16)
    xkv = (y * g_kv_ref[...]).astype(BF16)
    c_tab, sa_tab, sb_tab = c_ref[...], sa_ref[...], sb_ref[...]
    q = _dot(xq, wq_ref[...]) * (SWA_HEAD_DIM ** -0.5 * LOG2E)
    for j, s in enumerate(_partial_rope(q, c_tab, sa_tab, sb_tab)):
        q_ref[:, j * LANES:(j + 1) * LANES] = s.astype(q_ref.dtype)
    kv = _dot(xkv, wkv_ref[...])
    kw = k_ref.shape[1]
    for j, s in enumerate(_partial_rope(kv[:, :kw], c_tab, sa_tab, sb_tab)):
        k_ref[:, j * LANES:(j + 1) * LANES] = s
    v_ref[...] = kv[:, kw:]


def _swa_in(h, g_q, g_kv, w_q, w_kv, c_tab, sa_tab, sb_tab, *, q_dtype):
    n, d = h.shape
    tm = min(TOKEN_TILE, n)
    qw = w_q.shape[1]
    kw = w_kv.shape[1] // 2
    pos_tiles = c_tab.shape[0] // tm
    row = lambda i: (i, 0)
    tab = lambda i: (i % pos_tiles, 0)
    vmem = (2 * _nbytes((tm, d), F32) + _nbytes(w_q.shape, BF16) + _nbytes(w_kv.shape, BF16)
            + 6 * _nbytes((tm, LANES), F32) + 2 * _nbytes((tm, qw), q_dtype) + 4 * _nbytes((tm, kw), F32)
            + 6 * _nbytes((tm, qw), F32))
    return pl.pallas_call(
        _swa_in_body,
        grid=(n // tm,),
        in_specs=[pl.BlockSpec((tm, d), row), _resident((1, d)), _resident((1, d)),
                  _resident(w_q.shape), _resident(w_kv.shape),
                  pl.BlockSpec((tm, LANES), tab), pl.BlockSpec((tm, LANES), tab), pl.BlockSpec((tm, LANES), tab)],
        out_specs=[pl.BlockSpec((tm, qw), row), pl.BlockSpec((tm, kw), row), pl.BlockSpec((tm, kw), row)],
        out_shape=[jax.ShapeDtypeStruct((n, qw), q_dtype), jax.ShapeDtypeStruct((n, kw), F32),
                   jax.ShapeDtypeStruct((n, kw), F32)],
        compiler_params=_params(("parallel",), vmem),
        name="swa_in",
    )(h, g_q, g_kv, w_q, w_kv, c_tab, sa_tab, sb_tab)


def _swa_in_t_body(h_ref, g_q_ref, g_kv_ref, wqt_ref, wk_ref, wvt_ref, wv_ref, cos_t_ref, sin_t_ref,
                   c_ref, sa_ref, sb_ref, qt_ref, k_ref, vt_ref, kwin_ref, vwin_ref, *, tiles):
    y = _rms_rows(h_ref[...])
    xq = (y * g_q_ref[...]).astype(BF16)
    xkv = (y * g_kv_ref[...]).astype(BF16)
    tm = xq.shape[0]
    hd = SWA_HEAD_DIM
    half = ROT_DIM // 2
    cos_t, sin_t = cos_t_ref[...], sin_t_ref[...]
    qt = _dot_nt(wqt_ref[...], xq) * (hd ** -0.5 * LOG2E)
    for hq in range(qt.shape[0] // hd):
        base = hq * hd
        x1 = qt[base:base + half]
        x2 = qt[base + half:base + 2 * half]
        rot = jnp.concatenate([x1 * cos_t - x2 * sin_t, x2 * cos_t + x1 * sin_t], axis=0)
        qt_ref[0, base:base + 2 * half, :] = rot.astype(qt_ref.dtype)
        qt_ref[0, base + 2 * half:base + hd, :] = qt[base + 2 * half:base + hd].astype(qt_ref.dtype)
    k_rot = _partial_rope(_dot(xkv, wk_ref[...]), c_ref[...], sa_ref[...], sb_ref[...])
    for j, s in enumerate(k_rot):
        k_ref[:, j * LANES:(j + 1) * LANES] = s.astype(k_ref.dtype)
    vt_ref[0] = _dot_nt(wvt_ref[...], xkv).astype(vt_ref.dtype)
    win = kwin_ref.shape[0]

    @pl.when(pl.program_id(0) % tiles == tiles - 1)
    def _():
        for j, s in enumerate(k_rot):
            kwin_ref[:, j * LANES:(j + 1) * LANES] = s[tm - win:, :]
        vwin_ref[...] = _dot(xkv[tm - win:, :], wv_ref[...])


def _swa_in_t(h, g_q, g_kv, w_q, w_kv, cos_t, sin_t, c_tab, sa_tab, sb_tab, *, batch, win):
    n, d = h.shape
    seq = n // batch
    tm = min(TOKEN_TILE, seq)
    tiles = seq // tm
    qw = w_q.shape[1]
    kw = w_kv.shape[1] // 2
    wqt = w_q.T
    wk, wv = w_kv[:, :kw], w_kv[:, kw:]
    wvt = wv.T
    row = lambda i: (i, 0)
    tab = lambda i: (i % tiles, 0)
    tab_t = lambda i: (0, i % tiles)
    feat = lambda i: (i // tiles, 0, i % tiles)
    per_seq = lambda i: (i // tiles, 0)
    half = ROT_DIM // 2
    vmem = (2 * _nbytes((tm, d), F32) + 2 * _nbytes(w_q.shape, BF16) + 3 * _nbytes(w_kv.shape, BF16)
            + 8 * _nbytes((tm, LANES), F32) + 2 * _nbytes((tm, qw + 2 * kw), BF16) + 4 * _nbytes((win, kw), F32)
            + 4 * _nbytes((tm, qw), F32))
    return pl.pallas_call(
        functools.partial(_swa_in_t_body, tiles=tiles),
        grid=(n // tm,),
        in_specs=[pl.BlockSpec((tm, d), row), _resident((1, d)), _resident((1, d)),
                  _resident(wqt.shape), _resident(wk.shape), _resident(wvt.shape), _resident(wv.shape),
                  pl.BlockSpec((half, tm), tab_t), pl.BlockSpec((half, tm), tab_t),
                  pl.BlockSpec((tm, LANES), tab), pl.BlockSpec((tm, LANES), tab), pl.BlockSpec((tm, LANES), tab)],
        out_specs=[pl.BlockSpec((1, qw, tm), feat), pl.BlockSpec((tm, kw), row), pl.BlockSpec((1, kw, tm), feat),
                   pl.BlockSpec((win, kw), per_seq), pl.BlockSpec((win, kw), per_seq)],
        out_shape=[jax.ShapeDtypeStruct((batch, qw, seq), BF16), jax.ShapeDtypeStruct((n, kw), BF16),
                   jax.ShapeDtypeStruct((batch, kw, seq), BF16),
                   jax.ShapeDtypeStruct((batch * win, kw), F32), jax.ShapeDtypeStruct((batch * win, kw), F32)],
        compiler_params=_params(("arbitrary",), vmem),
        name="swa_in_t",
    )(h, g_q, g_kv, wqt, wk, wvt, wv, cos_t, sin_t, c_tab, sa_tab, sb_tab)


def _attn_prompt_body(sinks_ref, mask_ref, qt_ref, kp_ref, kc_ref, vtp_ref, vtc_ref, ot_ref, *, group):
    blk = kp_ref.shape[0]
    nblk = kc_ref.shape[0] // blk
    hd = SWA_HEAD_DIM
    kvh_n = kc_ref.shape[1] // hd
    cols = group * blk
    kj = lax.broadcasted_iota(jnp.int32, (blk, cols), 0)
    qi = lax.broadcasted_iota(jnp.int32, (blk, cols), 1) % blk
    own = kj <= qi
    lane_head = lax.broadcasted_iota(jnp.int32, (1, cols), 1) // blk
    ones_rows = jnp.ones((BF16_SUBLANES, blk), BF16)
    has_prev = pl.program_id(1) > 0

    def scores(j, kvh):
        tile, lo = divmod(kvh * hd, LANES)
        q4t = jnp.concatenate([qt_ref[0, (kvh * group + g) * hd:(kvh * group + g + 1) * hd, j * blk:(j + 1) * blk]
                               for g in range(group)], axis=1)
        rhs = jnp.concatenate([q4t if part * hd == lo else jnp.zeros_like(q4t) for part in range(LANES // hd)], axis=0)
        k_tile = slice(tile * LANES, (tile + 1) * LANES)
        k_prev = kp_ref[:, k_tile] if j == 0 else kc_ref[(j - 1) * blk:j * blk, k_tile]
        return _dot(kc_ref[j * blk:(j + 1) * blk, k_tile], rhs), _dot(k_prev, rhs)

    def finish(j, kvh, s_own, s_prev):
        if j == 0:
            s_prev = jnp.where(has_prev, s_prev, NEG)
        s = jnp.where(own, s_own, s_prev)
        sink = jnp.full((1, cols), sinks_ref[kvh * group] * LOG2E, F32)
        for g in range(1, group):
            sink = jnp.where(lane_head == g, sinks_ref[kvh * group + g] * LOG2E, sink)
        m = jnp.maximum(jnp.max(s, axis=0, keepdims=True), sink)
        e = jnp.exp2(s - m).astype(BF16)
        p_own = e * mask_ref[...]
        p = jnp.concatenate([p_own, e - p_own], axis=0)
        head_rows = slice(kvh * hd, (kvh + 1) * hd)
        vt_own = vtc_ref[0, head_rows, j * blk:(j + 1) * blk]
        vt_prev = vtp_ref[0, head_rows, :] if j == 0 else vtc_ref[0, head_rows, (j - 1) * blk:j * blk]
        vt = jnp.concatenate([jnp.concatenate([vt_own, ones_rows], axis=0),
                              jnp.concatenate([vt_prev, ones_rows], axis=0)], axis=1)
        acc = _dot(vt, p)
        denom = acc[hd:hd + 1, :] + jnp.exp2(sink - m)
        ot = acc[:hd] / denom
        for g in range(group):
            hq = kvh * group + g
            ot_ref[0, hq * hd:(hq + 1) * hd, j * blk:(j + 1) * blk] = ot[:, g * blk:(g + 1) * blk].astype(ot_ref.dtype)

    units = [(j, kvh) for j in range(nblk) for kvh in range(kvh_n)]
    queue = [scores(*unit) for unit in units[:ATTN_LOOKAHEAD]]
    for idx, unit in enumerate(units):
        if idx + ATTN_LOOKAHEAD < len(units):
            queue.append(scores(*units[idx + ATTN_LOOKAHEAD]))
        finish(*unit, *queue.pop(0))


def _attn_prompt(sinks, qt, k, vt):
    batch, qw, seq = qt.shape
    kw = k.shape[1]
    blk = WINDOW
    nblk = ATTN_BLOCKS_PER_STEP if seq % (ATTN_BLOCKS_PER_STEP * blk) == 0 else 1
    span = nblk * blk
    steps = seq // span
    group = qw // kw
    cols = group * blk
    own = (jnp.arange(blk)[:, None] <= (jnp.arange(cols) % blk)[None, :]).astype(BF16)
    cur_t = lambda b, i: (b, 0, i)
    prev_t = lambda b, i: (b, 0, jnp.maximum(i * nblk - 1, 0))
    cur = lambda b, i: (b * steps + i, 0)
    prev = lambda b, i: (b * steps * nblk + jnp.maximum(i * nblk - 1, 0), 0)
    vmem = (4 * _nbytes((qw, span), BF16) + 6 * _nbytes((span, kw), BF16) + 16 * _nbytes((blk, cols), F32))
    return pl.pallas_call(
        functools.partial(_attn_prompt_body, group=group),
        grid=(batch, steps),
        in_specs=[pl.BlockSpec(memory_space=pltpu.SMEM), _resident((blk, cols)), pl.BlockSpec((1, qw, span), cur_t),
                  pl.BlockSpec((blk, kw), prev), pl.BlockSpec((span, kw), cur),
                  pl.BlockSpec((1, kw, blk), prev_t), pl.BlockSpec((1, kw, span), cur_t)],
        out_specs=pl.BlockSpec((1, qw, span), cur_t),
        out_shape=jax.ShapeDtypeStruct((batch, qw, seq), BF16),
        compiler_params=_params(("parallel", "parallel"), vmem),
        name="attn_prompt",
    )(sinks, own, qt, k, k, vt, vt)


def _attn_sample_body(sink_ref, q_ref, kn_ref, vn_ref, kc_ref, vc_ref, o_ref, kw_ref, vw_ref, *, seq, group, q_start):
    rows = q_ref.shape[0]
    nseq = rows // seq
    win = kc_ref.shape[1]
    hd = SWA_HEAD_DIM
    kw = kn_ref.shape[1]
    heads = q_ref.shape[1] // kw
    srows = heads * rows
    lhs = jnp.concatenate([q_ref[:, hq * kw:(hq + 1) * kw] for hq in range(heads)], axis=0)
    kn = kn_ref[...]
    vn = vn_ref[...]
    sink = sink_ref[...]

    r_c = lax.broadcasted_iota(jnp.int32, (srows, win), 0) % rows
    c_c = lax.broadcasted_iota(jnp.int32, (srows, win), 1)
    seq_c = r_c // seq
    rel_c = (r_c % seq) + win - c_c
    ok_c = (rel_c >= 0) & (rel_c < WINDOW) & (q_start - win + c_c >= 0)
    s_c = _dot_nt(lhs, kc_ref[0].astype(BF16))
    for b in range(1, nseq):
        s_c = jnp.where(seq_c == b, _dot_nt(lhs, kc_ref[b].astype(BF16)), s_c)
    s_c = jnp.where(ok_c, s_c, NEG)

    r_n = lax.broadcasted_iota(jnp.int32, (srows, rows), 0) % rows
    c_n = lax.broadcasted_iota(jnp.int32, (srows, rows), 1)
    rel_n = (r_n % seq) - (c_n % seq)
    ok_n = (rel_n >= 0) & (rel_n < WINDOW) & ((r_n // seq) == (c_n // seq))
    s_n = jnp.where(ok_n, _dot_nt(lhs, kn.astype(BF16)), NEG)

    m = jnp.maximum(jnp.maximum(jnp.max(s_c, axis=-1, keepdims=True), jnp.max(s_n, axis=-1, keepdims=True)), sink)
    e_c = jnp.exp2(s_c - m)
    e_n = jnp.exp2(s_n - m)
    denom = jnp.sum(e_c, axis=-1, keepdims=True) + jnp.sum(e_n, axis=-1, keepdims=True) + jnp.exp2(sink - m)
    acc = _dot(e_n.astype(BF16), vn.astype(BF16))
    for b in range(nseq):
        acc = acc + _dot(jnp.where(seq_c == b, e_c, 0.0).astype(BF16), vc_ref[b].astype(BF16))
    o = acc / denom
    for hq in range(heads):
        kvh = hq // group
        o_ref[:, hq * hd:(hq + 1) * hd] = o[hq * rows:(hq + 1) * rows, kvh * hd:(kvh + 1) * hd].astype(o_ref.dtype)
    for b in range(nseq):
        kw_ref[b, 0:win - seq, :] = kc_ref[b, seq:win, :]
        kw_ref[b, win - seq:win, :] = kn[b * seq:(b + 1) * seq, :]
        vw_ref[b, 0:win - seq, :] = vc_ref[b, seq:win, :]
        vw_ref[b, win - seq:win, :] = vn[b * seq:(b + 1) * seq, :]


def _attn_sample(sinks, q_wide, k_new, v_new, k_cache, v_cache, *, seq, q_start):
    n, qww = q_wide.shape
    kw = k_new.shape[1]
    heads = qww // kw
    hd = SWA_HEAD_DIM
    group = heads // (kw // hd)
    batch, win, _ = k_cache.shape
    nseq = ATTN_SAMPLE_SEQS if batch % ATTN_SAMPLE_SEQS == 0 else batch
    rows = nseq * seq
    sink_rows = jnp.repeat(sinks * LOG2E, rows)[:, None]
    row = lambda i: (i, 0)
    cache = lambda i: (i, 0, 0)
    vmem = (8 * _nbytes((nseq, win, kw), F32) + 4 * _nbytes((rows, qww), BF16)
            + 16 * _nbytes((heads * rows, win + kw), F32))
    return pl.pallas_call(
        functools.partial(_attn_sample_body, seq=seq, group=group, q_start=q_start),
        grid=(batch // nseq,),
        in_specs=[_resident((heads * rows, 1)), pl.BlockSpec((rows, qww), row),
                  pl.BlockSpec((rows, kw), row), pl.BlockSpec((rows, kw), row),
                  pl.BlockSpec((nseq, win, kw), cache), pl.BlockSpec((nseq, win, kw), cache)],
        out_specs=[pl.BlockSpec((rows, heads * hd), row), pl.BlockSpec((nseq, win, kw), cache),
                   pl.BlockSpec((nseq, win, kw), cache)],
        out_shape=[jax.ShapeDtypeStruct((n, heads * hd), BF16), jax.ShapeDtypeStruct((batch, win, kw), F32),
                   jax.ShapeDtypeStruct((batch, win, kw), F32)],
        compiler_params=_params(("parallel",), vmem),
        name="attn_sample",
    )(sink_rows, q_wide, k_new, v_new, k_cache, v_cache)


def _ret_rope_tables(pos, dk):
    inv = 1.0 / (RET_ROPE_THETA ** jnp.linspace(0.0, 1.0, dk // 2, dtype=F32))
    ang = pos[:, None] * inv[None, :]
    return jnp.cos(ang), jnp.sin(ang)


def _ret_key_scale(log_g, seq, n, dk):
    chunk = _ret_chunk(seq)
    tm = min(TOKEN_TILE, n)
    assert tm % chunk == 0
    left = (chunk - 1 - jnp.arange(tm) % chunk).astype(F32)
    per_head = jnp.exp(log_g[None, :] * left[:, None]) * dk ** -0.5
    return jnp.repeat(per_head, dk, axis=1)


def _swa_cos_sin(pos):
    half = ROT_DIM // 2
    inv = ROPE_THETA ** (-jnp.arange(half, dtype=F32) / half)
    ang = pos[:, None] * inv[None, :]
    return jnp.cos(ang), jnp.sin(ang)


def _swa_rope_tables(pos):
    half = ROT_DIM // 2
    cos, sin = _swa_cos_sin(pos)
    n = pos.shape[0]
    pad = jnp.zeros((n, SWA_HEAD_DIM - 2 * half), F32)
    c_head = jnp.concatenate([cos, cos, pad + 1.0], axis=1)
    sa_head = jnp.concatenate([-sin, jnp.zeros_like(sin), pad], axis=1)
    sb_head = jnp.concatenate([jnp.zeros_like(sin), sin, pad], axis=1)
    reps = LANES // SWA_HEAD_DIM
    return tuple(jnp.tile(t, (1, reps)) for t in (c_head, sa_head, sb_head))


def _tile_rows(tab, seq, n):
    tm = min(TOKEN_TILE, n)
    return tab if seq >= tm else jnp.tile(tab, (tm // seq, 1))


def kernel(x_prompt, x_sample, state_ret, cache_k_win, cache_v_win, ret_norm_pre, ret_w_in, ret_w_out, ret_norm_post, kv_norm, w_kv, swa_norm_pre, swa_w_q, swa_sinks, swa_w_o, swa_norm_post, ffn_norm_pre, ffn_w1, ffn_w2, ffn_norm_post):
    n_a = DEPTH // 2
    assert n_a == 1 and DEPTH == 2, "one retention layer followed by one sliding-window layer"
    d = x_prompt.shape[-1]
    heads = RET_HEADS
    dk = ret_w_out.shape[-1] // heads
    dv = ret_w_out.shape[-2] // heads
    kvh, hd = SWA_KV_HEADS, SWA_HEAD_DIM
    row2 = lambda g: g.reshape(1, d)
    log_g = jnp.log1p(-jnp.exp2(-5.0 - jnp.arange(heads, dtype=F32)))

    w_in = ret_w_in[0].astype(BF16)
    w_out = ret_w_out[0].astype(BF16)
    wq = swa_w_q[0].astype(BF16)
    wkv = w_kv.astype(BF16)
    wo = swa_w_o[0].astype(BF16)
    w1 = [ffn_w1[l].astype(BF16) for l in range(DEPTH)]
    w2 = [ffn_w2[l].astype(BF16) for l in range(DEPTH)]
    sinks = swa_sinks[0]
    q_heads = wq.shape[1] // hd
    on_kv_head = (jnp.arange(q_heads)[:, None] // (q_heads // kvh) == jnp.arange(kvh)[None, :]).astype(BF16)
    wq_wide = (wq.reshape(d, q_heads, 1, hd) * on_kv_head[None, :, :, None]).reshape(d, q_heads * kvh * hd)

    def trunk(x, pos, ret_mixer, swa_mixer):
        b, t, _ = x.shape
        n = b * t
        h = x.reshape(n, d)
        cos, sin = (_tile_rows(tab, t, n) for tab in _ret_rope_tables(pos, dk))
        o, o_layout, state = ret_mixer(h, cos, sin, _ret_key_scale(log_g, t, n, dk))
        h = _out_ffn(o, h, w_out, row2(ret_norm_post[0]), row2(ffn_norm_pre[0]), w1[0], w2[0], row2(ffn_norm_post[0]),
                     o_layout=o_layout)
        o, o_layout, k_win, v_win = swa_mixer(h, pos, b, t)
        h = _out_ffn(o, h, wo, row2(swa_norm_post[0]), row2(ffn_norm_pre[1]), w1[1], w2[1], row2(ffn_norm_post[1]),
                     o_layout=o_layout)
        return h.reshape(b, t, d), state, k_win, v_win

    b_p, t_p, _ = x_prompt.shape
    w_p = min(WINDOW, t_p)

    def swa_prompt(h, pos, b, t):
        cos, sin = _swa_cos_sin(pos)
        qt, k, vt, k_win, v_win = _swa_in_t(h, row2(swa_norm_pre[0]), row2(kv_norm), wq, wkv, cos.T, sin.T,
                                            *_swa_rope_tables(pos), batch=b, win=w_p)
        return _attn_prompt(sinks, qt, k, vt), "features", k_win, v_win

    def ret_prompt(h, cos, sin, kscale):
        o, state = _ret_prompt(log_g, h, row2(ret_norm_pre[0]), w_in, cos, sin, kscale,
                               batch=b_p, heads=heads, dk=dk, dv=dv)
        return o, "tokens", state

    y_prompt, state_p, k_win_p, v_win_p = trunk(x_prompt, jnp.arange(t_p, dtype=F32), ret_prompt, swa_prompt)

    b_s, t_s, _ = x_sample.shape
    w_s = cache_k_win.shape[1]
    kc = cache_k_win.reshape(b_s, w_s, kvh * hd)
    vc = cache_v_win.reshape(b_s, w_s, kvh * hd)

    def swa_sample(h, pos, b, t):
        tabs = tuple(_tile_rows(tab, t, b * t) for tab in _swa_rope_tables(pos))
        q, k, v = _swa_in(h, row2(swa_norm_pre[0]), row2(kv_norm), wq_wide, wkv, *tabs, q_dtype=BF16)
        o, k_win, v_win = _attn_sample(sinks, q, k, v, kc, vc, seq=t, q_start=PAST_LEN)
        return o, "tokens", k_win, v_win

    def ret_sample(h, cos, sin, kscale):
        q, kd, v, sg = _ret_in(h, row2(ret_norm_pre[0]), w_in, cos, sin, kscale,
                               heads=heads, dk=dk, dv=dv, out_dtype=F32)
        o, state = _ret_sample(log_g, q, kd, v, sg, state_ret[0], seq=t_s)
        return o, "heads", state

    y_sample, state_s, k_win_s, v_win_s = trunk(x_sample, PAST_LEN + jnp.arange(t_s, dtype=F32), ret_sample, swa_sample)

    return (y_prompt, y_sample, state_p[None], state_s[None],
            k_win_p.reshape(b_p, w_p, kvh, hd), v_win_p.reshape(b_p, w_p, kvh, hd),
            k_win_s.reshape(b_s, w_s, kvh, hd), v_win_s.reshape(b_s, w_s, kvh, hd))
```

```python
import functools

import jax
import jax.numpy as jnp
from jax import lax
from jax.experimental import pallas as pl
from jax.experimental.pallas import tpu as pltpu

DEPTH = 2
PAST_LEN = 16384
RET_HEADS = 4
RET_ROPE_THETA = 10000.0
SWA_HEAD_DIM = 64
SWA_KV_HEADS = 4
WINDOW = 128
ROPE_THETA = 500000.0
ROT_DIM = SWA_HEAD_DIM // 4
EPS = 1e-6
NEG = -1e30
LOG2E = 1.4426950408889634

LANES = 128
SUBLANES = 8
BF16_SUBLANES = 16
VMEM_CAP_BYTES = 64 * 1024 * 1024
VMEM_BUDGET_BYTES = VMEM_CAP_BYTES - 8 * 1024 * 1024

TOKEN_TILE = 512
COL_CHUNK = 1024
RET_KERNEL_CHUNK = 256
SAMPLE_GROUP = 2
ATTN_SAMPLE_SEQS = 4
ATTN_BLOCKS_PER_STEP = 8
ATTN_LOOKAHEAD = 4
OUT_FFN_ROW_PARTS = 2
SWA_TOKEN_TILE = 1024
SWA_ROW_PARTS = 4

F32 = jnp.float32
BF16 = jnp.bfloat16


def _params(semantics, vmem_bytes):
    limit = int(min(max(vmem_bytes, 16 * 1024 * 1024), VMEM_BUDGET_BYTES))
    return pltpu.CompilerParams(dimension_semantics=semantics, vmem_limit_bytes=limit)


def _resident(shape):
    nd = len(shape)
    return pl.BlockSpec(shape, lambda *_: (0,) * nd, pipeline_mode=pl.Buffered(1))


def _nbytes(shape, dtype):
    n = 1
    for s in shape:
        n *= s
    return n * jnp.dtype(dtype).itemsize


def _rms_rows(x):
    return x * lax.rsqrt(jnp.mean(x * x, axis=-1, keepdims=True) + EPS)


def _dot(a, b):
    return jnp.dot(a, b, preferred_element_type=F32)


def _dot_nt(a, b):
    return lax.dot_general(a, b, (((1,), (1,)), ((), ())), preferred_element_type=F32)


def _dot_tn(a, b):
    return lax.dot_general(a, b, (((0,), (0,)), ((), ())), preferred_element_type=F32)


def _ret_in_body(h_ref, g_ref, w_ref, cos_ref, sin_ref, kscale_ref, q_ref, kd_ref, v_ref, sg_ref, *, heads, dk, dv,
                 rows=slice(None)):
    xn = (_rms_rows(h_ref[rows, :]) * g_ref[...]).astype(BF16)
    cos = cos_ref[rows, :]
    sin = sin_ref[rows, :]
    half = dk // 2
    qk_w = heads * dk
    v_w = heads * dv

    def proj(lo, width):
        return _dot(xn, w_ref[:, lo:lo + width])

    for base, ref, scale_ref in ((0, q_ref, None), (qk_w, kd_ref, kscale_ref)):
        p = proj(base, qk_w)
        for hh in range(heads):
            lo, mid, hi = hh * dk, hh * dk + half, (hh + 1) * dk
            x1 = p[:, lo:mid]
            x2 = p[:, mid:hi]
            o1 = x1 * cos - x2 * sin
            o2 = x2 * cos + x1 * sin
            if scale_ref is not None:
                o1 = o1 * scale_ref[rows, lo:mid]
                o2 = o2 * scale_ref[rows, mid:hi]
            ref[hh, rows, :half] = o1.astype(ref.dtype)
            ref[hh, rows, half:] = o2.astype(ref.dtype)
    cw = min(COL_CHUNK, v_w)
    per_chunk = cw // dv
    for c in range(v_w // cw):
        v = proj(2 * qk_w + c * cw, cw)
        for j in range(per_chunk):
            v_ref[c * per_chunk + j, rows, :] = v[:, j * dv:(j + 1) * dv].astype(v_ref.dtype)
    for c in range(v_w // cw):
        gate = proj(2 * qk_w + v_w + c * cw, cw)
        sg = gate * jax.nn.sigmoid(gate)
        for j in range(per_chunk):
            sg_ref[c * per_chunk + j, rows, :] = sg[:, j * dv:(j + 1) * dv].astype(sg_ref.dtype)


def _ret_in(h, g, w_in, cos, sin, kscale, *, heads, dk, dv, out_dtype):
    n, d = h.shape
    tm = min(TOKEN_TILE, n)
    qk_w, v_w = heads * dk, heads * dv
    pos_tiles = cos.shape[0] // tm
    row = lambda i: (i, 0)
    tab = lambda i: (i % pos_tiles, 0)
    by_head = lambda i: (0, i, 0)
    vmem = (2 * _nbytes((tm, d), F32) + _nbytes(w_in.shape, BF16) + 4 * _nbytes((tm, dk // 2), F32)
            + _nbytes((tm, qk_w), F32) + 2 * _nbytes((tm, 2 * qk_w + 2 * v_w), out_dtype)
            + 4 * _nbytes((tm, COL_CHUNK), F32))
    return pl.pallas_call(
        functools.partial(_ret_in_body, heads=heads, dk=dk, dv=dv),
        grid=(n // tm,),
        in_specs=[pl.BlockSpec((tm, d), row), _resident((1, d)), _resident(w_in.shape),
                  pl.BlockSpec((tm, dk // 2), tab), pl.BlockSpec((tm, dk // 2), tab), _resident((tm, qk_w))],
        out_specs=[pl.BlockSpec((heads, tm, dk), by_head), pl.BlockSpec((heads, tm, dk), by_head),
                   pl.BlockSpec((heads, tm, dv), by_head), pl.BlockSpec((heads, tm, dv), by_head)],
        out_shape=[jax.ShapeDtypeStruct((heads, n, dk), out_dtype), jax.ShapeDtypeStruct((heads, n, dk), out_dtype),
                   jax.ShapeDtypeStruct((heads, n, dv), out_dtype), jax.ShapeDtypeStruct((heads, n, dv), out_dtype)],
        compiler_params=_params(("parallel",), vmem),
        name="ret_in",
    )(h, g, w_in, cos, sin, kscale)


def _ret_prompt_body(lg_ref, h_ref, g_ref, w_ref, cos_ref, sin_ref, kscale_ref, o_ref, s_out_ref,
                     q_s, kd_s, v_s, sg_s, s_ref, *, heads, dk, dv, chunk):
    t = pl.program_id(1)

    @pl.when(t == 0)
    def _():
        s_ref[...] = jnp.zeros_like(s_ref)

    tm = h_ref.shape[0]
    for c in range(tm // chunk):
        _ret_in_body(h_ref, g_ref, w_ref, cos_ref, sin_ref, kscale_ref, q_s, kd_s, v_s, sg_s,
                     heads=heads, dk=dk, dv=dv, rows=pl.ds(c * chunk, chunk))

    ri = lax.broadcasted_iota(jnp.int32, (chunk, chunk), 0)
    ci = lax.broadcasted_iota(jnp.int32, (chunk, chunk), 1)
    lower = (ri >= ci).astype(F32)
    row_v = lax.broadcasted_iota(jnp.int32, (chunk, dv), 0).astype(F32)
    causal, q_decay, chunk_decay = [], [], []
    for hh in range(heads):
        lg = lg_ref[hh]
        causal.append(lower * jnp.exp(jnp.full((1, chunk), -lg * chunk, F32)))
        q_decay.append(jnp.exp(lg * (row_v + 1.0)))
        chunk_decay.append(jnp.exp(jnp.full((1, dv), lg * chunk, F32)))

    for c in range(tm // chunk):
        rows = pl.ds(c * chunk, chunk)
        qk = [_dot_nt(q_s[hh, rows, :], kd_s[hh, rows, :]) for hh in range(heads)]
        grow = [_dot_tn(kd_s[hh, rows, :], v_s[hh, rows, :]) for hh in range(heads)]
        for hh in range(heads):
            s_prev = s_ref[hh]
            lhs = jnp.concatenate([(qk[hh] * causal[hh]).astype(BF16), q_s[hh, rows, :]], axis=1)
            rhs = jnp.concatenate([v_s[hh, rows, :], s_prev.astype(BF16)], axis=0)
            o = _rms_rows(q_decay[hh] * _dot(lhs, rhs))
            s_ref[hh] = chunk_decay[hh] * s_prev + grow[hh]
            o_ref[rows, hh * dv:(hh + 1) * dv] = (o * sg_s[hh, rows, :].astype(F32)).astype(o_ref.dtype)

    @pl.when(t == pl.num_programs(1) - 1)
    def _():
        s_out_ref[0] = s_ref[...]


def _ret_chunk(seq):
    return RET_KERNEL_CHUNK if seq % RET_KERNEL_CHUNK == 0 else seq


def _ret_prompt(log_g, h, g, w_in, cos, sin, kscale, *, batch, heads, dk, dv):
    n, d = h.shape
    seq = n // batch
    tm = min(TOKEN_TILE, seq)
    chunk = _ret_chunk(seq)
    assert tm % chunk == 0 and seq % tm == 0
    nt = seq // tm
    qk_w, v_w = heads * dk, heads * dv
    row = lambda b, t: (b * nt + t, 0)
    tab = lambda b, t: (t, 0)
    vmem = (2 * _nbytes((tm, d), F32) + _nbytes(w_in.shape, BF16) + 4 * _nbytes((tm, dk // 2), F32)
            + _nbytes((tm, qk_w), F32) + 2 * _nbytes((tm, v_w), BF16) + _nbytes((tm, 2 * qk_w + 2 * v_w), BF16)
            + 3 * _nbytes((heads, dk, dv), F32) + 4 * _nbytes((tm, COL_CHUNK), F32)
            + 2 * heads * (_nbytes((chunk, chunk), F32) + _nbytes((dk, dv), F32) + _nbytes((chunk, dv), F32)))
    return pl.pallas_call(
        functools.partial(_ret_prompt_body, heads=heads, dk=dk, dv=dv, chunk=chunk),
        grid=(batch, nt),
        in_specs=[pl.BlockSpec(memory_space=pltpu.SMEM),
                  pl.BlockSpec((tm, d), row), _resident((1, d)), _resident(w_in.shape),
                  pl.BlockSpec((tm, dk // 2), tab), pl.BlockSpec((tm, dk // 2), tab), _resident((tm, qk_w))],
        out_specs=[pl.BlockSpec((tm, v_w), row),
                   pl.BlockSpec((1, heads, dk, dv), lambda b, t: (b, 0, 0, 0))],
        out_shape=[jax.ShapeDtypeStruct((n, v_w), BF16),
                   jax.ShapeDtypeStruct((batch, heads, dk, dv), F32)],
        scratch_shapes=[pltpu.VMEM((heads, tm, dk), BF16), pltpu.VMEM((heads, tm, dk), BF16),
                        pltpu.VMEM((heads, tm, dv), BF16), pltpu.VMEM((heads, tm, dv), BF16),
                        pltpu.VMEM((heads, dk, dv), F32)],
        compiler_params=_params(("parallel", "arbitrary"), vmem),
        name="ret_prompt",
    )(log_g, h, g, w_in, cos, sin, kscale)


def _ret_sample_body(lg_ref, q_ref, kd_ref, v_ref, sg_ref, s_in_ref, o_ref, s_out_ref, *, seq):
    heads, rows, dk = q_ref.shape
    dv = v_ref.shape[2]
    group = rows // seq
    ri = lax.broadcasted_iota(jnp.int32, (rows, rows), 0)
    ci = lax.broadcasted_iota(jnp.int32, (rows, rows), 1)
    visible = ((ri // seq) == (ci // seq)) & (ri >= ci)
    row_v = lax.broadcasted_iota(jnp.int32, (rows, dv), 0)
    row_k = lax.broadcasted_iota(jnp.int32, (rows, dk), 0)
    for hh in range(heads):
        lg = lg_ref[hh]
        causal = jnp.where(visible, jnp.exp(jnp.full((rows, rows), -lg * seq, F32)), 0.0)
        q_decay = jnp.exp(lg * ((row_v % seq).astype(F32) + 1.0))
        chunk_decay = jnp.exp(jnp.full((1, dv), lg * seq, F32))
        q = q_ref[hh].astype(BF16)
        kd = kd_ref[hh]
        v = v_ref[hh].astype(BF16)
        o = _dot((_dot_nt(q, kd.astype(BF16)) * causal).astype(BF16), v)
        for g in range(group):
            s_prev = s_in_ref[g, hh]
            o = jnp.where((row_v // seq) == g, o + _dot(q, s_prev.astype(BF16)), o)
            kd_g = jnp.where((row_k // seq) == g, kd, 0.0).astype(BF16)
            s_out_ref[g, hh] = chunk_decay * s_prev + _dot_tn(kd_g, v)
        o = _rms_rows(q_decay * o)
        o_ref[hh] = (o * sg_ref[hh]).astype(o_ref.dtype)


def _ret_sample(log_g, q, kd, v, sg, state, *, seq):
    heads, n, dk = q.shape
    dv = v.shape[2]
    batch = n // seq
    group = SAMPLE_GROUP if batch % SAMPLE_GROUP == 0 else batch
    rows = group * seq
    by_head = lambda i: (0, i, 0)
    st = lambda i: (i, 0, 0, 0)
    vmem = (4 * _nbytes((group, heads, dk, dv), F32) + 8 * _nbytes((rows, heads * dv), F32)
            + 4 * _nbytes((dk, dv), F32))
    return pl.pallas_call(
        functools.partial(_ret_sample_body, seq=seq),
        grid=(batch // group,),
        in_specs=[pl.BlockSpec(memory_space=pltpu.SMEM),
                  pl.BlockSpec((heads, rows, dk), by_head), pl.BlockSpec((heads, rows, dk), by_head),
                  pl.BlockSpec((heads, rows, dv), by_head), pl.BlockSpec((heads, rows, dv), by_head),
                  pl.BlockSpec((group, heads, dk, dv), st)],
        out_specs=[pl.BlockSpec((heads, rows, dv), by_head), pl.BlockSpec((group, heads, dk, dv), st)],
        out_shape=[jax.ShapeDtypeStruct((heads, n, dv), F32),
                   jax.ShapeDtypeStruct((batch, heads, dk, dv), F32)],
        compiler_params=_params(("parallel",), vmem),
        name="ret_sample",
    )(log_g, q, kd, v, sg, state)


def _out_ffn_body(o_ref, h_ref, wo_ref, g_post_ref, g_pre_ref, w1_ref, w2_ref, g_ffn_ref, y_ref, *, o_layout):
    tm = h_ref.shape[0]
    parts = OUT_FFN_ROW_PARTS if tm % (OUT_FFN_ROW_PARTS * BF16_SUBLANES) == 0 else 1
    rp = tm // parts
    d_ff = w1_ref.shape[1]
    fc = min(COL_CHUNK, d_ff)

    def mixer_out(p):
        rows = slice(p * rp, (p + 1) * rp)
        if o_layout == "features":
            return _dot_tn(o_ref[0, :, rows], wo_ref[...])
        if o_layout == "heads":
            o = jnp.concatenate([o_ref[hh, rows, :] for hh in range(o_ref.shape[0])], axis=1)
            return _dot(o.astype(BF16), wo_ref[...])
        return _dot(o_ref[rows, :].astype(BF16), wo_ref[...])

    a = [mixer_out(p) for p in range(parts)]
    h1, x = [], []
    for p in range(parts):
        rows = slice(p * rp, (p + 1) * rp)
        h1.append(h_ref[rows, :] + _rms_rows(a[p]) * g_post_ref[...])
        x.append((_rms_rows(h1[p]) * g_pre_ref[...]).astype(BF16))
    acc = [jnp.zeros((rp, h_ref.shape[1]), F32) for _ in range(parts)]
    for c in range(d_ff // fc):
        for p in range(parts):
            u = jnp.maximum(_dot(x[p], w1_ref[:, c * fc:(c + 1) * fc]), 0.0)
            acc[p] = acc[p] + _dot((u * u).astype(BF16), w2_ref[c * fc:(c + 1) * fc, :])
    for p in range(parts):
        rows = slice(p * rp, (p + 1) * rp)
        y_ref[rows, :] = h1[p] + _rms_rows(acc[p]) * g_ffn_ref[...]


def _out_ffn(o, h, w_o, g_post, g_pre, w1, w2, g_ffn, *, o_layout):
    n, d = h.shape
    kdim = w_o.shape[0]
    tm = min(TOKEN_TILE, n)
    row = lambda i: (i, 0)
    if o_layout == "features":
        tiles = o.shape[2] // tm
        o_spec = pl.BlockSpec((1, kdim, tm), lambda i: (i // tiles, 0, i % tiles))
    elif o_layout == "heads":
        o_spec = pl.BlockSpec((o.shape[0], tm, o.shape[2]), lambda i: (0, i, 0))
    else:
        o_spec = pl.BlockSpec((tm, kdim), row)
    vmem = (2 * _nbytes((tm, kdim), o.dtype) + 4 * _nbytes((tm, d), F32) + _nbytes(w_o.shape, BF16)
            + _nbytes(w1.shape, BF16) + _nbytes(w2.shape, BF16) + 6 * _nbytes((tm, COL_CHUNK), F32))
    return pl.pallas_call(
        functools.partial(_out_ffn_body, o_layout=o_layout),
        grid=(n // tm,),
        in_specs=[o_spec, pl.BlockSpec((tm, d), row), _resident(w_o.shape),
                  _resident((1, d)), _resident((1, d)), _resident(w1.shape), _resident(w2.shape),
                  _resident((1, d))],
        out_specs=pl.BlockSpec((tm, d), row),
        out_shape=jax.ShapeDtypeStruct((n, d), F32),
        compiler_params=_params(("parallel",), vmem),
        name="out_ffn",
    )(o, h, w_o, g_post, g_pre, w1, w2, g_ffn)


def _partial_rope(x, c_tab, sa_tab, sb_tab):
    half = ROT_DIM // 2
    outs = []
    for j in range(x.shape[1] // LANES):
        s = x[:, j * LANES:(j + 1) * LANES]
        outs.append(s * c_tab + pltpu.roll(s, LANES - half, axis=1) * sa_tab + pltpu.roll(s, half, axis=1) * sb_tab)
    return outs


def _swa_in_body(h_ref, g_q_ref, g_kv_ref, wq_ref, wkv_ref, c_ref, sa_ref, sb_ref, q_ref, k_ref, v_ref):
    y = _rms_rows(h_ref[...])
    xq = (y * g_q_ref[...]).astype(BF16)
    xkv = (y * g_kv_ref[...]).astype(BF16)
    c_tab, sa_tab, sb_tab = c_ref[...], sa_ref[...], sb_ref[...]
    q = _dot(xq, wq_ref[...]) * (SWA_HEAD_DIM ** -0.5 * LOG2E)
    for j, s in enumerate(_partial_rope(q, c_tab, sa_tab, sb_tab)):
        q_ref[:, j * LANES:(j + 1) * LANES] = s.astype(q_ref.dtype)
    kv = _dot(xkv, wkv_ref[...])
    kw = k_ref.shape[1]
    for j, s in enumerate(_partial_rope(kv[:, :kw], c_tab, sa_tab, sb_tab)):
        k_ref[:, j * LANES:(j + 1) * LANES] = s
    v_ref[...] = kv[:, kw:]


def _swa_in(h, g_q, g_kv, w_q, w_kv, c_tab, sa_tab, sb_tab, *, q_dtype):
    n, d = h.shape
    tm = min(TOKEN_TILE, n)
    qw = w_q.shape[1]
    kw = w_kv.shape[1] // 2
    pos_tiles = c_tab.shape[0] // tm
    row = lambda i: (i, 0)
    tab = lambda i: (i % pos_tiles, 0)
    vmem = (2 * _nbytes((tm, d), F32) + _nbytes(w_q.shape, BF16) + _nbytes(w_kv.shape, BF16)
            + 6 * _nbytes((tm, LANES), F32) + 2 * _nbytes((tm, qw), q_dtype) + 4 * _nbytes((tm, kw), F32)
            + 6 * _nbytes((tm, qw), F32))
    return pl.pallas_call(
        _swa_in_body,
        grid=(n // tm,),
        in_specs=[pl.BlockSpec((tm, d), row), _resident((1, d)), _resident((1, d)),
                  _resident(w_q.shape), _resident(w_kv.shape),
                  pl.BlockSpec((tm, LANES), tab), pl.BlockSpec((tm, LANES), tab), pl.BlockSpec((tm, LANES), tab)],
        out_specs=[pl.BlockSpec((tm, qw), row), pl.BlockSpec((tm, kw), row), pl.BlockSpec((tm, kw), row)],
        out_shape=[jax.ShapeDtypeStruct((n, qw), q_dtype), jax.ShapeDtypeStruct((n, kw), F32),
                   jax.ShapeDtypeStruct((n, kw), F32)],
        compiler_params=_params(("parallel",), vmem),
        name="swa_in",
    )(h, g_q, g_kv, w_q, w_kv, c_tab, sa_tab, sb_tab)


def _swa_in_t_body(h_ref, g_q_ref, g_kv_ref, wqt_ref, wk_ref, wvt_ref, wv_ref, cos_t_ref, sin_t_ref,
                   c_ref, sa_ref, sb_ref, qt_ref, k_ref, vt_ref, kwin_ref, vwin_ref, *, tiles):
    tm = h_ref.shape[0]
    hd = SWA_HEAD_DIM
    half = ROT_DIM // 2
    win = kwin_ref.shape[0]
    parts = SWA_ROW_PARTS if tm % (SWA_ROW_PARTS * LANES) == 0 else 1
    rp = tm // parts
    assert rp >= win
    for p in range(parts):
        rows = slice(p * rp, (p + 1) * rp)
        y = _rms_rows(h_ref[rows, :])
        xq = (y * g_q_ref[...]).astype(BF16)
        xkv = (y * g_kv_ref[...]).astype(BF16)
        cos_t, sin_t = cos_t_ref[:, rows], sin_t_ref[:, rows]
        qt = _dot_nt(wqt_ref[...], xq) * (hd ** -0.5 * LOG2E)
        for hq in range(qt.shape[0] // hd):
            base = hq * hd
            x1 = qt[base:base + half]
            x2 = qt[base + half:base + 2 * half]
            rot = jnp.concatenate([x1 * cos_t - x2 * sin_t, x2 * cos_t + x1 * sin_t], axis=0)
            qt_ref[0, base:base + 2 * half, rows] = rot.astype(qt_ref.dtype)
            qt_ref[0, base + 2 * half:base + hd, rows] = qt[base + 2 * half:base + hd].astype(qt_ref.dtype)
        k_rot = _partial_rope(_dot(xkv, wk_ref[...]), c_ref[rows, :], sa_ref[rows, :], sb_ref[rows, :])
        for j, s in enumerate(k_rot):
            k_ref[rows, j * LANES:(j + 1) * LANES] = s.astype(k_ref.dtype)
        vt_ref[0, :, rows] = _dot_nt(wvt_ref[...], xkv).astype(vt_ref.dtype)
        if p == parts - 1:
            @pl.when(pl.program_id(0) % tiles == tiles - 1)
            def _():
                for j, s in enumerate(k_rot):
                    kwin_ref[:, j * LANES:(j + 1) * LANES] = s[rp - win:, :]
                vwin_ref[...] = _dot(xkv[rp - win:, :], wv_ref[...])


def _swa_in_t(h, g_q, g_kv, w_q, w_kv, cos_t, sin_t, c_tab, sa_tab, sb_tab, *, batch, win):
    n, d = h.shape
    seq = n // batch
    tm = min(SWA_TOKEN_TILE, seq)
    tiles = seq // tm
    qw = w_q.shape[1]
    kw = w_kv.shape[1] // 2
    wqt = w_q.T
    wk, wv = w_kv[:, :kw], w_kv[:, kw:]
    wvt = wv.T
    row = lambda i: (i, 0)
    tab = lambda i: (i % tiles, 0)
    tab_t = lambda i: (0, i % tiles)
    feat = lambda i: (i // tiles, 0, i % tiles)
    per_seq = lambda i: (i // tiles, 0)
    half = ROT_DIM // 2
    vmem = (2 * _nbytes((tm, d), F32) + 2 * _nbytes(w_q.shape, BF16) + 3 * _nbytes(w_kv.shape, BF16)
            + 8 * _nbytes((tm, LANES), F32) + 2 * _nbytes((tm, qw + 2 * kw), BF16) + 4 * _nbytes((win, kw), F32)
            + 4 * _nbytes((tm, qw), F32))
    return pl.pallas_call(
        functools.partial(_swa_in_t_body, tiles=tiles),
        grid=(n // tm,),
        in_specs=[pl.BlockSpec((tm, d), row), _resident((1, d)), _resident((1, d)),
                  _resident(wqt.shape), _resident(wk.shape), _resident(wvt.shape), _resident(wv.shape),
                  pl.BlockSpec((half, tm), tab_t), pl.BlockSpec((half, tm), tab_t),
                  pl.BlockSpec((tm, LANES), tab), pl.BlockSpec((tm, LANES), tab), pl.BlockSpec((tm, LANES), tab)],
        out_specs=[pl.BlockSpec((1, qw, tm), feat), pl.BlockSpec((tm, kw), row), pl.BlockSpec((1, kw, tm), feat),
                   pl.BlockSpec((win, kw), per_seq), pl.BlockSpec((win, kw), per_seq)],
        out_shape=[jax.ShapeDtypeStruct((batch, qw, seq), BF16), jax.ShapeDtypeStruct((n, kw), BF16),
                   jax.ShapeDtypeStruct((batch, kw, seq), BF16),
                   jax.ShapeDtypeStruct((batch * win, kw), F32), jax.ShapeDtypeStruct((batch * win, kw), F32)],
        compiler_params=_params(("arbitrary",), vmem),
        name="swa_in_t",
    )(h, g_q, g_kv, wqt, wk, wvt, wv, cos_t, sin_t, c_tab, sa_tab, sb_tab)


def _attn_prompt_body(sinks_ref, mask_ref, qt_ref, kp_ref, kc_ref, vtp_ref, vtc_ref, ot_ref, *, group):
    blk = kp_ref.shape[0]
    nblk = kc_ref.shape[0] // blk
    hd = SWA_HEAD_DIM
    kvh_n = kc_ref.shape[1] // hd
    cols = group * blk
    kj = lax.broadcasted_iota(jnp.int32, (blk, cols), 0)
    qi = lax.broadcasted_iota(jnp.int32, (blk, cols), 1) % blk
    own = kj <= qi
    lane_head = lax.broadcasted_iota(jnp.int32, (1, cols), 1) // blk
    ones_rows = jnp.ones((BF16_SUBLANES, blk), BF16)
    has_prev = pl.program_id(1) > 0

    def scores(j, kvh):
        tile, lo = divmod(kvh * hd, LANES)
        q4t = jnp.concatenate([qt_ref[0, (kvh * group + g) * hd:(kvh * group + g + 1) * hd, j * blk:(j + 1) * blk]
                               for g in range(group)], axis=1)
        rhs = jnp.concatenate([q4t if part * hd == lo else jnp.zeros_like(q4t) for part in range(LANES // hd)], axis=0)
        k_tile = slice(tile * LANES, (tile + 1) * LANES)
        k_prev = kp_ref[:, k_tile] if j == 0 else kc_ref[(j - 1) * blk:j * blk, k_tile]
        return _dot(kc_ref[j * blk:(j + 1) * blk, k_tile], rhs), _dot(k_prev, rhs)

    def finish(j, kvh, s_own, s_prev):
        if j == 0:
            s_prev = jnp.where(has_prev, s_prev, NEG)
        s = jnp.where(own, s_own, s_prev)
        sink = jnp.full((1, cols), sinks_ref[kvh * group] * LOG2E, F32)
        for g in range(1, group):
            sink = jnp.where(lane_head == g, sinks_ref[kvh * group + g] * LOG2E, sink)
        m = jnp.maximum(jnp.max(s, axis=0, keepdims=True), sink)
        e = jnp.exp2(s - m).astype(BF16)
        p_own = e * mask_ref[...]
        p = jnp.concatenate([p_own, e - p_own], axis=0)
        head_rows = slice(kvh * hd, (kvh + 1) * hd)
        vt_own = vtc_ref[0, head_rows, j * blk:(j + 1) * blk]
        vt_prev = vtp_ref[0, head_rows, :] if j == 0 else vtc_ref[0, head_rows, (j - 1) * blk:j * blk]
        vt = jnp.concatenate([jnp.concatenate([vt_own, ones_rows], axis=0),
                              jnp.concatenate([vt_prev, ones_rows], axis=0)], axis=1)
        acc = _dot(vt, p)
        denom = acc[hd:hd + 1, :] + jnp.exp2(sink - m)
        ot = acc[:hd] / denom
        for g in range(group):
            hq = kvh * group + g
            ot_ref[0, hq * hd:(hq + 1) * hd, j * blk:(j + 1) * blk] = ot[:, g * blk:(g + 1) * blk].astype(ot_ref.dtype)

    units = [(j, kvh) for j in range(nblk) for kvh in range(kvh_n)]
    queue = [scores(*unit) for unit in units[:ATTN_LOOKAHEAD]]
    for idx, unit in enumerate(units):
        if idx + ATTN_LOOKAHEAD < len(units):
            queue.append(scores(*units[idx + ATTN_LOOKAHEAD]))
        finish(*unit, *queue.pop(0))


def _attn_prompt(sinks, qt, k, vt):
    batch, qw, seq = qt.shape
    kw = k.shape[1]
    blk = WINDOW
    nblk = ATTN_BLOCKS_PER_STEP if seq % (ATTN_BLOCKS_PER_STEP * blk) == 0 else 1
    span = nblk * blk
    steps = seq // span
    group = qw // kw
    cols = group * blk
    own = (jnp.arange(blk)[:, None] <= (jnp.arange(cols) % blk)[None, :]).astype(BF16)
    cur_t = lambda b, i: (b, 0, i)
    prev_t = lambda b, i: (b, 0, jnp.maximum(i * nblk - 1, 0))
    cur = lambda b, i: (b * steps + i, 0)
    prev = lambda b, i: (b * steps * nblk + jnp.maximum(i * nblk - 1, 0), 0)
    vmem = (4 * _nbytes((qw, span), BF16) + 6 * _nbytes((span, kw), BF16) + 16 * _nbytes((blk, cols), F32))
    return pl.pallas_call(
        functools.partial(_attn_prompt_body, group=group),
        grid=(batch, steps),
        in_specs=[pl.BlockSpec(memory_space=pltpu.SMEM), _resident((blk, cols)), pl.BlockSpec((1, qw, span), cur_t),
                  pl.BlockSpec((blk, kw), prev), pl.BlockSpec((span, kw), cur),
                  pl.BlockSpec((1, kw, blk), prev_t), pl.BlockSpec((1, kw, span), cur_t)],
        out_specs=pl.BlockSpec((1, qw, span), cur_t),
        out_shape=jax.ShapeDtypeStruct((batch, qw, seq), BF16),
        compiler_params=_params(("parallel", "parallel"), vmem),
        name="attn_prompt",
    )(sinks, own, qt, k, k, vt, vt)


def _attn_sample_body(sink_ref, q_ref, kn_ref, vn_ref, kc_ref, vc_ref, o_ref, kw_ref, vw_ref, *, seq, group, q_start):
    rows = q_ref.shape[0]
    nseq = rows // seq
    win = kc_ref.shape[1]
    hd = SWA_HEAD_DIM
    kw = kn_ref.shape[1]
    heads = q_ref.shape[1] // kw
    srows = heads * rows
    lhs = jnp.concatenate([q_ref[:, hq * kw:(hq + 1) * kw] for hq in range(heads)], axis=0)
    kn = kn_ref[...]
    vn = vn_ref[...]
    sink = sink_ref[...]

    r_c = lax.broadcasted_iota(jnp.int32, (srows, win), 0) % rows
    c_c = lax.broadcasted_iota(jnp.int32, (srows, win), 1)
    seq_c = r_c // seq
    rel_c = (r_c % seq) + win - c_c
    ok_c = (rel_c >= 0) & (rel_c < WINDOW) & (q_start - win + c_c >= 0)
    s_c = _dot_nt(lhs, kc_ref[0].astype(BF16))
    for b in range(1, nseq):
        s_c = jnp.where(seq_c == b, _dot_nt(lhs, kc_ref[b].astype(BF16)), s_c)
    s_c = jnp.where(ok_c, s_c, NEG)

    r_n = lax.broadcasted_iota(jnp.int32, (srows, rows), 0) % rows
    c_n = lax.broadcasted_iota(jnp.int32, (srows, rows), 1)
    rel_n = (r_n % seq) - (c_n % seq)
    ok_n = (rel_n >= 0) & (rel_n < WINDOW) & ((r_n // seq) == (c_n // seq))
    s_n = jnp.where(ok_n, _dot_nt(lhs, kn.astype(BF16)), NEG)

    m = jnp.maximum(jnp.maximum(jnp.max(s_c, axis=-1, keepdims=True), jnp.max(s_n, axis=-1, keepdims=True)), sink)
    e_c = jnp.exp2(s_c - m)
    e_n = jnp.exp2(s_n - m)
    denom = jnp.sum(e_c, axis=-1, keepdims=True) + jnp.sum(e_n, axis=-1, keepdims=True) + jnp.exp2(sink - m)
    acc = _dot(e_n.astype(BF16), vn.astype(BF16))
    for b in range(nseq):
        acc = acc + _dot(jnp.where(seq_c == b, e_c, 0.0).astype(BF16), vc_ref[b].astype(BF16))
    o = acc / denom
    for hq in range(heads):
        kvh = hq // group
        o_ref[:, hq * hd:(hq + 1) * hd] = o[hq * rows:(hq + 1) * rows, kvh * hd:(kvh + 1) * hd].astype(o_ref.dtype)
    for b in range(nseq):
        kw_ref[b, 0:win - seq, :] = kc_ref[b, seq:win, :]
        kw_ref[b, win - seq:win, :] = kn[b * seq:(b + 1) * seq, :]
        vw_ref[b, 0:win - seq, :] = vc_ref[b, seq:win, :]
        vw_ref[b, win - seq:win, :] = vn[b * seq:(b + 1) * seq, :]


def _attn_sample(sinks, q_wide, k_new, v_new, k_cache, v_cache, *, seq, q_start):
    n, qww = q_wide.shape
    kw = k_new.shape[1]
    heads = qww // kw
    hd = SWA_HEAD_DIM
    group = heads // (kw // hd)
    batch, win, _ = k_cache.shape
    nseq = ATTN_SAMPLE_SEQS if batch % ATTN_SAMPLE_SEQS == 0 else batch
    rows = nseq * seq
    sink_rows = jnp.repeat(sinks * LOG2E, rows)[:, None]
    row = lambda i: (i, 0)
    cache = lambda i: (i, 0, 0)
    vmem = (8 * _nbytes((nseq, win, kw), F32) + 4 * _nbytes((rows, qww), BF16)
            + 16 * _nbytes((heads * rows, win + kw), F32))
    return pl.pallas_call(
        functools.partial(_attn_sample_body, seq=seq, group=group, q_start=q_start),
        grid=(batch // nseq,),
        in_specs=[_resident((heads * rows, 1)), pl.BlockSpec((rows, qww), row),
                  pl.BlockSpec((rows, kw), row), pl.BlockSpec((rows, kw), row),
                  pl.BlockSpec((nseq, win, kw), cache), pl.BlockSpec((nseq, win, kw), cache)],
        out_specs=[pl.BlockSpec((rows, heads * hd), row), pl.BlockSpec((nseq, win, kw), cache),
                   pl.BlockSpec((nseq, win, kw), cache)],
        out_shape=[jax.ShapeDtypeStruct((n, heads * hd), BF16), jax.ShapeDtypeStruct((batch, win, kw), F32),
                   jax.ShapeDtypeStruct((batch, win, kw), F32)],
        compiler_params=_params(("parallel",), vmem),
        name="attn_sample",
    )(sink_rows, q_wide, k_new, v_new, k_cache, v_cache)


def _ret_rope_tables(pos, dk):
    inv = 1.0 / (RET_ROPE_THETA ** jnp.linspace(0.0, 1.0, dk // 2, dtype=F32))
    ang = pos[:, None] * inv[None, :]
    return jnp.cos(ang), jnp.sin(ang)


def _ret_key_scale(log_g, seq, n, dk):
    chunk = _ret_chunk(seq)
    tm = min(TOKEN_TILE, n)
    assert tm % chunk == 0
    left = (chunk - 1 - jnp.arange(tm) % chunk).astype(F32)
    per_head = jnp.exp(log_g[None, :] * left[:, None]) * dk ** -0.5
    return jnp.repeat(per_head, dk, axis=1)


def _swa_cos_sin(pos):
    half = ROT_DIM // 2
    inv = ROPE_THETA ** (-jnp.arange(half, dtype=F32) / half)
    ang = pos[:, None] * inv[None, :]
    return jnp.cos(ang), jnp.sin(ang)


def _swa_rope_tables(pos):
    half = ROT_DIM // 2
    cos, sin = _swa_cos_sin(pos)
    n = pos.shape[0]
    pad = jnp.zeros((n, SWA_HEAD_DIM - 2 * half), F32)
    c_head = jnp.concatenate([cos, cos, pad + 1.0], axis=1)
    sa_head = jnp.concatenate([-sin, jnp.zeros_like(sin), pad], axis=1)
    sb_head = jnp.concatenate([jnp.zeros_like(sin), sin, pad], axis=1)
    reps = LANES // SWA_HEAD_DIM
    return tuple(jnp.tile(t, (1, reps)) for t in (c_head, sa_head, sb_head))


def _tile_rows(tab, seq, n):
    tm = min(TOKEN_TILE, n)
    return tab if seq >= tm else jnp.tile(tab, (tm // seq, 1))


def kernel(x_prompt, x_sample, state_ret, cache_k_win, cache_v_win, ret_norm_pre, ret_w_in, ret_w_out, ret_norm_post, kv_norm, w_kv, swa_norm_pre, swa_w_q, swa_sinks, swa_w_o, swa_norm_post, ffn_norm_pre, ffn_w1, ffn_w2, ffn_norm_post):
    n_a = DEPTH // 2
    assert n_a == 1 and DEPTH == 2, "one retention layer followed by one sliding-window layer"
    d = x_prompt.shape[-1]
    heads = RET_HEADS
    dk = ret_w_out.shape[-1] // heads
    dv = ret_w_out.shape[-2] // heads
    kvh, hd = SWA_KV_HEADS, SWA_HEAD_DIM
    row2 = lambda g: g.reshape(1, d)
    log_g = jnp.log1p(-jnp.exp2(-5.0 - jnp.arange(heads, dtype=F32)))

    w_in = ret_w_in[0].astype(BF16)
    w_out = ret_w_out[0].astype(BF16)
    wq = swa_w_q[0].astype(BF16)
    wkv = w_kv.astype(BF16)
    wo = swa_w_o[0].astype(BF16)
    w1 = [ffn_w1[l].astype(BF16) for l in range(DEPTH)]
    w2 = [ffn_w2[l].astype(BF16) for l in range(DEPTH)]
    sinks = swa_sinks[0]
    q_heads = wq.shape[1] // hd
    on_kv_head = (jnp.arange(q_heads)[:, None] // (q_heads // kvh) == jnp.arange(kvh)[None, :]).astype(BF16)
    wq_wide = (wq.reshape(d, q_heads, 1, hd) * on_kv_head[None, :, :, None]).reshape(d, q_heads * kvh * hd)

    def trunk(x, pos, ret_mixer, swa_mixer):
        b, t, _ = x.shape
        n = b * t
        h = x.reshape(n, d)
        cos, sin = (_tile_rows(tab, t, n) for tab in _ret_rope_tables(pos, dk))
        o, o_layout, state = ret_mixer(h, cos, sin, _ret_key_scale(log_g, t, n, dk))
        h = _out_ffn(o, h, w_out, row2(ret_norm_post[0]), row2(ffn_norm_pre[0]), w1[0], w2[0], row2(ffn_norm_post[0]),
                     o_layout=o_layout)
        o, o_layout, k_win, v_win = swa_mixer(h, pos, b, t)
        h = _out_ffn(o, h, wo, row2(swa_norm_post[0]), row2(ffn_norm_pre[1]), w1[1], w2[1], row2(ffn_norm_post[1]),
                     o_layout=o_layout)
        return h.reshape(b, t, d), state, k_win, v_win

    b_p, t_p, _ = x_prompt.shape
    w_p = min(WINDOW, t_p)

    def swa_prompt(h, pos, b, t):
        cos, sin = _swa_cos_sin(pos)
        qt, k, vt, k_win, v_win = _swa_in_t(h, row2(swa_norm_pre[0]), row2(kv_norm), wq, wkv, cos.T, sin.T,
                                            *_swa_rope_tables(pos), batch=b, win=w_p)
        return _attn_prompt(sinks, qt, k, vt), "features", k_win, v_win

    def ret_prompt(h, cos, sin, kscale):
        o, state = _ret_prompt(log_g, h, row2(ret_norm_pre[0]), w_in, cos, sin, kscale,
                               batch=b_p, heads=heads, dk=dk, dv=dv)
        return o, "tokens", state

    y_prompt, state_p, k_win_p, v_win_p = trunk(x_prompt, jnp.arange(t_p, dtype=F32), ret_prompt, swa_prompt)

    b_s, t_s, _ = x_sample.shape
    w_s = cache_k_win.shape[1]
    kc = cache_k_win.reshape(b_s, w_s, kvh * hd)
    vc = cache_v_win.reshape(b_s, w_s, kvh * hd)

    def swa_sample(h, pos, b, t):
        tabs = tuple(_tile_rows(tab, t, b * t) for tab in _swa_rope_tables(pos))
        q, k, v = _swa_in(h, row2(swa_norm_pre[0]), row2(kv_norm), wq_wide, wkv, *tabs, q_dtype=BF16)
        o, k_win, v_win = _attn_sample(sinks, q, k, v, kc, vc, seq=t, q_start=PAST_LEN)
        return o, "tokens", k_win, v_win

    def ret_sample(h, cos, sin, kscale):
        q, kd, v, sg = _ret_in(h, row2(ret_norm_pre[0]), w_in, cos, sin, kscale,
                               heads=heads, dk=dk, dv=dv, out_dtype=F32)
        o, state = _ret_sample(log_g, q, kd, v, sg, state_ret[0], seq=t_s)
        return o, "heads", state

    y_sample, state_s, k_win_s, v_win_s = trunk(x_sample, PAST_LEN + jnp.arange(t_s, dtype=F32), ret_sample, swa_sample)

    return (y_prompt, y_sample, state_p[None], state_s[None],
            k_win_p.reshape(b_p, w_p, kvh, hd), v_win_p.reshape(b_p, w_p, kvh, hd),
            k_win_s.reshape(b_s, w_s, kvh, hd), v_win_s.reshape(b_s, w_s, kvh, hd))
```

```python
import functools

import jax
import jax.numpy as jnp
from jax import lax
from jax.experimental import pallas as pl
from jax.experimental.pallas import tpu as pltpu

DEPTH = 2
PAST_LEN = 16384
RET_HEADS = 4
RET_ROPE_THETA = 10000.0
SWA_HEAD_DIM = 64
SWA_KV_HEADS = 4
WINDOW = 128
ROPE_THETA = 500000.0
ROT_DIM = SWA_HEAD_DIM // 4
EPS = 1e-6
NEG = -1e30
LOG2E = 1.4426950408889634

LANES = 128
SUBLANES = 8
BF16_SUBLANES = 16
VMEM_CAP_BYTES = 64 * 1024 * 1024
VMEM_BUDGET_BYTES = VMEM_CAP_BYTES - 8 * 1024 * 1024

TOKEN_TILE = 512
COL_CHUNK = 1024
RET_KERNEL_CHUNK = 256
SAMPLE_GROUP = 2
ATTN_SAMPLE_SEQS = 4
ATTN_BLOCKS_PER_STEP = 16
ATTN_LOOKAHEAD = 4
OUT_FFN_ROW_PARTS = 2
SWA_TOKEN_TILE = 1024
SWA_ROW_PARTS = 4

F32 = jnp.float32
BF16 = jnp.bfloat16


def _params(semantics, vmem_bytes):
    limit = int(min(max(vmem_bytes, 16 * 1024 * 1024), VMEM_BUDGET_BYTES))
    return pltpu.CompilerParams(dimension_semantics=semantics, vmem_limit_bytes=limit)


def _resident(shape):
    nd = len(shape)
    return pl.BlockSpec(shape, lambda *_: (0,) * nd, pipeline_mode=pl.Buffered(1))


def _nbytes(shape, dtype):
    n = 1
    for s in shape:
        n *= s
    return n * jnp.dtype(dtype).itemsize


def _rms_rows(x):
    return x * lax.rsqrt(jnp.mean(x * x, axis=-1, keepdims=True) + EPS)


def _dot(a, b):
    return jnp.dot(a, b, preferred_element_type=F32)


def _dot_nt(a, b):
    return lax.dot_general(a, b, (((1,), (1,)), ((), ())), preferred_element_type=F32)


def _dot_tn(a, b):
    return lax.dot_general(a, b, (((0,), (0,)), ((), ())), preferred_element_type=F32)


def _ret_in_body(h_ref, g_ref, w_ref, cos_ref, sin_ref, kscale_ref, q_ref, kd_ref, v_ref, sg_ref, *, heads, dk, dv,
                 rows=slice(None)):
    xn = (_rms_rows(h_ref[rows, :]) * g_ref[...]).astype(BF16)
    cos = cos_ref[rows, :]
    sin = sin_ref[rows, :]
    half = dk // 2
    qk_w = heads * dk
    v_w = heads * dv

    def proj(lo, width):
        return _dot(xn, w_ref[:, lo:lo + width])

    for base, ref, scale_ref in ((0, q_ref, None), (qk_w, kd_ref, kscale_ref)):
        p = proj(base, qk_w)
        for hh in range(heads):
            lo, mid, hi = hh * dk, hh * dk + half, (hh + 1) * dk
            x1 = p[:, lo:mid]
            x2 = p[:, mid:hi]
            o1 = x1 * cos - x2 * sin
            o2 = x2 * cos + x1 * sin
            if scale_ref is not None:
                o1 = o1 * scale_ref[rows, lo:mid]
                o2 = o2 * scale_ref[rows, mid:hi]
            ref[hh, rows, :half] = o1.astype(ref.dtype)
            ref[hh, rows, half:] = o2.astype(ref.dtype)
    cw = min(COL_CHUNK, v_w)
    per_chunk = cw // dv
    for c in range(v_w // cw):
        v = proj(2 * qk_w + c * cw, cw)
        for j in range(per_chunk):
            v_ref[c * per_chunk + j, rows, :] = v[:, j * dv:(j + 1) * dv].astype(v_ref.dtype)
    for c in range(v_w // cw):
        gate = proj(2 * qk_w + v_w + c * cw, cw)
        sg = gate * jax.nn.sigmoid(gate)
        for j in range(per_chunk):
            sg_ref[c * per_chunk + j, rows, :] = sg[:, j * dv:(j + 1) * dv].astype(sg_ref.dtype)


def _ret_in(h, g, w_in, cos, sin, kscale, *, heads, dk, dv, out_dtype):
    n, d = h.shape
    tm = min(TOKEN_TILE, n)
    qk_w, v_w = heads * dk, heads * dv
    pos_tiles = cos.shape[0] // tm
    row = lambda i: (i, 0)
    tab = lambda i: (i % pos_tiles, 0)
    by_head = lambda i: (0, i, 0)
    vmem = (2 * _nbytes((tm, d), F32) + _nbytes(w_in.shape, BF16) + 4 * _nbytes((tm, dk // 2), F32)
            + _nbytes((tm, qk_w), F32) + 2 * _nbytes((tm, 2 * qk_w + 2 * v_w), out_dtype)
            + 4 * _nbytes((tm, COL_CHUNK), F32))
    return pl.pallas_call(
        functools.partial(_ret_in_body, heads=heads, dk=dk, dv=dv),
        grid=(n // tm,),
        in_specs=[pl.BlockSpec((tm, d), row), _resident((1, d)), _resident(w_in.shape),
                  pl.BlockSpec((tm, dk // 2), tab), pl.BlockSpec((tm, dk // 2), tab), _resident((tm, qk_w))],
        out_specs=[pl.BlockSpec((heads, tm, dk), by_head), pl.BlockSpec((heads, tm, dk), by_head),
                   pl.BlockSpec((heads, tm, dv), by_head), pl.BlockSpec((heads, tm, dv), by_head)],
        out_shape=[jax.ShapeDtypeStruct((heads, n, dk), out_dtype), jax.ShapeDtypeStruct((heads, n, dk), out_dtype),
                   jax.ShapeDtypeStruct((heads, n, dv), out_dtype), jax.ShapeDtypeStruct((heads, n, dv), out_dtype)],
        compiler_params=_params(("parallel",), vmem),
        name="ret_in",
    )(h, g, w_in, cos, sin, kscale)


def _ret_prompt_body(lg_ref, h_ref, g_ref, w_ref, cos_ref, sin_ref, kscale_ref, o_ref, s_out_ref,
                     q_s, kd_s, v_s, sg_s, s_ref, *, heads, dk, dv, chunk):
    t = pl.program_id(1)

    @pl.when(t == 0)
    def _():
        s_ref[...] = jnp.zeros_like(s_ref)

    tm = h_ref.shape[0]
    for c in range(tm // chunk):
        _ret_in_body(h_ref, g_ref, w_ref, cos_ref, sin_ref, kscale_ref, q_s, kd_s, v_s, sg_s,
                     heads=heads, dk=dk, dv=dv, rows=pl.ds(c * chunk, chunk))

    ri = lax.broadcasted_iota(jnp.int32, (chunk, chunk), 0)
    ci = lax.broadcasted_iota(jnp.int32, (chunk, chunk), 1)
    lower = (ri >= ci).astype(F32)
    row_v = lax.broadcasted_iota(jnp.int32, (chunk, dv), 0).astype(F32)
    causal, q_decay, chunk_decay = [], [], []
    for hh in range(heads):
        lg = lg_ref[hh]
        causal.append(lower * jnp.exp(jnp.full((1, chunk), -lg * chunk, F32)))
        q_decay.append(jnp.exp(lg * (row_v + 1.0)))
        chunk_decay.append(jnp.exp(jnp.full((1, dv), lg * chunk, F32)))

    for c in range(tm // chunk):
        rows = pl.ds(c * chunk, chunk)
        qk = [_dot_nt(q_s[hh, rows, :], kd_s[hh, rows, :]) for hh in range(heads)]
        grow = [_dot_tn(kd_s[hh, rows, :], v_s[hh, rows, :]) for hh in range(heads)]
        for hh in range(heads):
            s_prev = s_ref[hh]
            lhs = jnp.concatenate([(qk[hh] * causal[hh]).astype(BF16), q_s[hh, rows, :]], axis=1)
            rhs = jnp.concatenate([v_s[hh, rows, :], s_prev.astype(BF16)], axis=0)
            o = _rms_rows(q_decay[hh] * _dot(lhs, rhs))
            s_ref[hh] = chunk_decay[hh] * s_prev + grow[hh]
            o_ref[rows, hh * dv:(hh + 1) * dv] = (o * sg_s[hh, rows, :].astype(F32)).astype(o_ref.dtype)

    @pl.when(t == pl.num_programs(1) - 1)
    def _():
        s_out_ref[0] = s_ref[...]


def _ret_chunk(seq):
    return RET_KERNEL_CHUNK if seq % RET_KERNEL_CHUNK == 0 else seq


def _ret_prompt(log_g, h, g, w_in, cos, sin, kscale, *, batch, heads, dk, dv):
    n, d = h.shape
    seq = n // batch
    tm = min(TOKEN_TILE, seq)
    chunk = _ret_chunk(seq)
    assert tm % chunk == 0 and seq % tm == 0
    nt = seq // tm
    qk_w, v_w = heads * dk, heads * dv
    row = lambda b, t: (b * nt + t, 0)
    tab = lambda b, t: (t, 0)
    vmem = (2 * _nbytes((tm, d), F32) + _nbytes(w_in.shape, BF16) + 4 * _nbytes((tm, dk // 2), F32)
            + _nbytes((tm, qk_w), F32) + 2 * _nbytes((tm, v_w), BF16) + _nbytes((tm, 2 * qk_w + 2 * v_w), BF16)
            + 3 * _nbytes((heads, dk, dv), F32) + 4 * _nbytes((tm, COL_CHUNK), F32)
            + 2 * heads * (_nbytes((chunk, chunk), F32) + _nbytes((dk, dv), F32) + _nbytes((chunk, dv), F32)))
    return pl.pallas_call(
        functools.partial(_ret_prompt_body, heads=heads, dk=dk, dv=dv, chunk=chunk),
        grid=(batch, nt),
        in_specs=[pl.BlockSpec(memory_space=pltpu.SMEM),
                  pl.BlockSpec((tm, d), row), _resident((1, d)), _resident(w_in.shape),
                  pl.BlockSpec((tm, dk // 2), tab), pl.BlockSpec((tm, dk // 2), tab), _resident((tm, qk_w))],
        out_specs=[pl.BlockSpec((tm, v_w), row),
                   pl.BlockSpec((1, heads, dk, dv), lambda b, t: (b, 0, 0, 0))],
        out_shape=[jax.ShapeDtypeStruct((n, v_w), BF16),
                   jax.ShapeDtypeStruct((batch, heads, dk, dv), F32)],
        scratch_shapes=[pltpu.VMEM((heads, tm, dk), BF16), pltpu.VMEM((heads, tm, dk), BF16),
                        pltpu.VMEM((heads, tm, dv), BF16), pltpu.VMEM((heads, tm, dv), BF16),
                        pltpu.VMEM((heads, dk, dv), F32)],
        compiler_params=_params(("parallel", "arbitrary"), vmem),
        name="ret_prompt",
    )(log_g, h, g, w_in, cos, sin, kscale)


def _ret_sample_body(lg_ref, q_ref, kd_ref, v_ref, sg_ref, s_in_ref, o_ref, s_out_ref, *, seq):
    heads, rows, dk = q_ref.shape
    dv = v_ref.shape[2]
    group = rows // seq
    ri = lax.broadcasted_iota(jnp.int32, (rows, rows), 0)
    ci = lax.broadcasted_iota(jnp.int32, (rows, rows), 1)
    visible = ((ri // seq) == (ci // seq)) & (ri >= ci)
    row_v = lax.broadcasted_iota(jnp.int32, (rows, dv), 0)
    row_k = lax.broadcasted_iota(jnp.int32, (rows, dk), 0)
    for hh in range(heads):
        lg = lg_ref[hh]
        causal = jnp.where(visible, jnp.exp(jnp.full((rows, rows), -lg * seq, F32)), 0.0)
        q_decay = jnp.exp(lg * ((row_v % seq).astype(F32) + 1.0))
        chunk_decay = jnp.exp(jnp.full((1, dv), lg * seq, F32))
        q = q_ref[hh].astype(BF16)
        kd = kd_ref[hh]
        v = v_ref[hh].astype(BF16)
        o = _dot((_dot_nt(q, kd.astype(BF16)) * causal).astype(BF16), v)
        for g in range(group):
            s_prev = s_in_ref[g, hh]
            o = jnp.where((row_v // seq) == g, o + _dot(q, s_prev.astype(BF16)), o)
            kd_g = jnp.where((row_k // seq) == g, kd, 0.0).astype(BF16)
            s_out_ref[g, hh] = chunk_decay * s_prev + _dot_tn(kd_g, v)
        o = _rms_rows(q_decay * o)
        o_ref[hh] = (o * sg_ref[hh]).astype(o_ref.dtype)


def _ret_sample(log_g, q, kd, v, sg, state, *, seq):
    heads, n, dk = q.shape
    dv = v.shape[2]
    batch = n // seq
    group = SAMPLE_GROUP if batch % SAMPLE_GROUP == 0 else batch
    rows = group * seq
    by_head = lambda i: (0, i, 0)
    st = lambda i: (i, 0, 0, 0)
    vmem = (4 * _nbytes((group, heads, dk, dv), F32) + 8 * _nbytes((rows, heads * dv), F32)
            + 4 * _nbytes((dk, dv), F32))
    return pl.pallas_call(
        functools.partial(_ret_sample_body, seq=seq),
        grid=(batch // group,),
        in_specs=[pl.BlockSpec(memory_space=pltpu.SMEM),
                  pl.BlockSpec((heads, rows, dk), by_head), pl.BlockSpec((heads, rows, dk), by_head),
                  pl.BlockSpec((heads, rows, dv), by_head), pl.BlockSpec((heads, rows, dv), by_head),
                  pl.BlockSpec((group, heads, dk, dv), st)],
        out_specs=[pl.BlockSpec((heads, rows, dv), by_head), pl.BlockSpec((group, heads, dk, dv), st)],
        out_shape=[jax.ShapeDtypeStruct((heads, n, dv), F32),
                   jax.ShapeDtypeStruct((batch, heads, dk, dv), F32)],
        compiler_params=_params(("parallel",), vmem),
        name="ret_sample",
    )(log_g, q, kd, v, sg, state)


def _out_ffn_body(o_ref, h_ref, wo_ref, g_post_ref, g_pre_ref, w1_ref, w2_ref, g_ffn_ref, y_ref, *, o_layout):
    tm = h_ref.shape[0]
    parts = OUT_FFN_ROW_PARTS if tm % (OUT_FFN_ROW_PARTS * BF16_SUBLANES) == 0 else 1
    rp = tm // parts
    d_ff = w1_ref.shape[2]
    fc = min(COL_CHUNK, d_ff)

    def mixer_out(p):
        rows = slice(p * rp, (p + 1) * rp)
        if o_layout == "features":
            return _dot_tn(o_ref[0, :, rows], wo_ref[...])
        if o_layout == "heads":
            o = jnp.concatenate([o_ref[hh, rows, :] for hh in range(o_ref.shape[0])], axis=1)
            return _dot(o.astype(BF16), wo_ref[...])
        return _dot(o_ref[rows, :].astype(BF16), wo_ref[...])

    a = [mixer_out(p) for p in range(parts)]
    h1, x = [], []
    for p in range(parts):
        rows = slice(p * rp, (p + 1) * rp)
        h1.append(h_ref[rows, :] + _rms_rows(a[p]) * g_post_ref[...])
        x.append((_rms_rows(h1[p]) * g_pre_ref[...]).astype(BF16))
    acc = [jnp.zeros((rp, h_ref.shape[1]), F32) for _ in range(parts)]
    for c in range(d_ff // fc):
        for p in range(parts):
            u = jnp.maximum(_dot(x[p], w1_ref[0, :, c * fc:(c + 1) * fc]), 0.0)
            acc[p] = acc[p] + _dot((u * u).astype(BF16), w2_ref[0, c * fc:(c + 1) * fc, :])
    for p in range(parts):
        rows = slice(p * rp, (p + 1) * rp)
        y_ref[rows, :] = h1[p] + _rms_rows(acc[p]) * g_ffn_ref[...]


def _out_ffn(o, h, w_o, g_post, g_pre, w1, w2, g_ffn, *, layer, o_layout):
    n, d = h.shape
    one_layer = lambda w: pl.BlockSpec((1,) + w.shape[1:], lambda i: (layer, 0, 0), pipeline_mode=pl.Buffered(1))
    kdim = w_o.shape[0]
    tm = min(TOKEN_TILE, n)
    row = lambda i: (i, 0)
    if o_layout == "features":
        tiles = o.shape[2] // tm
        o_spec = pl.BlockSpec((1, kdim, tm), lambda i: (i // tiles, 0, i % tiles))
    elif o_layout == "heads":
        o_spec = pl.BlockSpec((o.shape[0], tm, o.shape[2]), lambda i: (0, i, 0))
    else:
        o_spec = pl.BlockSpec((tm, kdim), row)
    vmem = (2 * _nbytes((tm, kdim), o.dtype) + 4 * _nbytes((tm, d), F32) + _nbytes(w_o.shape, BF16)
            + _nbytes(w1.shape[1:], BF16) + _nbytes(w2.shape[1:], BF16) + 6 * _nbytes((tm, COL_CHUNK), F32))
    return pl.pallas_call(
        functools.partial(_out_ffn_body, o_layout=o_layout),
        grid=(n // tm,),
        in_specs=[o_spec, pl.BlockSpec((tm, d), row), _resident(w_o.shape),
                  _resident((1, d)), _resident((1, d)), one_layer(w1), one_layer(w2),
                  _resident((1, d))],
        out_specs=pl.BlockSpec((tm, d), row),
        out_shape=jax.ShapeDtypeStruct((n, d), F32),
        compiler_params=_params(("parallel",), vmem),
        name="out_ffn",
    )(o, h, w_o, g_post, g_pre, w1, w2, g_ffn)


def _partial_rope(x, c_tab, sa_tab, sb_tab):
    half = ROT_DIM // 2
    outs = []
    for j in range(x.shape[1] // LANES):
        s = x[:, j * LANES:(j + 1) * LANES]
        outs.append(s * c_tab + pltpu.roll(s, LANES - half, axis=1) * sa_tab + pltpu.roll(s, half, axis=1) * sb_tab)
    return outs


def _swa_in_body(h_ref, g_q_ref, g_kv_ref, wq_ref, wkv_ref, c_ref, sa_ref, sb_ref, q_ref, k_ref, v_ref):
    y = _rms_rows(h_ref[...])
    xq = (y * g_q_ref[...]).astype(BF16)
    xkv = (y * g_kv_ref[...]).astype(BF16)
    c_tab, sa_tab, sb_tab = c_ref[...], sa_ref[...], sb_ref[...]
    q = _dot(xq, wq_ref[...]) * (SWA_HEAD_DIM ** -0.5 * LOG2E)
    for j, s in enumerate(_partial_rope(q, c_tab, sa_tab, sb_tab)):
        q_ref[:, j * LANES:(j + 1) * LANES] = s.astype(q_ref.dtype)
    kv = _dot(xkv, wkv_ref[...])
    kw = k_ref.shape[1]
    for j, s in enumerate(_partial_rope(kv[:, :kw], c_tab, sa_tab, sb_tab)):
        k_ref[:, j * LANES:(j + 1) * LANES] = s
    v_ref[...] = kv[:, kw:]


def _swa_in(h, g_q, g_kv, w_q, w_kv, c_tab, sa_tab, sb_tab, *, q_dtype):
    n, d = h.shape
    tm = min(TOKEN_TILE, n)
    qw = w_q.shape[1]
    kw = w_kv.shape[1] // 2
    pos_tiles = c_tab.shape[0] // tm
    row = lambda i: (i, 0)
    tab = lambda i: (i % pos_tiles, 0)
    vmem = (2 * _nbytes((tm, d), F32) + _nbytes(w_q.shape, BF16) + _nbytes(w_kv.shape, BF16)
            + 6 * _nbytes((tm, LANES), F32) + 2 * _nbytes((tm, qw), q_dtype) + 4 * _nbytes((tm, kw), F32)
            + 6 * _nbytes((tm, qw), F32))
    return pl.pallas_call(
        _swa_in_body,
        grid=(n // tm,),
        in_specs=[pl.BlockSpec((tm, d), row), _resident((1, d)), _resident((1, d)),
                  _resident(w_q.shape), _resident(w_kv.shape),
                  pl.BlockSpec((tm, LANES), tab), pl.BlockSpec((tm, LANES), tab), pl.BlockSpec((tm, LANES), tab)],
        out_specs=[pl.BlockSpec((tm, qw), row), pl.BlockSpec((tm, kw), row), pl.BlockSpec((tm, kw), row)],
        out_shape=[jax.ShapeDtypeStruct((n, qw), q_dtype), jax.ShapeDtypeStruct((n, kw), F32),
                   jax.ShapeDtypeStruct((n, kw), F32)],
        compiler_params=_params(("parallel",), vmem),
        name="swa_in",
    )(h, g_q, g_kv, w_q, w_kv, c_tab, sa_tab, sb_tab)


def _swa_in_t_body(h_ref, g_q_ref, g_kv_ref, wqt_ref, wk_ref, wvt_ref, wv_ref, cos_t_ref, sin_t_ref,
                   c_ref, sa_ref, sb_ref, qt_ref, k_ref, vt_ref, kwin_ref, vwin_ref, *, tiles):
    tm = h_ref.shape[0]
    hd = SWA_HEAD_DIM
    half = ROT_DIM // 2
    win = kwin_ref.shape[0]
    parts = SWA_ROW_PARTS if tm % (SWA_ROW_PARTS * LANES) == 0 else 1
    rp = tm // parts
    assert rp >= win
    for p in range(parts):
        rows = slice(p * rp, (p + 1) * rp)
        y = _rms_rows(h_ref[rows, :])
        xq = (y * g_q_ref[...]).astype(BF16)
        xkv = (y * g_kv_ref[...]).astype(BF16)
        cos_t, sin_t = cos_t_ref[:, rows], sin_t_ref[:, rows]
        qt = _dot_nt(wqt_ref[...], xq) * (hd ** -0.5 * LOG2E)
        for hq in range(qt.shape[0] // hd):
            base = hq * hd
            x1 = qt[base:base + half]
            x2 = qt[base + half:base + 2 * half]
            rot = jnp.concatenate([x1 * cos_t - x2 * sin_t, x2 * cos_t + x1 * sin_t], axis=0)
            qt_ref[0, base:base + 2 * half, rows] = rot.astype(qt_ref.dtype)
            qt_ref[0, base + 2 * half:base + hd, rows] = qt[base + 2 * half:base + hd].astype(qt_ref.dtype)
        k_rot = _partial_rope(_dot(xkv, wk_ref[...]), c_ref[rows, :], sa_ref[rows, :], sb_ref[rows, :])
        for j, s in enumerate(k_rot):
            k_ref[rows, j * LANES:(j + 1) * LANES] = s.astype(k_ref.dtype)
        vt_ref[0, :, rows] = _dot_nt(wvt_ref[...], xkv).astype(vt_ref.dtype)
        if p == parts - 1:
            @pl.when(pl.program_id(0) % tiles == tiles - 1)
            def _():
                for j, s in enumerate(k_rot):
                    kwin_ref[:, j * LANES:(j + 1) * LANES] = s[rp - win:, :]
                vwin_ref[...] = _dot(xkv[rp - win:, :], wv_ref[...])


def _swa_in_t(h, g_q, g_kv, w_q, w_kv, cos_t, sin_t, c_tab, sa_tab, sb_tab, *, batch, win):
    n, d = h.shape
    seq = n // batch
    tm = min(SWA_TOKEN_TILE, seq)
    tiles = seq // tm
    qw = w_q.shape[1]
    kw = w_kv.shape[1] // 2
    wqt = w_q.T
    wk, wv = w_kv[:, :kw], w_kv[:, kw:]
    wvt = wv.T
    row = lambda i: (i, 0)
    tab = lambda i: (i % tiles, 0)
    tab_t = lambda i: (0, i % tiles)
    feat = lambda i: (i // tiles, 0, i % tiles)
    per_seq = lambda i: (i // tiles, 0)
    half = ROT_DIM // 2
    vmem = (2 * _nbytes((tm, d), F32) + 2 * _nbytes(w_q.shape, BF16) + 3 * _nbytes(w_kv.shape, BF16)
            + 8 * _nbytes((tm, LANES), F32) + 2 * _nbytes((tm, qw + 2 * kw), BF16) + 4 * _nbytes((win, kw), F32)
            + 4 * _nbytes((tm, qw), F32))
    return pl.pallas_call(
        functools.partial(_swa_in_t_body, tiles=tiles),
        grid=(n // tm,),
        in_specs=[pl.BlockSpec((tm, d), row), _resident((1, d)), _resident((1, d)),
                  _resident(wqt.shape), _resident(wk.shape), _resident(wvt.shape), _resident(wv.shape),
                  pl.BlockSpec((half, tm), tab_t), pl.BlockSpec((half, tm), tab_t),
                  pl.BlockSpec((tm, LANES), tab), pl.BlockSpec((tm, LANES), tab), pl.BlockSpec((tm, LANES), tab)],
        out_specs=[pl.BlockSpec((1, qw, tm), feat), pl.BlockSpec((tm, kw), row), pl.BlockSpec((1, kw, tm), feat),
                   pl.BlockSpec((win, kw), per_seq), pl.BlockSpec((win, kw), per_seq)],
        out_shape=[jax.ShapeDtypeStruct((batch, qw, seq), BF16), jax.ShapeDtypeStruct((n, kw), BF16),
                   jax.ShapeDtypeStruct((batch, kw, seq), BF16),
                   jax.ShapeDtypeStruct((batch * win, kw), F32), jax.ShapeDtypeStruct((batch * win, kw), F32)],
        compiler_params=_params(("arbitrary",), vmem),
        name="swa_in_t",
    )(h, g_q, g_kv, wqt, wk, wvt, wv, cos_t, sin_t, c_tab, sa_tab, sb_tab)


def _attn_prompt_body(sinks_ref, mask_ref, qt_ref, kp_ref, kc_ref, vtp_ref, vtc_ref, ot_ref, *, group):
    blk = kp_ref.shape[0]
    nblk = kc_ref.shape[0] // blk
    hd = SWA_HEAD_DIM
    kvh_n = kc_ref.shape[1] // hd
    cols = group * blk
    kj = lax.broadcasted_iota(jnp.int32, (blk, cols), 0)
    qi = lax.broadcasted_iota(jnp.int32, (blk, cols), 1) % blk
    own = kj <= qi
    lane_head = lax.broadcasted_iota(jnp.int32, (1, cols), 1) // blk
    ones_rows = jnp.ones((BF16_SUBLANES, blk), BF16)
    has_prev = pl.program_id(1) > 0

    def scores(j, kvh):
        tile, lo = divmod(kvh * hd, LANES)
        q4t = jnp.concatenate([qt_ref[0, (kvh * group + g) * hd:(kvh * group + g + 1) * hd, j * blk:(j + 1) * blk]
                               for g in range(group)], axis=1)
        rhs = jnp.concatenate([q4t if part * hd == lo else jnp.zeros_like(q4t) for part in range(LANES // hd)], axis=0)
        k_tile = slice(tile * LANES, (tile + 1) * LANES)
        k_prev = kp_ref[:, k_tile] if j == 0 else kc_ref[(j - 1) * blk:j * blk, k_tile]
        return _dot(kc_ref[j * blk:(j + 1) * blk, k_tile], rhs), _dot(k_prev, rhs)

    def finish(j, kvh, s_own, s_prev):
        if j == 0:
            s_prev = jnp.where(has_prev, s_prev, NEG)
        s = jnp.where(own, s_own, s_prev)
        sink = jnp.full((1, cols), sinks_ref[kvh * group] * LOG2E, F32)
        for g in range(1, group):
            sink = jnp.where(lane_head == g, sinks_ref[kvh * group + g] * LOG2E, sink)
        m = jnp.maximum(jnp.max(s, axis=0, keepdims=True), sink)
        e = jnp.exp2(s - m).astype(BF16)
        p_own = e * mask_ref[...]
        p = jnp.concatenate([p_own, e - p_own], axis=0)
        head_rows = slice(kvh * hd, (kvh + 1) * hd)
        vt_own = vtc_ref[0, head_rows, j * blk:(j + 1) * blk]
        vt_prev = vtp_ref[0, head_rows, :] if j == 0 else vtc_ref[0, head_rows, (j - 1) * blk:j * blk]
        vt = jnp.concatenate([jnp.concatenate([vt_own, ones_rows], axis=0),
                              jnp.concatenate([vt_prev, ones_rows], axis=0)], axis=1)
        acc = _dot(vt, p)
        denom = acc[hd:hd + 1, :] + jnp.exp2(sink - m)
        ot = acc[:hd] / denom
        for g in range(group):
            hq = kvh * group + g
            ot_ref[0, hq * hd:(hq + 1) * hd, j * blk:(j + 1) * blk] = ot[:, g * blk:(g + 1) * blk].astype(ot_ref.dtype)

    units = [(j, kvh) for j in range(nblk) for kvh in range(kvh_n)]
    queue = [scores(*unit) for unit in units[:ATTN_LOOKAHEAD]]
    for idx, unit in enumerate(units):
        if idx + ATTN_LOOKAHEAD < len(units):
            queue.append(scores(*units[idx + ATTN_LOOKAHEAD]))
        finish(*unit, *queue.pop(0))


def _attn_prompt(sinks, qt, k, vt):
    batch, qw, seq = qt.shape
    kw = k.shape[1]
    blk = WINDOW
    nblk = ATTN_BLOCKS_PER_STEP if seq % (ATTN_BLOCKS_PER_STEP * blk) == 0 else 1
    span = nblk * blk
    steps = seq // span
    group = qw // kw
    cols = group * blk
    own = (jnp.arange(blk)[:, None] <= (jnp.arange(cols) % blk)[None, :]).astype(BF16)
    cur_t = lambda b, i: (b, 0, i)
    prev_t = lambda b, i: (b, 0, jnp.maximum(i * nblk - 1, 0))
    cur = lambda b, i: (b * steps + i, 0)
    prev = lambda b, i: (b * steps * nblk + jnp.maximum(i * nblk - 1, 0), 0)
    vmem = (4 * _nbytes((qw, span), BF16) + 6 * _nbytes((span, kw), BF16) + 16 * _nbytes((blk, cols), F32))
    return pl.pallas_call(
        functools.partial(_attn_prompt_body, group=group),
        grid=(batch, steps),
        in_specs=[pl.BlockSpec(memory_space=pltpu.SMEM), _resident((blk, cols)), pl.BlockSpec((1, qw, span), cur_t),
                  pl.BlockSpec((blk, kw), prev), pl.BlockSpec((span, kw), cur),
                  pl.BlockSpec((1, kw, blk), prev_t), pl.BlockSpec((1, kw, span), cur_t)],
        out_specs=pl.BlockSpec((1, qw, span), cur_t),
        out_shape=jax.ShapeDtypeStruct((batch, qw, seq), BF16),
        compiler_params=_params(("parallel", "parallel"), vmem),
        name="attn_prompt",
    )(sinks, own, qt, k, k, vt, vt)


def _attn_sample_body(sink_ref, q_ref, kn_ref, vn_ref, kc_ref, vc_ref, o_ref, kw_ref, vw_ref, *, seq, group, q_start):
    rows = q_ref.shape[0]
    nseq = rows // seq
    win = kc_ref.shape[1]
    hd = SWA_HEAD_DIM
    kw = kn_ref.shape[1]
    heads = q_ref.shape[1] // kw
    srows = heads * rows
    lhs = jnp.concatenate([q_ref[:, hq * kw:(hq + 1) * kw] for hq in range(heads)], axis=0)
    kn = kn_ref[...]
    vn = vn_ref[...]
    sink = sink_ref[...]

    r_c = lax.broadcasted_iota(jnp.int32, (srows, win), 0) % rows
    c_c = lax.broadcasted_iota(jnp.int32, (srows, win), 1)
    seq_c = r_c // seq
    rel_c = (r_c % seq) + win - c_c
    ok_c = (rel_c >= 0) & (rel_c < WINDOW) & (q_start - win + c_c >= 0)
    s_c = _dot_nt(lhs, kc_ref[0].astype(BF16))
    for b in range(1, nseq):
        s_c = jnp.where(seq_c == b, _dot_nt(lhs, kc_ref[b].astype(BF16)), s_c)
    s_c = jnp.where(ok_c, s_c, NEG)

    r_n = lax.broadcasted_iota(jnp.int32, (srows, rows), 0) % rows
    c_n = lax.broadcasted_iota(jnp.int32, (srows, rows), 1)
    rel_n = (r_n % seq) - (c_n % seq)
    ok_n = (rel_n >= 0) & (rel_n < WINDOW) & ((r_n // seq) == (c_n // seq))
    s_n = jnp.where(ok_n, _dot_nt(lhs, kn.astype(BF16)), NEG)

    m = jnp.maximum(jnp.maximum(jnp.max(s_c, axis=-1, keepdims=True), jnp.max(s_n, axis=-1, keepdims=True)), sink)
    e_c = jnp.exp2(s_c - m)
    e_n = jnp.exp2(s_n - m)
    denom = jnp.sum(e_c, axis=-1, keepdims=True) + jnp.sum(e_n, axis=-1, keepdims=True) + jnp.exp2(sink - m)
    acc = _dot(e_n.astype(BF16), vn.astype(BF16))
    for b in range(nseq):
        acc = acc + _dot(jnp.where(seq_c == b, e_c, 0.0).astype(BF16), vc_ref[b].astype(BF16))
    o = acc / denom
    for hq in range(heads):
        kvh = hq // group
        o_ref[:, hq * hd:(hq + 1) * hd] = o[hq * rows:(hq + 1) * rows, kvh * hd:(kvh + 1) * hd].astype(o_ref.dtype)
    for b in range(nseq):
        kw_ref[b, 0:win - seq, :] = kc_ref[b, seq:win, :]
        kw_ref[b, win - seq:win, :] = kn[b * seq:(b + 1) * seq, :]
        vw_ref[b, 0:win - seq, :] = vc_ref[b, seq:win, :]
        vw_ref[b, win - seq:win, :] = vn[b * seq:(b + 1) * seq, :]


def _attn_sample(sinks, q_wide, k_new, v_new, k_cache, v_cache, *, seq, q_start):
    n, qww = q_wide.shape
    kw = k_new.shape[1]
    heads = qww // kw
    hd = SWA_HEAD_DIM
    group = heads // (kw // hd)
    batch, win, _ = k_cache.shape
    nseq = ATTN_SAMPLE_SEQS if batch % ATTN_SAMPLE_SEQS == 0 else batch
    rows = nseq * seq
    sink_rows = jnp.repeat(sinks * LOG2E, rows)[:, None]
    row = lambda i: (i, 0)
    cache = lambda i: (i, 0, 0)
    vmem = (8 * _nbytes((nseq, win, kw), F32) + 4 * _nbytes((rows, qww), BF16)
            + 16 * _nbytes((heads * rows, win + kw), F32))
    return pl.pallas_call(
        functools.partial(_attn_sample_body, seq=seq, group=group, q_start=q_start),
        grid=(batch // nseq,),
        in_specs=[_resident((heads * rows, 1)), pl.BlockSpec((rows, qww), row),
                  pl.BlockSpec((rows, kw), row), pl.BlockSpec((rows, kw), row),
                  pl.BlockSpec((nseq, win, kw), cache), pl.BlockSpec((nseq, win, kw), cache)],
        out_specs=[pl.BlockSpec((rows, heads * hd), row), pl.BlockSpec((nseq, win, kw), cache),
                   pl.BlockSpec((nseq, win, kw), cache)],
        out_shape=[jax.ShapeDtypeStruct((n, heads * hd), BF16), jax.ShapeDtypeStruct((batch, win, kw), F32),
                   jax.ShapeDtypeStruct((batch, win, kw), F32)],
        compiler_params=_params(("parallel",), vmem),
        name="attn_sample",
    )(sink_rows, q_wide, k_new, v_new, k_cache, v_cache)


def _ret_rope_tables(pos, dk):
    inv = 1.0 / (RET_ROPE_THETA ** jnp.linspace(0.0, 1.0, dk // 2, dtype=F32))
    ang = pos[:, None] * inv[None, :]
    return jnp.cos(ang), jnp.sin(ang)


def _ret_key_scale(log_g, seq, n, dk):
    chunk = _ret_chunk(seq)
    tm = min(TOKEN_TILE, n)
    assert tm % chunk == 0
    left = (chunk - 1 - jnp.arange(tm) % chunk).astype(F32)
    per_head = jnp.exp(log_g[None, :] * left[:, None]) * dk ** -0.5
    return jnp.repeat(per_head, dk, axis=1)


def _swa_cos_sin(pos):
    half = ROT_DIM // 2
    inv = ROPE_THETA ** (-jnp.arange(half, dtype=F32) / half)
    ang = pos[:, None] * inv[None, :]
    return jnp.cos(ang), jnp.sin(ang)


def _swa_rope_tables(pos):
    half = ROT_DIM // 2
    cos, sin = _swa_cos_sin(pos)
    n = pos.shape[0]
    pad = jnp.zeros((n, SWA_HEAD_DIM - 2 * half), F32)
    c_head = jnp.concatenate([cos, cos, pad + 1.0], axis=1)
    sa_head = jnp.concatenate([-sin, jnp.zeros_like(sin), pad], axis=1)
    sb_head = jnp.concatenate([jnp.zeros_like(sin), sin, pad], axis=1)
    reps = LANES // SWA_HEAD_DIM
    return tuple(jnp.tile(t, (1, reps)) for t in (c_head, sa_head, sb_head))


def _tile_rows(tab, seq, n):
    tm = min(TOKEN_TILE, n)
    return tab if seq >= tm else jnp.tile(tab, (tm // seq, 1))


def kernel(x_prompt, x_sample, state_ret, cache_k_win, cache_v_win, ret_norm_pre, ret_w_in, ret_w_out, ret_norm_post, kv_norm, w_kv, swa_norm_pre, swa_w_q, swa_sinks, swa_w_o, swa_norm_post, ffn_norm_pre, ffn_w1, ffn_w2, ffn_norm_post):
    n_a = DEPTH // 2
    assert n_a == 1 and DEPTH == 2, "one retention layer followed by one sliding-window layer"
    d = x_prompt.shape[-1]
    heads = RET_HEADS
    dk = ret_w_out.shape[-1] // heads
    dv = ret_w_out.shape[-2] // heads
    kvh, hd = SWA_KV_HEADS, SWA_HEAD_DIM
    row2 = lambda g: g.reshape(1, d)
    log_g = jnp.log1p(-jnp.exp2(-5.0 - jnp.arange(heads, dtype=F32)))

    w_in = ret_w_in[0].astype(BF16)
    w_out = ret_w_out[0].astype(BF16)
    wq = swa_w_q[0].astype(BF16)
    wkv = w_kv.astype(BF16)
    wo = swa_w_o[0].astype(BF16)
    w1 = ffn_w1.astype(BF16)
    w2 = ffn_w2.astype(BF16)
    sinks = swa_sinks[0]
    q_heads = wq.shape[1] // hd
    on_kv_head = (jnp.arange(q_heads)[:, None] // (q_heads // kvh) == jnp.arange(kvh)[None, :]).astype(BF16)
    wq_wide = (wq.reshape(d, q_heads, 1, hd) * on_kv_head[None, :, :, None]).reshape(d, q_heads * kvh * hd)

    def trunk(x, pos, ret_mixer, swa_mixer):
        b, t, _ = x.shape
        n = b * t
        h = x.reshape(n, d)
        cos, sin = (_tile_rows(tab, t, n) for tab in _ret_rope_tables(pos, dk))
        o, o_layout, state = ret_mixer(h, cos, sin, _ret_key_scale(log_g, t, n, dk))
        h = _out_ffn(o, h, w_out, row2(ret_norm_post[0]), row2(ffn_norm_pre[0]), w1, w2, row2(ffn_norm_post[0]),
                     layer=0, o_layout=o_layout)
        o, o_layout, k_win, v_win = swa_mixer(h, pos, b, t)
        h = _out_ffn(o, h, wo, row2(swa_norm_post[0]), row2(ffn_norm_pre[1]), w1, w2, row2(ffn_norm_post[1]),
                     layer=1, o_layout=o_layout)
        return h.reshape(b, t, d), state, k_win, v_win

    b_p, t_p, _ = x_prompt.shape
    w_p = min(WINDOW, t_p)

    def swa_prompt(h, pos, b, t):
        cos, sin = _swa_cos_sin(pos)
        qt, k, vt, k_win, v_win = _swa_in_t(h, row2(swa_norm_pre[0]), row2(kv_norm), wq, wkv, cos.T, sin.T,
                                            *_swa_rope_tables(pos), batch=b, win=w_p)
        return _attn_prompt(sinks, qt, k, vt), "features", k_win, v_win

    def ret_prompt(h, cos, sin, kscale):
        o, state = _ret_prompt(log_g, h, row2(ret_norm_pre[0]), w_in, cos, sin, kscale,
                               batch=b_p, heads=heads, dk=dk, dv=dv)
        return o, "tokens", state

    y_prompt, state_p, k_win_p, v_win_p = trunk(x_prompt, jnp.arange(t_p, dtype=F32), ret_prompt, swa_prompt)

    b_s, t_s, _ = x_sample.shape
    w_s = cache_k_win.shape[1]
    kc = cache_k_win.reshape(b_s, w_s, kvh * hd)
    vc = cache_v_win.reshape(b_s, w_s, kvh * hd)

    def swa_sample(h, pos, b, t):
        tabs = tuple(_tile_rows(tab, t, b * t) for tab in _swa_rope_tables(pos))
        q, k, v = _swa_in(h, row2(swa_norm_pre[0]), row2(kv_norm), wq_wide, wkv, *tabs, q_dtype=BF16)
        o, k_win, v_win = _attn_sample(sinks, q, k, v, kc, vc, seq=t, q_start=PAST_LEN)
        return o, "tokens", k_win, v_win

    def ret_sample(h, cos, sin, kscale):
        q, kd, v, sg = _ret_in(h, row2(ret_norm_pre[0]), w_in, cos, sin, kscale,
                               heads=heads, dk=dk, dv=dv, out_dtype=F32)
        o, state = _ret_sample(log_g, q, kd, v, sg, state_ret[0], seq=t_s)
        return o, "heads", state

    y_sample, state_s, k_win_s, v_win_s = trunk(x_sample, PAST_LEN + jnp.arange(t_s, dtype=F32), ret_sample, swa_sample)

    return (y_prompt, y_sample, state_p[None], state_s[None],
            k_win_p.reshape(b_p, w_p, kvh, hd), v_win_p.reshape(b_p, w_p, kvh, hd),
            k_win_s.reshape(b_s, w_s, kvh, hd), v_win_s.reshape(b_s, w_s, kvh, hd))
```

```python
import functools

import jax
import jax.numpy as jnp
from jax import lax
from jax.experimental import pallas as pl
from jax.experimental.pallas import tpu as pltpu

DEPTH = 2
PAST_LEN = 16384
RET_HEADS = 4
RET_ROPE_THETA = 10000.0
SWA_HEAD_DIM = 64
SWA_KV_HEADS = 4
WINDOW = 128
ROPE_THETA = 500000.0
ROT_DIM = SWA_HEAD_DIM // 4
EPS = 1e-6
NEG = -1e30
LOG2E = 1.4426950408889634

LANES = 128
SUBLANES = 8
BF16_SUBLANES = 16
VMEM_CAP_BYTES = 64 * 1024 * 1024
VMEM_BUDGET_BYTES = VMEM_CAP_BYTES - 8 * 1024 * 1024

TOKEN_TILE = 512
COL_CHUNK = 1024
RET_KERNEL_CHUNK = 256
SAMPLE_GROUP = 4
ATTN_SAMPLE_SEQS = 4
ATTN_BLOCKS_PER_STEP = 16
ATTN_LOOKAHEAD = 4
OUT_FFN_ROW_PARTS = 2
SWA_TOKEN_TILE = 1024
SWA_ROW_PARTS = 4

F32 = jnp.float32
BF16 = jnp.bfloat16


def _params(semantics, vmem_bytes):
    limit = int(min(max(vmem_bytes, 16 * 1024 * 1024), VMEM_BUDGET_BYTES))
    return pltpu.CompilerParams(dimension_semantics=semantics, vmem_limit_bytes=limit)


def _resident(shape):
    nd = len(shape)
    return pl.BlockSpec(shape, lambda *_: (0,) * nd, pipeline_mode=pl.Buffered(1))


def _nbytes(shape, dtype):
    n = 1
    for s in shape:
        n *= s
    return n * jnp.dtype(dtype).itemsize


def _rms_rows(x):
    return x * lax.rsqrt(jnp.mean(x * x, axis=-1, keepdims=True) + EPS)


def _dot(a, b):
    return jnp.dot(a, b, preferred_element_type=F32)


def _dot_nt(a, b):
    return lax.dot_general(a, b, (((1,), (1,)), ((), ())), preferred_element_type=F32)


def _dot_tn(a, b):
    return lax.dot_general(a, b, (((0,), (0,)), ((), ())), preferred_element_type=F32)


def _ret_in_body(h_ref, g_ref, w_ref, cos_ref, sin_ref, kscale_ref, q_ref, kd_ref, v_ref, sg_ref, *, heads, dk, dv,
                 rows=slice(None)):
    xn = (_rms_rows(h_ref[rows, :]) * g_ref[...]).astype(BF16)
    cos = cos_ref[rows, :]
    sin = sin_ref[rows, :]
    half = dk // 2
    qk_w = heads * dk
    v_w = heads * dv

    def proj(lo, width):
        return _dot(xn, w_ref[:, lo:lo + width])

    for base, ref, scale_ref in ((0, q_ref, None), (qk_w, kd_ref, kscale_ref)):
        p = proj(base, qk_w)
        for hh in range(heads):
            lo, mid, hi = hh * dk, hh * dk + half, (hh + 1) * dk
            x1 = p[:, lo:mid]
            x2 = p[:, mid:hi]
            o1 = x1 * cos - x2 * sin
            o2 = x2 * cos + x1 * sin
            if scale_ref is not None:
                o1 = o1 * scale_ref[rows, lo:mid]
                o2 = o2 * scale_ref[rows, mid:hi]
            ref[hh, rows, :half] = o1.astype(ref.dtype)
            ref[hh, rows, half:] = o2.astype(ref.dtype)
    cw = min(COL_CHUNK, v_w)
    per_chunk = cw // dv
    for c in range(v_w // cw):
        v = proj(2 * qk_w + c * cw, cw)
        for j in range(per_chunk):
            v_ref[c * per_chunk + j, rows, :] = v[:, j * dv:(j + 1) * dv].astype(v_ref.dtype)
    for c in range(v_w // cw):
        gate = proj(2 * qk_w + v_w + c * cw, cw)
        sg = gate * jax.nn.sigmoid(gate)
        for j in range(per_chunk):
            sg_ref[c * per_chunk + j, rows, :] = sg[:, j * dv:(j + 1) * dv].astype(sg_ref.dtype)


def _ret_in(h, g, w_in, cos, sin, kscale, *, heads, dk, dv, out_dtype):
    n, d = h.shape
    tm = min(TOKEN_TILE, n)
    qk_w, v_w = heads * dk, heads * dv
    pos_tiles = cos.shape[0] // tm
    row = lambda i: (i, 0)
    tab = lambda i: (i % pos_tiles, 0)
    by_head = lambda i: (0, i, 0)
    vmem = (2 * _nbytes((tm, d), F32) + _nbytes(w_in.shape, BF16) + 4 * _nbytes((tm, dk // 2), F32)
            + _nbytes((tm, qk_w), F32) + 2 * _nbytes((tm, 2 * qk_w + 2 * v_w), out_dtype)
            + 4 * _nbytes((tm, COL_CHUNK), F32))
    return pl.pallas_call(
        functools.partial(_ret_in_body, heads=heads, dk=dk, dv=dv),
        grid=(n // tm,),
        in_specs=[pl.BlockSpec((tm, d), row), _resident((1, d)), _resident(w_in.shape),
                  pl.BlockSpec((tm, dk // 2), tab), pl.BlockSpec((tm, dk // 2), tab), _resident((tm, qk_w))],
        out_specs=[pl.BlockSpec((heads, tm, dk), by_head), pl.BlockSpec((heads, tm, dk), by_head),
                   pl.BlockSpec((heads, tm, dv), by_head), pl.BlockSpec((heads, tm, dv), by_head)],
        out_shape=[jax.ShapeDtypeStruct((heads, n, dk), out_dtype), jax.ShapeDtypeStruct((heads, n, dk), out_dtype),
                   jax.ShapeDtypeStruct((heads, n, dv), out_dtype), jax.ShapeDtypeStruct((heads, n, dv), out_dtype)],
        compiler_params=_params(("parallel",), vmem),
        name="ret_in",
    )(h, g, w_in, cos, sin, kscale)


def _ret_prompt_body(lg_ref, h_ref, g_ref, w_ref, cos_ref, sin_ref, kscale_ref, o_ref, s_out_ref,
                     q_s, kd_s, v_s, sg_s, s_ref, *, heads, dk, dv, chunk):
    t = pl.program_id(1)

    @pl.when(t == 0)
    def _():
        s_ref[...] = jnp.zeros_like(s_ref)

    tm = h_ref.shape[0]
    for c in range(tm // chunk):
        _ret_in_body(h_ref, g_ref, w_ref, cos_ref, sin_ref, kscale_ref, q_s, kd_s, v_s, sg_s,
                     heads=heads, dk=dk, dv=dv, rows=pl.ds(c * chunk, chunk))

    ri = lax.broadcasted_iota(jnp.int32, (chunk, chunk), 0)
    ci = lax.broadcasted_iota(jnp.int32, (chunk, chunk), 1)
    lower = (ri >= ci).astype(F32)
    row_v = lax.broadcasted_iota(jnp.int32, (chunk, dv), 0).astype(F32)
    causal, q_decay, chunk_decay = [], [], []
    for hh in range(heads):
        lg = lg_ref[hh]
        causal.append(lower * jnp.exp(jnp.full((1, chunk), -lg * chunk, F32)))
        q_decay.append(jnp.exp(lg * (row_v + 1.0)))
        chunk_decay.append(jnp.exp(jnp.full((1, dv), lg * chunk, F32)))

    for c in range(tm // chunk):
        rows = pl.ds(c * chunk, chunk)
        qk = [_dot_nt(q_s[hh, rows, :], kd_s[hh, rows, :]) for hh in range(heads)]
        grow = [_dot_tn(kd_s[hh, rows, :], v_s[hh, rows, :]) for hh in range(heads)]
        for hh in range(heads):
            s_prev = s_ref[hh]
            lhs = jnp.concatenate([(qk[hh] * causal[hh]).astype(BF16), q_s[hh, rows, :]], axis=1)
            rhs = jnp.concatenate([v_s[hh, rows, :], s_prev.astype(BF16)], axis=0)
            o = _rms_rows(q_decay[hh] * _dot(lhs, rhs))
            s_ref[hh] = chunk_decay[hh] * s_prev + grow[hh]
            o_ref[rows, hh * dv:(hh + 1) * dv] = (o * sg_s[hh, rows, :].astype(F32)).astype(o_ref.dtype)

    @pl.when(t == pl.num_programs(1) - 1)
    def _():
        s_out_ref[0] = s_ref[...]


def _ret_chunk(seq):
    return RET_KERNEL_CHUNK if seq % RET_KERNEL_CHUNK == 0 else seq


def _ret_prompt(log_g, h, g, w_in, cos, sin, kscale, *, batch, heads, dk, dv):
    n, d = h.shape
    seq = n // batch
    tm = min(TOKEN_TILE, seq)
    chunk = _ret_chunk(seq)
    assert tm % chunk == 0 and seq % tm == 0
    nt = seq // tm
    qk_w, v_w = heads * dk, heads * dv
    row = lambda b, t: (b * nt + t, 0)
    tab = lambda b, t: (t, 0)
    vmem = (2 * _nbytes((tm, d), F32) + _nbytes(w_in.shape, BF16) + 4 * _nbytes((tm, dk // 2), F32)
            + _nbytes((tm, qk_w), F32) + 2 * _nbytes((tm, v_w), BF16) + _nbytes((tm, 2 * qk_w + 2 * v_w), BF16)
            + 3 * _nbytes((heads, dk, dv), F32) + 4 * _nbytes((tm, COL_CHUNK), F32)
            + 2 * heads * (_nbytes((chunk, chunk), F32) + _nbytes((dk, dv), F32) + _nbytes((chunk, dv), F32)))
    return pl.pallas_call(
        functools.partial(_ret_prompt_body, heads=heads, dk=dk, dv=dv, chunk=chunk),
        grid=(batch, nt),
        in_specs=[pl.BlockSpec(memory_space=pltpu.SMEM),
                  pl.BlockSpec((tm, d), row), _resident((1, d)), _resident(w_in.shape),
                  pl.BlockSpec((tm, dk // 2), tab), pl.BlockSpec((tm, dk // 2), tab), _resident((tm, qk_w))],
        out_specs=[pl.BlockSpec((tm, v_w), row),
                   pl.BlockSpec((1, heads, dk, dv), lambda b, t: (b, 0, 0, 0))],
        out_shape=[jax.ShapeDtypeStruct((n, v_w), BF16),
                   jax.ShapeDtypeStruct((batch, heads, dk, dv), F32)],
        scratch_shapes=[pltpu.VMEM((heads, tm, dk), BF16), pltpu.VMEM((heads, tm, dk), BF16),
                        pltpu.VMEM((heads, tm, dv), BF16), pltpu.VMEM((heads, tm, dv), BF16),
                        pltpu.VMEM((heads, dk, dv), F32)],
        compiler_params=_params(("parallel", "arbitrary"), vmem),
        name="ret_prompt",
    )(log_g, h, g, w_in, cos, sin, kscale)


def _ret_sample_body(lg_ref, q_ref, kd_ref, v_ref, sg_ref, s_in_ref, o_ref, s_out_ref, *, seq):
    heads, rows, dk = q_ref.shape
    dv = v_ref.shape[2]
    group = rows // seq
    ri = lax.broadcasted_iota(jnp.int32, (rows, rows), 0)
    ci = lax.broadcasted_iota(jnp.int32, (rows, rows), 1)
    visible = ((ri // seq) == (ci // seq)) & (ri >= ci)
    row_v = lax.broadcasted_iota(jnp.int32, (rows, dv), 0)
    row_k = lax.broadcasted_iota(jnp.int32, (rows, dk), 0)
    for hh in range(heads):
        lg = lg_ref[hh]
        causal = jnp.where(visible, jnp.exp(jnp.full((rows, rows), -lg * seq, F32)), 0.0)
        q_decay = jnp.exp(lg * ((row_v % seq).astype(F32) + 1.0))
        chunk_decay = jnp.exp(jnp.full((1, dv), lg * seq, F32))
        q = q_ref[hh].astype(BF16)
        kd = kd_ref[hh]
        v = v_ref[hh].astype(BF16)
        o = _dot((_dot_nt(q, kd.astype(BF16)) * causal).astype(BF16), v)
        for g in range(group):
            s_prev = s_in_ref[g, hh]
            o = jnp.where((row_v // seq) == g, o + _dot(q, s_prev.astype(BF16)), o)
            kd_g = jnp.where((row_k // seq) == g, kd, 0.0).astype(BF16)
            s_out_ref[g, hh] = chunk_decay * s_prev + _dot_tn(kd_g, v)
        o = _rms_rows(q_decay * o)
        o_ref[hh] = (o * sg_ref[hh]).astype(o_ref.dtype)


def _ret_sample(log_g, q, kd, v, sg, state, *, seq):
    heads, n, dk = q.shape
    dv = v.shape[2]
    batch = n // seq
    group = SAMPLE_GROUP if batch % SAMPLE_GROUP == 0 else batch
    rows = group * seq
    by_head = lambda i: (0, i, 0)
    st = lambda i: (i, 0, 0, 0)
    vmem = (4 * _nbytes((group, heads, dk, dv), F32) + 8 * _nbytes((rows, heads * dv), F32)
            + 4 * _nbytes((dk, dv), F32))
    return pl.pallas_call(
        functools.partial(_ret_sample_body, seq=seq),
        grid=(batch // group,),
        in_specs=[pl.BlockSpec(memory_space=pltpu.SMEM),
                  pl.BlockSpec((heads, rows, dk), by_head), pl.BlockSpec((heads, rows, dk), by_head),
                  pl.BlockSpec((heads, rows, dv), by_head), pl.BlockSpec((heads, rows, dv), by_head),
                  pl.BlockSpec((group, heads, dk, dv), st)],
        out_specs=[pl.BlockSpec((heads, rows, dv), by_head), pl.BlockSpec((group, heads, dk, dv), st)],
        out_shape=[jax.ShapeDtypeStruct((heads, n, dv), BF16 if rows % BF16_SUBLANES == 0 else F32),
                   jax.ShapeDtypeStruct((batch, heads, dk, dv), F32)],
        compiler_params=_params(("parallel",), vmem),
        name="ret_sample",
    )(log_g, q, kd, v, sg, state)


def _out_ffn_body(o_ref, h_ref, o2_ref, h2_ref, wo_ref, g_post_ref, g_pre_ref, w1_ref, w2_ref, g_ffn_ref,
                  y_ref, y2_ref, *, o_layout, o2_layout):
    weights = (wo_ref, g_post_ref, g_pre_ref, w1_ref, w2_ref, g_ffn_ref)
    last = pl.program_id(0) == pl.num_programs(0) - 1
    pl.when(jnp.logical_not(last))(lambda: _out_ffn_tile(o_ref, h_ref, *weights, y_ref, o_layout=o_layout))
    pl.when(last)(lambda: _out_ffn_tile(o2_ref, h2_ref, *weights, y2_ref, o_layout=o2_layout))


def _out_ffn_tile(o_ref, h_ref, wo_ref, g_post_ref, g_pre_ref, w1_ref, w2_ref, g_ffn_ref, y_ref, *, o_layout):
    tm = h_ref.shape[0]
    parts = OUT_FFN_ROW_PARTS if tm % (OUT_FFN_ROW_PARTS * BF16_SUBLANES) == 0 else 1
    rp = tm // parts
    d_ff = w1_ref.shape[2]
    fc = min(COL_CHUNK, d_ff)

    def mixer_out(p):
        rows = slice(p * rp, (p + 1) * rp)
        if o_layout == "features":
            return _dot_tn(o_ref[0, :, rows], wo_ref[...])
        if o_layout == "heads":
            o = jnp.concatenate([o_ref[hh, rows, :] for hh in range(o_ref.shape[0])], axis=1)
            return _dot(o.astype(BF16), wo_ref[...])
        return _dot(o_ref[rows, :].astype(BF16), wo_ref[...])

    a = [mixer_out(p) for p in range(parts)]
    h1, x = [], []
    for p in range(parts):
        rows = slice(p * rp, (p + 1) * rp)
        h1.append(h_ref[rows, :] + _rms_rows(a[p]) * g_post_ref[...])
        x.append((_rms_rows(h1[p]) * g_pre_ref[...]).astype(BF16))
    acc = [jnp.zeros((rp, h_ref.shape[1]), F32) for _ in range(parts)]
    for c in range(d_ff // fc):
        for p in range(parts):
            u = jnp.maximum(_dot(x[p], w1_ref[0, :, c * fc:(c + 1) * fc]), 0.0)
            acc[p] = acc[p] + _dot((u * u).astype(BF16), w2_ref[0, c * fc:(c + 1) * fc, :])
    for p in range(parts):
        rows = slice(p * rp, (p + 1) * rp)
        y_ref[rows, :] = h1[p] + _rms_rows(acc[p]) * g_ffn_ref[...]


def _out_ffn(o, h, o2, h2, w_o, g_post, g_pre, w1, w2, g_ffn, *, layer, o_layout, o2_layout):
    n, d = h.shape
    n2 = h2.shape[0]
    one_layer = lambda w: pl.BlockSpec((1,) + w.shape[1:], lambda i: (layer, 0, 0), pipeline_mode=pl.Buffered(1))
    kdim = w_o.shape[0]
    tm = min(TOKEN_TILE, n)
    tiles_n = n // tm
    tile = lambda i: jnp.minimum(i, tiles_n - 1)
    row = lambda i: (tile(i), 0)
    if o_layout == "features":
        tiles = o.shape[2] // tm
        o_spec = pl.BlockSpec((1, kdim, tm), lambda i: (tile(i) // tiles, 0, tile(i) % tiles))
    elif o_layout == "heads":
        o_spec = pl.BlockSpec((o.shape[0], tm, o.shape[2]), lambda i: (0, tile(i), 0))
    else:
        o_spec = pl.BlockSpec((tm, kdim), row)
    assert o2_layout in ("heads", "tokens") and n2 <= tm
    vmem = (2 * _nbytes((tm, kdim), o.dtype) + 4 * _nbytes((tm, d), F32) + _nbytes(w_o.shape, BF16)
            + _nbytes(w1.shape[1:], BF16) + _nbytes(w2.shape[1:], BF16) + 6 * _nbytes((tm, COL_CHUNK), F32)
            + _nbytes(o2.shape, o2.dtype) + 3 * _nbytes((n2, d), F32))
    return pl.pallas_call(
        functools.partial(_out_ffn_body, o_layout=o_layout, o2_layout=o2_layout),
        grid=(tiles_n + 1,),
        in_specs=[o_spec, pl.BlockSpec((tm, d), row), _resident(o2.shape), _resident((n2, d)), _resident(w_o.shape),
                  _resident((1, d)), _resident((1, d)), one_layer(w1), one_layer(w2),
                  _resident((1, d))],
        out_specs=[pl.BlockSpec((tm, d), row), pl.BlockSpec((n2, d), lambda i: (0, 0))],
        out_shape=[jax.ShapeDtypeStruct((n, d), F32), jax.ShapeDtypeStruct((n2, d), F32)],
        compiler_params=_params(("arbitrary",), vmem),
        name="out_ffn",
    )(o, h, o2, h2, w_o, g_post, g_pre, w1, w2, g_ffn)


def _partial_rope(x, c_tab, sa_tab, sb_tab):
    half = ROT_DIM // 2
    outs = []
    for j in range(x.shape[1] // LANES):
        s = x[:, j * LANES:(j + 1) * LANES]
        outs.append(s * c_tab + pltpu.roll(s, LANES - half, axis=1) * sa_tab + pltpu.roll(s, half, axis=1) * sb_tab)
    return outs


def _swa_in_body(h_ref, g_q_ref, g_kv_ref, wq_ref, wkv_ref, c_ref, sa_ref, sb_ref, q_ref, k_ref, v_ref):
    y = _rms_rows(h_ref[...])
    xq = (y * g_q_ref[...]).astype(BF16)
    xkv = (y * g_kv_ref[...]).astype(BF16)
    c_tab, sa_tab, sb_tab = c_ref[...], sa_ref[...], sb_ref[...]
    q = _dot(xq, wq_ref[...]) * (SWA_HEAD_DIM ** -0.5 * LOG2E)
    for j, s in enumerate(_partial_rope(q, c_tab, sa_tab, sb_tab)):
        q_ref[:, j * LANES:(j + 1) * LANES] = s.astype(q_ref.dtype)
    kv = _dot(xkv, wkv_ref[...])
    kw = k_ref.shape[1]
    for j, s in enumerate(_partial_rope(kv[:, :kw], c_tab, sa_tab, sb_tab)):
        k_ref[:, j * LANES:(j + 1) * LANES] = s
    v_ref[...] = kv[:, kw:]


def _swa_in(h, g_q, g_kv, w_q, w_kv, c_tab, sa_tab, sb_tab, *, q_dtype):
    n, d = h.shape
    tm = min(TOKEN_TILE, n)
    qw = w_q.shape[1]
    kw = w_kv.shape[1] // 2
    pos_tiles = c_tab.shape[0] // tm
    row = lambda i: (i, 0)
    tab = lambda i: (i % pos_tiles, 0)
    vmem = (2 * _nbytes((tm, d), F32) + _nbytes(w_q.shape, BF16) + _nbytes(w_kv.shape, BF16)
            + 6 * _nbytes((tm, LANES), F32) + 2 * _nbytes((tm, qw), q_dtype) + 4 * _nbytes((tm, kw), F32)
            + 6 * _nbytes((tm, qw), F32))
    return pl.pallas_call(
        _swa_in_body,
        grid=(n // tm,),
        in_specs=[pl.BlockSpec((tm, d), row), _resident((1, d)), _resident((1, d)),
                  _resident(w_q.shape), _resident(w_kv.shape),
                  pl.BlockSpec((tm, LANES), tab), pl.BlockSpec((tm, LANES), tab), pl.BlockSpec((tm, LANES), tab)],
        out_specs=[pl.BlockSpec((tm, qw), row), pl.BlockSpec((tm, kw), row), pl.BlockSpec((tm, kw), row)],
        out_shape=[jax.ShapeDtypeStruct((n, qw), q_dtype), jax.ShapeDtypeStruct((n, kw), F32),
                   jax.ShapeDtypeStruct((n, kw), F32)],
        compiler_params=_params(("parallel",), vmem),
        name="swa_in",
    )(h, g_q, g_kv, w_q, w_kv, c_tab, sa_tab, sb_tab)


def _swa_in_t_body(h_ref, g_q_ref, g_kv_ref, wqt_ref, wk_ref, wvt_ref, wv_ref, cos_t_ref, sin_t_ref,
                   c_ref, sa_ref, sb_ref, qt_ref, k_ref, vt_ref, kwin_ref, vwin_ref, *, tiles):
    tm = h_ref.shape[0]
    hd = SWA_HEAD_DIM
    half = ROT_DIM // 2
    win = kwin_ref.shape[0]
    parts = SWA_ROW_PARTS if tm % (SWA_ROW_PARTS * LANES) == 0 else 1
    rp = tm // parts
    assert rp >= win
    for p in range(parts):
        rows = slice(p * rp, (p + 1) * rp)
        y = _rms_rows(h_ref[rows, :])
        xq = (y * g_q_ref[...]).astype(BF16)
        xkv = (y * g_kv_ref[...]).astype(BF16)
        cos_t, sin_t = cos_t_ref[:, rows], sin_t_ref[:, rows]
        qt = _dot_nt(wqt_ref[...], xq) * (hd ** -0.5 * LOG2E)
        for hq in range(qt.shape[0] // hd):
            base = hq * hd
            x1 = qt[base:base + half]
            x2 = qt[base + half:base + 2 * half]
            rot = jnp.concatenate([x1 * cos_t - x2 * sin_t, x2 * cos_t + x1 * sin_t], axis=0)
            qt_ref[0, base:base + 2 * half, rows] = rot.astype(qt_ref.dtype)
            qt_ref[0, base + 2 * half:base + hd, rows] = qt[base + 2 * half:base + hd].astype(qt_ref.dtype)
        k_rot = _partial_rope(_dot(xkv, wk_ref[...]), c_ref[rows, :], sa_ref[rows, :], sb_ref[rows, :])
        for j, s in enumerate(k_rot):
            k_ref[rows, j * LANES:(j + 1) * LANES] = s.astype(k_ref.dtype)
        vt_ref[0, :, rows] = _dot_nt(wvt_ref[...], xkv).astype(vt_ref.dtype)
        if p == parts - 1:
            @pl.when(pl.program_id(0) % tiles == tiles - 1)
            def _():
                for j, s in enumerate(k_rot):
                    kwin_ref[:, j * LANES:(j + 1) * LANES] = s[rp - win:, :]
                vwin_ref[...] = _dot(xkv[rp - win:, :], wv_ref[...])


def _swa_in_t(h, g_q, g_kv, w_q, w_kv, cos_t, sin_t, c_tab, sa_tab, sb_tab, *, batch, win):
    n, d = h.shape
    seq = n // batch
    tm = min(SWA_TOKEN_TILE, seq)
    tiles = seq // tm
    qw = w_q.shape[1]
    kw = w_kv.shape[1] // 2
    wqt = w_q.T
    wk, wv = w_kv[:, :kw], w_kv[:, kw:]
    wvt = wv.T
    row = lambda i: (i, 0)
    tab = lambda i: (i % tiles, 0)
    tab_t = lambda i: (0, i % tiles)
    feat = lambda i: (i // tiles, 0, i % tiles)
    per_seq = lambda i: (i // tiles, 0)
    half = ROT_DIM // 2
    vmem = (2 * _nbytes((tm, d), F32) + 2 * _nbytes(w_q.shape, BF16) + 3 * _nbytes(w_kv.shape, BF16)
            + 8 * _nbytes((tm, LANES), F32) + 2 * _nbytes((tm, qw + 2 * kw), BF16) + 4 * _nbytes((win, kw), F32)
            + 4 * _nbytes((tm, qw), F32))
    return pl.pallas_call(
        functools.partial(_swa_in_t_body, tiles=tiles),
        grid=(n // tm,),
        in_specs=[pl.BlockSpec((tm, d), row), _resident((1, d)), _resident((1, d)),
                  _resident(wqt.shape), _resident(wk.shape), _resident(wvt.shape), _resident(wv.shape),
                  pl.BlockSpec((half, tm), tab_t), pl.BlockSpec((half, tm), tab_t),
                  pl.BlockSpec((tm, LANES), tab), pl.BlockSpec((tm, LANES), tab), pl.BlockSpec((tm, LANES), tab)],
        out_specs=[pl.BlockSpec((1, qw, tm), feat), pl.BlockSpec((tm, kw), row), pl.BlockSpec((1, kw, tm), feat),
                   pl.BlockSpec((win, kw), per_seq), pl.BlockSpec((win, kw), per_seq)],
        out_shape=[jax.ShapeDtypeStruct((batch, qw, seq), BF16), jax.ShapeDtypeStruct((n, kw), BF16),
                   jax.ShapeDtypeStruct((batch, kw, seq), BF16),
                   jax.ShapeDtypeStruct((batch * win, kw), F32), jax.ShapeDtypeStruct((batch * win, kw), F32)],
        compiler_params=_params(("arbitrary",), vmem),
        name="swa_in_t",
    )(h, g_q, g_kv, wqt, wk, wvt, wv, cos_t, sin_t, c_tab, sa_tab, sb_tab)


def _attn_prompt_body(sinks_ref, mask_ref, qt_ref, kp_ref, kc_ref, vtp_ref, vtc_ref, ot_ref, *, group):
    blk = kp_ref.shape[0]
    nblk = kc_ref.shape[0] // blk
    hd = SWA_HEAD_DIM
    kvh_n = kc_ref.shape[1] // hd
    cols = group * blk
    kj = lax.broadcasted_iota(jnp.int32, (blk, cols), 0)
    qi = lax.broadcasted_iota(jnp.int32, (blk, cols), 1) % blk
    own = kj <= qi
    lane_head = lax.broadcasted_iota(jnp.int32, (1, cols), 1) // blk
    ones_rows = jnp.ones((BF16_SUBLANES, blk), BF16)
    has_prev = pl.program_id(1) > 0

    def scores(j, kvh):
        tile, lo = divmod(kvh * hd, LANES)
        q4t = jnp.concatenate([qt_ref[0, (kvh * group + g) * hd:(kvh * group + g + 1) * hd, j * blk:(j + 1) * blk]
                               for g in range(group)], axis=1)
        rhs = jnp.concatenate([q4t if part * hd == lo else jnp.zeros_like(q4t) for part in range(LANES // hd)], axis=0)
        k_tile = slice(tile * LANES, (tile + 1) * LANES)
        k_prev = kp_ref[:, k_tile] if j == 0 else kc_ref[(j - 1) * blk:j * blk, k_tile]
        return _dot(kc_ref[j * blk:(j + 1) * blk, k_tile], rhs), _dot(k_prev, rhs)

    def finish(j, kvh, s_own, s_prev):
        if j == 0:
            s_prev = jnp.where(has_prev, s_prev, NEG)
        s = jnp.where(own, s_own, s_prev)
        sink = jnp.full((1, cols), sinks_ref[kvh * group] * LOG2E, F32)
        for g in range(1, group):
            sink = jnp.where(lane_head == g, sinks_ref[kvh * group + g] * LOG2E, sink)
        m = jnp.maximum(jnp.max(s, axis=0, keepdims=True), sink)
        e = jnp.exp2(s - m).astype(BF16)
        p_own = e * mask_ref[...]
        p = jnp.concatenate([p_own, e - p_own], axis=0)
        head_rows = slice(kvh * hd, (kvh + 1) * hd)
        vt_own = vtc_ref[0, head_rows, j * blk:(j + 1) * blk]
        vt_prev = vtp_ref[0, head_rows, :] if j == 0 else vtc_ref[0, head_rows, (j - 1) * blk:j * blk]
        vt = jnp.concatenate([jnp.concatenate([vt_own, ones_rows], axis=0),
                              jnp.concatenate([vt_prev, ones_rows], axis=0)], axis=1)
        acc = _dot(vt, p)
        denom = acc[hd:hd + 1, :] + jnp.exp2(sink - m)
        ot = acc[:hd] / denom
        for g in range(group):
            hq = kvh * group + g
            ot_ref[0, hq * hd:(hq + 1) * hd, j * blk:(j + 1) * blk] = ot[:, g * blk:(g + 1) * blk].astype(ot_ref.dtype)

    units = [(j, kvh) for j in range(nblk) for kvh in range(kvh_n)]
    queue = [scores(*unit) for unit in units[:ATTN_LOOKAHEAD]]
    for idx, unit in enumerate(units):
        if idx + ATTN_LOOKAHEAD < len(units):
            queue.append(scores(*units[idx + ATTN_LOOKAHEAD]))
        finish(*unit, *queue.pop(0))


def _attn_prompt(sinks, qt, k, vt):
    batch, qw, seq = qt.shape
    kw = k.shape[1]
    blk = WINDOW
    nblk = ATTN_BLOCKS_PER_STEP if seq % (ATTN_BLOCKS_PER_STEP * blk) == 0 else 1
    span = nblk * blk
    steps = seq // span
    group = qw // kw
    cols = group * blk
    own = (jnp.arange(blk)[:, None] <= (jnp.arange(cols) % blk)[None, :]).astype(BF16)
    cur_t = lambda b, i: (b, 0, i)
    prev_t = lambda b, i: (b, 0, jnp.maximum(i * nblk - 1, 0))
    cur = lambda b, i: (b * steps + i, 0)
    prev = lambda b, i: (b * steps * nblk + jnp.maximum(i * nblk - 1, 0), 0)
    vmem = (4 * _nbytes((qw, span), BF16) + 6 * _nbytes((span, kw), BF16) + 16 * _nbytes((blk, cols), F32))
    return pl.pallas_call(
        functools.partial(_attn_prompt_body, group=group),
        grid=(batch, steps),
        in_specs=[pl.BlockSpec(memory_space=pltpu.SMEM), _resident((blk, cols)), pl.BlockSpec((1, qw, span), cur_t),
                  pl.BlockSpec((blk, kw), prev), pl.BlockSpec((span, kw), cur),
                  pl.BlockSpec((1, kw, blk), prev_t), pl.BlockSpec((1, kw, span), cur_t)],
        out_specs=pl.BlockSpec((1, qw, span), cur_t),
        out_shape=jax.ShapeDtypeStruct((batch, qw, seq), BF16),
        compiler_params=_params(("parallel", "parallel"), vmem),
        name="attn_prompt",
    )(sinks, own, qt, k, k, vt, vt)


def _attn_sample_body(sink_ref, q_ref, kn_ref, vn_ref, kc_ref, vc_ref, o_ref, kw_ref, vw_ref, *, seq, group, q_start):
    rows = q_ref.shape[0]
    nseq = rows // seq
    win = kc_ref.shape[1]
    hd = SWA_HEAD_DIM
    kw = kn_ref.shape[1]
    heads = q_ref.shape[1] // kw
    srows = heads * rows
    lhs = jnp.concatenate([q_ref[:, hq * kw:(hq + 1) * kw] for hq in range(heads)], axis=0)
    kn = kn_ref[...]
    vn = vn_ref[...]
    sink = sink_ref[...]

    r_c = lax.broadcasted_iota(jnp.int32, (srows, win), 0) % rows
    c_c = lax.broadcasted_iota(jnp.int32, (srows, win), 1)
    seq_c = r_c // seq
    rel_c = (r_c % seq) + win - c_c
    ok_c = (rel_c >= 0) & (rel_c < WINDOW) & (q_start - win + c_c >= 0)
    s_c = _dot_nt(lhs, kc_ref[0].astype(BF16))
    for b in range(1, nseq):
        s_c = jnp.where(seq_c == b, _dot_nt(lhs, kc_ref[b].astype(BF16)), s_c)
    s_c = jnp.where(ok_c, s_c, NEG)

    r_n = lax.broadcasted_iota(jnp.int32, (srows, rows), 0) % rows
    c_n = lax.broadcasted_iota(jnp.int32, (srows, rows), 1)
    rel_n = (r_n % seq) - (c_n % seq)
    ok_n = (rel_n >= 0) & (rel_n < WINDOW) & ((r_n // seq) == (c_n // seq))
    s_n = jnp.where(ok_n, _dot_nt(lhs, kn.astype(BF16)), NEG)

    m = jnp.maximum(jnp.maximum(jnp.max(s_c, axis=-1, keepdims=True), jnp.max(s_n, axis=-1, keepdims=True)), sink)
    e_c = jnp.exp2(s_c - m)
    e_n = jnp.exp2(s_n - m)
    denom = jnp.sum(e_c, axis=-1, keepdims=True) + jnp.sum(e_n, axis=-1, keepdims=True) + jnp.exp2(sink - m)
    acc = _dot(e_n.astype(BF16), vn.astype(BF16))
    for b in range(nseq):
        acc = acc + _dot(jnp.where(seq_c == b, e_c, 0.0).astype(BF16), vc_ref[b].astype(BF16))
    o = acc / denom
    for hq in range(heads):
        kvh = hq // group
        o_ref[:, hq * hd:(hq + 1) * hd] = o[hq * rows:(hq + 1) * rows, kvh * hd:(kvh + 1) * hd].astype(o_ref.dtype)
    for b in range(nseq):
        kw_ref[b, 0:win - seq, :] = kc_ref[b, seq:win, :]
        kw_ref[b, win - seq:win, :] = kn[b * seq:(b + 1) * seq, :]
        vw_ref[b, 0:win - seq, :] = vc_ref[b, seq:win, :]
        vw_ref[b, win - seq:win, :] = vn[b * seq:(b + 1) * seq, :]


def _attn_sample(sinks, q_wide, k_new, v_new, k_cache, v_cache, *, seq, q_start):
    n, qww = q_wide.shape
    kw = k_new.shape[1]
    heads = qww // kw
    hd = SWA_HEAD_DIM
    group = heads // (kw // hd)
    batch, win, _ = k_cache.shape
    nseq = ATTN_SAMPLE_SEQS if batch % ATTN_SAMPLE_SEQS == 0 else batch
    rows = nseq * seq
    sink_rows = jnp.repeat(sinks * LOG2E, rows)[:, None]
    row = lambda i: (i, 0)
    cache = lambda i: (i, 0, 0)
    vmem = (8 * _nbytes((nseq, win, kw), F32) + 4 * _nbytes((rows, qww), BF16)
            + 16 * _nbytes((heads * rows, win + kw), F32))
    return pl.pallas_call(
        functools.partial(_attn_sample_body, seq=seq, group=group, q_start=q_start),
        grid=(batch // nseq,),
        in_specs=[_resident((heads * rows, 1)), pl.BlockSpec((rows, qww), row),
                  pl.BlockSpec((rows, kw), row), pl.BlockSpec((rows, kw), row),
                  pl.BlockSpec((nseq, win, kw), cache), pl.BlockSpec((nseq, win, kw), cache)],
        out_specs=[pl.BlockSpec((rows, heads * hd), row), pl.BlockSpec((nseq, win, kw), cache),
                   pl.BlockSpec((nseq, win, kw), cache)],
        out_shape=[jax.ShapeDtypeStruct((n, heads * hd), BF16), jax.ShapeDtypeStruct((batch, win, kw), F32),
                   jax.ShapeDtypeStruct((batch, win, kw), F32)],
        compiler_params=_params(("parallel",), vmem),
        name="attn_sample",
    )(sink_rows, q_wide, k_new, v_new, k_cache, v_cache)


def _ret_rope_tables(pos, dk):
    inv = 1.0 / (RET_ROPE_THETA ** jnp.linspace(0.0, 1.0, dk // 2, dtype=F32))
    ang = pos[:, None] * inv[None, :]
    return jnp.cos(ang), jnp.sin(ang)


def _ret_key_scale(log_g, seq, n, dk):
    chunk = _ret_chunk(seq)
    tm = min(TOKEN_TILE, n)
    assert tm % chunk == 0
    left = (chunk - 1 - jnp.arange(tm) % chunk).astype(F32)
    per_head = jnp.exp(log_g[None, :] * left[:, None]) * dk ** -0.5
    return jnp.repeat(per_head, dk, axis=1)


def _swa_cos_sin(pos):
    half = ROT_DIM // 2
    inv = ROPE_THETA ** (-jnp.arange(half, dtype=F32) / half)
    ang = pos[:, None] * inv[None, :]
    return jnp.cos(ang), jnp.sin(ang)


def _swa_rope_tables(pos):
    half = ROT_DIM // 2
    cos, sin = _swa_cos_sin(pos)
    n = pos.shape[0]
    pad = jnp.zeros((n, SWA_HEAD_DIM - 2 * half), F32)
    c_head = jnp.concatenate([cos, cos, pad + 1.0], axis=1)
    sa_head = jnp.concatenate([-sin, jnp.zeros_like(sin), pad], axis=1)
    sb_head = jnp.concatenate([jnp.zeros_like(sin), sin, pad], axis=1)
    reps = LANES // SWA_HEAD_DIM
    return tuple(jnp.tile(t, (1, reps)) for t in (c_head, sa_head, sb_head))


def _tile_rows(tab, seq, n):
    tm = min(TOKEN_TILE, n)
    return tab if seq >= tm else jnp.tile(tab, (tm // seq, 1))


def kernel(x_prompt, x_sample, state_ret, cache_k_win, cache_v_win, ret_norm_pre, ret_w_in, ret_w_out, ret_norm_post, kv_norm, w_kv, swa_norm_pre, swa_w_q, swa_sinks, swa_w_o, swa_norm_post, ffn_norm_pre, ffn_w1, ffn_w2, ffn_norm_post):
    n_a = DEPTH // 2
    assert n_a == 1 and DEPTH == 2, "one retention layer followed by one sliding-window layer"
    d = x_prompt.shape[-1]
    heads = RET_HEADS
    dk = ret_w_out.shape[-1] // heads
    dv = ret_w_out.shape[-2] // heads
    kvh, hd = SWA_KV_HEADS, SWA_HEAD_DIM
    row2 = lambda g: g.reshape(1, d)
    log_g = jnp.log1p(-jnp.exp2(-5.0 - jnp.arange(heads, dtype=F32)))

    w_in = ret_w_in[0].astype(BF16)
    w_out = ret_w_out[0].astype(BF16)
    wq = swa_w_q[0].astype(BF16)
    wkv = w_kv.astype(BF16)
    wo = swa_w_o[0].astype(BF16)
    w1 = ffn_w1.astype(BF16)
    w2 = ffn_w2.astype(BF16)
    sinks = swa_sinks[0]
    q_heads = wq.shape[1] // hd
    on_kv_head = (jnp.arange(q_heads)[:, None] // (q_heads // kvh) == jnp.arange(kvh)[None, :]).astype(BF16)
    wq_wide = (wq.reshape(d, q_heads, 1, hd) * on_kv_head[None, :, :, None]).reshape(d, q_heads * kvh * hd)

    b_p, t_p, _ = x_prompt.shape
    b_s, t_s, _ = x_sample.shape
    n_p, n_s = b_p * t_p, b_s * t_s
    pos_p = jnp.arange(t_p, dtype=F32)
    pos_s = PAST_LEN + jnp.arange(t_s, dtype=F32)
    h_p = x_prompt.reshape(n_p, d)
    h_s = x_sample.reshape(n_s, d)

    def ret_tables(pos, t, n):
        cos, sin = (_tile_rows(tab, t, n) for tab in _ret_rope_tables(pos, dk))
        return cos, sin, _ret_key_scale(log_g, t, n, dk)

    o_p, state_p = _ret_prompt(log_g, h_p, row2(ret_norm_pre[0]), w_in, *ret_tables(pos_p, t_p, n_p),
                               batch=b_p, heads=heads, dk=dk, dv=dv)
    q, kd, v, sg = _ret_in(h_s, row2(ret_norm_pre[0]), w_in, *ret_tables(pos_s, t_s, n_s),
                           heads=heads, dk=dk, dv=dv, out_dtype=F32)
    o_s, state_s = _ret_sample(log_g, q, kd, v, sg, state_ret[0], seq=t_s)
    h_p, h_s = _out_ffn(o_p, h_p, o_s, h_s, w_out, row2(ret_norm_post[0]), row2(ffn_norm_pre[0]), w1, w2,
                        row2(ffn_norm_post[0]), layer=0, o_layout="tokens", o2_layout="heads")

    w_p = min(WINDOW, t_p)
    cos, sin = _swa_cos_sin(pos_p)
    qt, k, vt, k_win_p, v_win_p = _swa_in_t(h_p, row2(swa_norm_pre[0]), row2(kv_norm), wq, wkv, cos.T, sin.T,
                                            *_swa_rope_tables(pos_p), batch=b_p, win=w_p)
    o_p = _attn_prompt(sinks, qt, k, vt)
    w_s = cache_k_win.shape[1]
    tabs = tuple(_tile_rows(tab, t_s, n_s) for tab in _swa_rope_tables(pos_s))
    q, k, v = _swa_in(h_s, row2(swa_norm_pre[0]), row2(kv_norm), wq_wide, wkv, *tabs, q_dtype=BF16)
    o_s, k_win_s, v_win_s = _attn_sample(sinks, q, k, v, cache_k_win.reshape(b_s, w_s, kvh * hd),
                                         cache_v_win.reshape(b_s, w_s, kvh * hd), seq=t_s, q_start=PAST_LEN)
    h_p, h_s = _out_ffn(o_p, h_p, o_s, h_s, wo, row2(swa_norm_post[0]), row2(ffn_norm_pre[1]), w1, w2,
                        row2(ffn_norm_post[1]), layer=1, o_layout="features", o2_layout="tokens")

    return (h_p.reshape(b_p, t_p, d), h_s.reshape(b_s, t_s, d), state_p[None], state_s[None],
            k_win_p.reshape(b_p, w_p, kvh, hd), v_win_p.reshape(b_p, w_p, kvh, hd),
            k_win_s.reshape(b_s, w_s, kvh, hd), v_win_s.reshape(b_s, w_s, kvh, hd))
```

```python
import functools

import jax
import jax.numpy as jnp
from jax import lax
from jax.experimental import pallas as pl
from jax.experimental.pallas import tpu as pltpu

DEPTH = 2
PAST_LEN = 16384
RET_HEADS = 4
RET_ROPE_THETA = 10000.0
SWA_HEAD_DIM = 64
SWA_KV_HEADS = 4
WINDOW = 128
ROPE_THETA = 500000.0
ROT_DIM = SWA_HEAD_DIM // 4
EPS = 1e-6
NEG = -1e30
LOG2E = 1.4426950408889634

LANES = 128
BF16_SUBLANES = 16
VMEM_CAP_BYTES = 64 * 1024 * 1024
VMEM_BUDGET_BYTES = VMEM_CAP_BYTES - 8 * 1024 * 1024
VMEM_FLOOR_BYTES = 16 * 1024 * 1024

TOKEN_TILE = 512
COL_CHUNK = 1024
RET_KERNEL_CHUNK = 256
SAMPLE_GROUP = 4
ATTN_SAMPLE_SEQS = 4
ATTN_BLOCKS_PER_STEP = 16
ATTN_LOOKAHEAD = 4
OUT_FFN_ROW_PARTS = 2
SWA_TOKEN_TILE = 1024
SWA_ROW_PARTS = 4

F32 = jnp.float32
BF16 = jnp.bfloat16


def _params(semantics, vmem_bytes):
    limit = int(min(max(vmem_bytes, VMEM_FLOOR_BYTES), VMEM_BUDGET_BYTES))
    return pltpu.CompilerParams(dimension_semantics=semantics, vmem_limit_bytes=limit)


def _resident(shape):
    nd = len(shape)
    return pl.BlockSpec(shape, lambda *_: (0,) * nd, pipeline_mode=pl.Buffered(1))


def _nbytes(shape, dtype):
    n = 1
    for s in shape:
        n *= s
    return n * jnp.dtype(dtype).itemsize


def _rms_rows(x):
    return x * lax.rsqrt(jnp.mean(x * x, axis=-1, keepdims=True) + EPS)


def _dot(a, b):
    return jnp.dot(a, b, preferred_element_type=F32)


def _dot_nt(a, b):
    return lax.dot_general(a, b, (((1,), (1,)), ((), ())), preferred_element_type=F32)


def _dot_tn(a, b):
    return lax.dot_general(a, b, (((0,), (0,)), ((), ())), preferred_element_type=F32)


def _ret_in_body(h_ref, g_ref, w_ref, cos_ref, sin_ref, kscale_ref, q_ref, kd_ref, v_ref, sg_ref, *, heads, dk, dv,
                 rows=slice(None)):
    xn = (_rms_rows(h_ref[rows, :]) * g_ref[...]).astype(BF16)
    cos = cos_ref[rows, :]
    sin = sin_ref[rows, :]
    half = dk // 2
    qk_w = heads * dk
    v_w = heads * dv

    def proj(lo, width):
        return _dot(xn, w_ref[:, lo:lo + width])

    for base, ref, scale_ref in ((0, q_ref, None), (qk_w, kd_ref, kscale_ref)):
        p = proj(base, qk_w)
        for hh in range(heads):
            lo, mid, hi = hh * dk, hh * dk + half, (hh + 1) * dk
            x1 = p[:, lo:mid]
            x2 = p[:, mid:hi]
            o1 = x1 * cos - x2 * sin
            o2 = x2 * cos + x1 * sin
            if scale_ref is not None:
                o1 = o1 * scale_ref[rows, lo:mid]
                o2 = o2 * scale_ref[rows, mid:hi]
            ref[hh, rows, :half] = o1.astype(ref.dtype)
            ref[hh, rows, half:] = o2.astype(ref.dtype)
    cw = min(COL_CHUNK, v_w)
    per_chunk = cw // dv
    for c in range(v_w // cw):
        v = proj(2 * qk_w + c * cw, cw)
        for j in range(per_chunk):
            v_ref[c * per_chunk + j, rows, :] = v[:, j * dv:(j + 1) * dv].astype(v_ref.dtype)
    for c in range(v_w // cw):
        gate = proj(2 * qk_w + v_w + c * cw, cw)
        sg = gate * jax.nn.sigmoid(gate)
        for j in range(per_chunk):
            sg_ref[c * per_chunk + j, rows, :] = sg[:, j * dv:(j + 1) * dv].astype(sg_ref.dtype)


def _ret_in(h, g, w_in, cos, sin, kscale, *, heads, dk, dv, out_dtype):
    n, d = h.shape
    tm = min(TOKEN_TILE, n)
    qk_w, v_w = heads * dk, heads * dv
    pos_tiles = cos.shape[0] // tm
    row = lambda i: (i, 0)
    tab = lambda i: (i % pos_tiles, 0)
    by_head = lambda i: (0, i, 0)
    vmem = (2 * _nbytes((tm, d), F32) + _nbytes(w_in.shape, BF16) + 4 * _nbytes((tm, dk // 2), F32)
            + _nbytes((tm, qk_w), F32) + 2 * _nbytes((tm, 2 * qk_w + 2 * v_w), out_dtype)
            + 4 * _nbytes((tm, COL_CHUNK), F32))
    return pl.pallas_call(
        functools.partial(_ret_in_body, heads=heads, dk=dk, dv=dv),
        grid=(n // tm,),
        in_specs=[pl.BlockSpec((tm, d), row), _resident((1, d)), _resident(w_in.shape),
                  pl.BlockSpec((tm, dk // 2), tab), pl.BlockSpec((tm, dk // 2), tab), _resident((tm, qk_w))],
        out_specs=[pl.BlockSpec((heads, tm, dk), by_head), pl.BlockSpec((heads, tm, dk), by_head),
                   pl.BlockSpec((heads, tm, dv), by_head), pl.BlockSpec((heads, tm, dv), by_head)],
        out_shape=[jax.ShapeDtypeStruct((heads, n, dk), out_dtype), jax.ShapeDtypeStruct((heads, n, dk), out_dtype),
                   jax.ShapeDtypeStruct((heads, n, dv), out_dtype), jax.ShapeDtypeStruct((heads, n, dv), out_dtype)],
        compiler_params=_params(("parallel",), vmem),
        name="ret_in",
    )(h, g, w_in, cos, sin, kscale)


def _ret_prompt_body(lg_ref, h_ref, g_ref, w_ref, cos_ref, sin_ref, kscale_ref, o_ref, s_out_ref,
                     q_s, kd_s, v_s, sg_s, s_ref, *, heads, dk, dv, chunk):
    t = pl.program_id(1)

    @pl.when(t == 0)
    def _():
        s_ref[...] = jnp.zeros_like(s_ref)

    tm = h_ref.shape[0]
    for c in range(tm // chunk):
        _ret_in_body(h_ref, g_ref, w_ref, cos_ref, sin_ref, kscale_ref, q_s, kd_s, v_s, sg_s,
                     heads=heads, dk=dk, dv=dv, rows=pl.ds(c * chunk, chunk))

    ri = lax.broadcasted_iota(jnp.int32, (chunk, chunk), 0)
    ci = lax.broadcasted_iota(jnp.int32, (chunk, chunk), 1)
    lower = (ri >= ci).astype(F32)
    row_v = lax.broadcasted_iota(jnp.int32, (chunk, dv), 0).astype(F32)
    causal, q_decay, chunk_decay = [], [], []
    for hh in range(heads):
        lg = lg_ref[hh]
        causal.append(lower * jnp.exp(jnp.full((1, chunk), -lg * chunk, F32)))
        q_decay.append(jnp.exp(lg * (row_v + 1.0)))
        chunk_decay.append(jnp.exp(jnp.full((1, dv), lg * chunk, F32)))

    for c in range(tm // chunk):
        rows = pl.ds(c * chunk, chunk)
        qk = [_dot_nt(q_s[hh, rows, :], kd_s[hh, rows, :]) for hh in range(heads)]
        grow = [_dot_tn(kd_s[hh, rows, :], v_s[hh, rows, :]) for hh in range(heads)]
        for hh in range(heads):
            s_prev = s_ref[hh]
            lhs = jnp.concatenate([(qk[hh] * causal[hh]).astype(BF16), q_s[hh, rows, :]], axis=1)
            rhs = jnp.concatenate([v_s[hh, rows, :], s_prev.astype(BF16)], axis=0)
            o = _rms_rows(q_decay[hh] * _dot(lhs, rhs))
            s_ref[hh] = chunk_decay[hh] * s_prev + grow[hh]
            o_ref[rows, hh * dv:(hh + 1) * dv] = (o * sg_s[hh, rows, :].astype(F32)).astype(o_ref.dtype)

    @pl.when(t == pl.num_programs(1) - 1)
    def _():
        s_out_ref[0] = s_ref[...]


def _ret_chunk(seq):
    return RET_KERNEL_CHUNK if seq % RET_KERNEL_CHUNK == 0 else seq


def _ret_prompt(log_g, h, g, w_in, cos, sin, kscale, *, batch, heads, dk, dv):
    n, d = h.shape
    seq = n // batch
    tm = min(TOKEN_TILE, seq)
    chunk = _ret_chunk(seq)
    assert tm % chunk == 0 and seq % tm == 0
    nt = seq // tm
    qk_w, v_w = heads * dk, heads * dv
    row = lambda b, t: (b * nt + t, 0)
    tab = lambda b, t: (t, 0)
    vmem = (2 * _nbytes((tm, d), F32) + _nbytes(w_in.shape, BF16) + 4 * _nbytes((tm, dk // 2), F32)
            + _nbytes((tm, qk_w), F32) + 2 * _nbytes((tm, v_w), BF16) + _nbytes((tm, 2 * qk_w + 2 * v_w), BF16)
            + 3 * _nbytes((heads, dk, dv), F32) + 4 * _nbytes((tm, COL_CHUNK), F32)
            + 2 * heads * (_nbytes((chunk, chunk), F32) + _nbytes((dk, dv), F32) + _nbytes((chunk, dv), F32)))
    return pl.pallas_call(
        functools.partial(_ret_prompt_body, heads=heads, dk=dk, dv=dv, chunk=chunk),
        grid=(batch, nt),
        in_specs=[pl.BlockSpec(memory_space=pltpu.SMEM),
                  pl.BlockSpec((tm, d), row), _resident((1, d)), _resident(w_in.shape),
                  pl.BlockSpec((tm, dk // 2), tab), pl.BlockSpec((tm, dk // 2), tab), _resident((tm, qk_w))],
        out_specs=[pl.BlockSpec((tm, v_w), row),
                   pl.BlockSpec((1, heads, dk, dv), lambda b, t: (b, 0, 0, 0))],
        out_shape=[jax.ShapeDtypeStruct((n, v_w), BF16),
                   jax.ShapeDtypeStruct((batch, heads, dk, dv), F32)],
        scratch_shapes=[pltpu.VMEM((heads, tm, dk), BF16), pltpu.VMEM((heads, tm, dk), BF16),
                        pltpu.VMEM((heads, tm, dv), BF16), pltpu.VMEM((heads, tm, dv), BF16),
                        pltpu.VMEM((heads, dk, dv), F32)],
        compiler_params=_params(("parallel", "arbitrary"), vmem),
        name="ret_prompt",
    )(log_g, h, g, w_in, cos, sin, kscale)


def _ret_sample_body(lg_ref, q_ref, kd_ref, v_ref, sg_ref, s_in_ref, o_ref, s_out_ref, *, seq):
    heads, rows, dk = q_ref.shape
    dv = v_ref.shape[2]
    group = rows // seq
    ri = lax.broadcasted_iota(jnp.int32, (rows, rows), 0)
    ci = lax.broadcasted_iota(jnp.int32, (rows, rows), 1)
    visible = ((ri // seq) == (ci // seq)) & (ri >= ci)
    row_v = lax.broadcasted_iota(jnp.int32, (rows, dv), 0)
    row_k = lax.broadcasted_iota(jnp.int32, (rows, dk), 0)
    for hh in range(heads):
        lg = lg_ref[hh]
        causal = jnp.where(visible, jnp.exp(jnp.full((rows, rows), -lg * seq, F32)), 0.0)
        q_decay = jnp.exp(lg * ((row_v % seq).astype(F32) + 1.0))
        chunk_decay = jnp.exp(jnp.full((1, dv), lg * seq, F32))
        q = q_ref[hh].astype(BF16)
        kd = kd_ref[hh]
        v = v_ref[hh].astype(BF16)
        o = _dot((_dot_nt(q, kd.astype(BF16)) * causal).astype(BF16), v)
        for g in range(group):
            s_prev = s_in_ref[g, hh]
            o = jnp.where((row_v // seq) == g, o + _dot(q, s_prev.astype(BF16)), o)
            kd_g = jnp.where((row_k // seq) == g, kd, 0.0).astype(BF16)
            s_out_ref[g, hh] = chunk_decay * s_prev + _dot_tn(kd_g, v)
        o = _rms_rows(q_decay * o)
        o_ref[hh] = (o * sg_ref[hh]).astype(o_ref.dtype)


def _ret_sample(log_g, q, kd, v, sg, state, *, seq):
    heads, n, dk = q.shape
    dv = v.shape[2]
    batch = n // seq
    group = SAMPLE_GROUP if batch % SAMPLE_GROUP == 0 else batch
    rows = group * seq
    by_head = lambda i: (0, i, 0)
    st = lambda i: (i, 0, 0, 0)
    vmem = (4 * _nbytes((group, heads, dk, dv), F32) + 8 * _nbytes((rows, heads * dv), F32)
            + 4 * _nbytes((dk, dv), F32))
    return pl.pallas_call(
        functools.partial(_ret_sample_body, seq=seq),
        grid=(batch // group,),
        in_specs=[pl.BlockSpec(memory_space=pltpu.SMEM),
                  pl.BlockSpec((heads, rows, dk), by_head), pl.BlockSpec((heads, rows, dk), by_head),
                  pl.BlockSpec((heads, rows, dv), by_head), pl.BlockSpec((heads, rows, dv), by_head),
                  pl.BlockSpec((group, heads, dk, dv), st)],
        out_specs=[pl.BlockSpec((heads, rows, dv), by_head), pl.BlockSpec((group, heads, dk, dv), st)],
        out_shape=[jax.ShapeDtypeStruct((heads, n, dv), BF16 if rows % BF16_SUBLANES == 0 else F32),
                   jax.ShapeDtypeStruct((batch, heads, dk, dv), F32)],
        compiler_params=_params(("parallel",), vmem),
        name="ret_sample",
    )(log_g, q, kd, v, sg, state)


def _out_ffn_body(o_ref, h_ref, o2_ref, h2_ref, wo_ref, g_post_ref, g_pre_ref, w1_ref, w2_ref, g_ffn_ref,
                  y_ref, y2_ref, *, o_layout, o2_layout):
    weights = (wo_ref, g_post_ref, g_pre_ref, w1_ref, w2_ref, g_ffn_ref)
    last = pl.program_id(0) == pl.num_programs(0) - 1
    pl.when(jnp.logical_not(last))(lambda: _out_ffn_tile(o_ref, h_ref, *weights, y_ref, o_layout=o_layout))
    pl.when(last)(lambda: _out_ffn_tile(o2_ref, h2_ref, *weights, y2_ref, o_layout=o2_layout))


def _out_ffn_tile(o_ref, h_ref, wo_ref, g_post_ref, g_pre_ref, w1_ref, w2_ref, g_ffn_ref, y_ref, *, o_layout):
    tm = h_ref.shape[0]
    parts = OUT_FFN_ROW_PARTS if tm % (OUT_FFN_ROW_PARTS * BF16_SUBLANES) == 0 else 1
    rp = tm // parts
    d_ff = w1_ref.shape[2]
    fc = min(COL_CHUNK, d_ff)

    def mixer_out(p):
        rows = slice(p * rp, (p + 1) * rp)
        if o_layout == "features":
            return _dot_tn(o_ref[0, :, rows], wo_ref[...])
        if o_layout == "heads":
            o = jnp.concatenate([o_ref[hh, rows, :] for hh in range(o_ref.shape[0])], axis=1)
            return _dot(o.astype(BF16), wo_ref[...])
        return _dot(o_ref[rows, :].astype(BF16), wo_ref[...])

    a = [mixer_out(p) for p in range(parts)]
    h1, x = [], []
    for p in range(parts):
        rows = slice(p * rp, (p + 1) * rp)
        h1.append(h_ref[rows, :] + _rms_rows(a[p]) * g_post_ref[...])
        x.append((_rms_rows(h1[p]) * g_pre_ref[...]).astype(BF16))
    acc = [jnp.zeros((rp, h_ref.shape[1]), F32) for _ in range(parts)]
    for c in range(d_ff // fc):
        for p in range(parts):
            u = jnp.maximum(_dot(x[p], w1_ref[0, :, c * fc:(c + 1) * fc]), 0.0)
            acc[p] = acc[p] + _dot((u * u).astype(BF16), w2_ref[0, c * fc:(c + 1) * fc, :])
    for p in range(parts):
        rows = slice(p * rp, (p + 1) * rp)
        y_ref[rows, :] = h1[p] + _rms_rows(acc[p]) * g_ffn_ref[...]


def _out_ffn(o, h, o2, h2, w_o, g_post, g_pre, w1, w2, g_ffn, *, layer, o_layout, o2_layout):
    n, d = h.shape
    n2 = h2.shape[0]
    one_layer = lambda w: pl.BlockSpec((1,) + w.shape[1:], lambda i: (layer, 0, 0), pipeline_mode=pl.Buffered(1))
    kdim = w_o.shape[0]
    tm = min(TOKEN_TILE, n)
    tiles_n = n // tm
    tile = lambda i: jnp.minimum(i, tiles_n - 1)
    row = lambda i: (tile(i), 0)
    if o_layout == "features":
        tiles = o.shape[2] // tm
        o_spec = pl.BlockSpec((1, kdim, tm), lambda i: (tile(i) // tiles, 0, tile(i) % tiles))
    elif o_layout == "heads":
        o_spec = pl.BlockSpec((o.shape[0], tm, o.shape[2]), lambda i: (0, tile(i), 0))
    else:
        o_spec = pl.BlockSpec((tm, kdim), row)
    assert o2_layout in ("heads", "tokens") and n2 <= tm
    vmem = (2 * _nbytes((tm, kdim), o.dtype) + 4 * _nbytes((tm, d), F32) + _nbytes(w_o.shape, BF16)
            + _nbytes(w1.shape[1:], BF16) + _nbytes(w2.shape[1:], BF16) + 6 * _nbytes((tm, COL_CHUNK), F32)
            + _nbytes(o2.shape, o2.dtype) + 3 * _nbytes((n2, d), F32))
    return pl.pallas_call(
        functools.partial(_out_ffn_body, o_layout=o_layout, o2_layout=o2_layout),
        grid=(tiles_n + 1,),
        in_specs=[o_spec, pl.BlockSpec((tm, d), row), _resident(o2.shape), _resident((n2, d)), _resident(w_o.shape),
                  _resident((1, d)), _resident((1, d)), one_layer(w1), one_layer(w2),
                  _resident((1, d))],
        out_specs=[pl.BlockSpec((tm, d), row), pl.BlockSpec((n2, d), lambda i: (0, 0))],
        out_shape=[jax.ShapeDtypeStruct((n, d), F32), jax.ShapeDtypeStruct((n2, d), F32)],
        compiler_params=_params(("arbitrary",), vmem),
        name="out_ffn",
    )(o, h, o2, h2, w_o, g_post, g_pre, w1, w2, g_ffn)


def _partial_rope(x, c_tab, sa_tab, sb_tab):
    half = ROT_DIM // 2
    outs = []
    for j in range(x.shape[1] // LANES):
        s = x[:, j * LANES:(j + 1) * LANES]
        outs.append(s * c_tab + pltpu.roll(s, LANES - half, axis=1) * sa_tab + pltpu.roll(s, half, axis=1) * sb_tab)
    return outs


def _swa_in_body(h_ref, g_q_ref, g_kv_ref, wq_ref, wkv_ref, c_ref, sa_ref, sb_ref, q_ref, k_ref, v_ref):
    y = _rms_rows(h_ref[...])
    xq = (y * g_q_ref[...]).astype(BF16)
    xkv = (y * g_kv_ref[...]).astype(BF16)
    c_tab, sa_tab, sb_tab = c_ref[...], sa_ref[...], sb_ref[...]
    q = _dot(xq, wq_ref[...]) * (SWA_HEAD_DIM ** -0.5 * LOG2E)
    for j, s in enumerate(_partial_rope(q, c_tab, sa_tab, sb_tab)):
        q_ref[:, j * LANES:(j + 1) * LANES] = s.astype(q_ref.dtype)
    kv = _dot(xkv, wkv_ref[...])
    kw = k_ref.shape[1]
    for j, s in enumerate(_partial_rope(kv[:, :kw], c_tab, sa_tab, sb_tab)):
        k_ref[:, j * LANES:(j + 1) * LANES] = s
    v_ref[...] = kv[:, kw:]


def _swa_in(h, g_q, g_kv, w_q, w_kv, c_tab, sa_tab, sb_tab, *, q_dtype):
    n, d = h.shape
    tm = min(TOKEN_TILE, n)
    qw = w_q.shape[1]
    kw = w_kv.shape[1] // 2
    pos_tiles = c_tab.shape[0] // tm
    row = lambda i: (i, 0)
    tab = lambda i: (i % pos_tiles, 0)
    vmem = (2 * _nbytes((tm, d), F32) + _nbytes(w_q.shape, BF16) + _nbytes(w_kv.shape, BF16)
            + 6 * _nbytes((tm, LANES), F32) + 2 * _nbytes((tm, qw), q_dtype) + 4 * _nbytes((tm, kw), F32)
            + 6 * _nbytes((tm, qw), F32))
    return pl.pallas_call(
        _swa_in_body,
        grid=(n // tm,),
        in_specs=[pl.BlockSpec((tm, d), row), _resident((1, d)), _resident((1, d)),
                  _resident(w_q.shape), _resident(w_kv.shape),
                  pl.BlockSpec((tm, LANES), tab), pl.BlockSpec((tm, LANES), tab), pl.BlockSpec((tm, LANES), tab)],
        out_specs=[pl.BlockSpec((tm, qw), row), pl.BlockSpec((tm, kw), row), pl.BlockSpec((tm, kw), row)],
        out_shape=[jax.ShapeDtypeStruct((n, qw), q_dtype), jax.ShapeDtypeStruct((n, kw), F32),
                   jax.ShapeDtypeStruct((n, kw), F32)],
        compiler_params=_params(("parallel",), vmem),
        name="swa_in",
    )(h, g_q, g_kv, w_q, w_kv, c_tab, sa_tab, sb_tab)


def _swa_in_t_body(h_ref, g_q_ref, g_kv_ref, wqt_ref, wk_ref, wvt_ref, wv_ref, cos_t_ref, sin_t_ref,
                   c_ref, sa_ref, sb_ref, qt_ref, k_ref, vt_ref, kwin_ref, vwin_ref, *, tiles):
    tm = h_ref.shape[0]
    hd = SWA_HEAD_DIM
    half = ROT_DIM // 2
    win = kwin_ref.shape[0]
    parts = SWA_ROW_PARTS if tm % (SWA_ROW_PARTS * LANES) == 0 else 1
    rp = tm // parts
    assert rp >= win
    for p in range(parts):
        rows = slice(p * rp, (p + 1) * rp)
        y = _rms_rows(h_ref[rows, :])
        xq = (y * g_q_ref[...]).astype(BF16)
        xkv = (y * g_kv_ref[...]).astype(BF16)
        cos_t, sin_t = cos_t_ref[:, rows], sin_t_ref[:, rows]
        qt = _dot_nt(wqt_ref[...], xq) * (hd ** -0.5 * LOG2E)
        for hq in range(qt.shape[0] // hd):
            base = hq * hd
            x1 = qt[base:base + half]
            x2 = qt[base + half:base + 2 * half]
            rot = jnp.concatenate([x1 * cos_t - x2 * sin_t, x2 * cos_t + x1 * sin_t], axis=0)
            qt_ref[0, base:base + 2 * half, rows] = rot.astype(qt_ref.dtype)
            qt_ref[0, base + 2 * half:base + hd, rows] = qt[base + 2 * half:base + hd].astype(qt_ref.dtype)
        k_rot = _partial_rope(_dot(xkv, wk_ref[...]), c_ref[rows, :], sa_ref[rows, :], sb_ref[rows, :])
        for j, s in enumerate(k_rot):
            k_ref[rows, j * LANES:(j + 1) * LANES] = s.astype(k_ref.dtype)
        vt_ref[0, :, rows] = _dot_nt(wvt_ref[...], xkv).astype(vt_ref.dtype)
        if p == parts - 1:
            @pl.when(pl.program_id(0) % tiles == tiles - 1)
            def _():
                for j, s in enumerate(k_rot):
                    kwin_ref[:, j * LANES:(j + 1) * LANES] = s[rp - win:, :]
                vwin_ref[...] = _dot(xkv[rp - win:, :], wv_ref[...])


def _swa_in_t(h, g_q, g_kv, w_q, w_kv, cos_t, sin_t, c_tab, sa_tab, sb_tab, *, batch, win):
    n, d = h.shape
    seq = n // batch
    tm = min(SWA_TOKEN_TILE, seq)
    tiles = seq // tm
    qw = w_q.shape[1]
    kw = w_kv.shape[1] // 2
    wqt = w_q.T
    wk, wv = w_kv[:, :kw], w_kv[:, kw:]
    wvt = wv.T
    row = lambda i: (i, 0)
    tab = lambda i: (i % tiles, 0)
    tab_t = lambda i: (0, i % tiles)
    feat = lambda i: (i // tiles, 0, i % tiles)
    per_seq = lambda i: (i // tiles, 0)
    half = ROT_DIM // 2
    vmem = (2 * _nbytes((tm, d), F32) + 2 * _nbytes(w_q.shape, BF16) + 3 * _nbytes(w_kv.shape, BF16)
            + 8 * _nbytes((tm, LANES), F32) + 2 * _nbytes((tm, qw + 2 * kw), BF16) + 4 * _nbytes((win, kw), F32)
            + 4 * _nbytes((tm, qw), F32))
    return pl.pallas_call(
        functools.partial(_swa_in_t_body, tiles=tiles),
        grid=(n // tm,),
        in_specs=[pl.BlockSpec((tm, d), row), _resident((1, d)), _resident((1, d)),
                  _resident(wqt.shape), _resident(wk.shape), _resident(wvt.shape), _resident(wv.shape),
                  pl.BlockSpec((half, tm), tab_t), pl.BlockSpec((half, tm), tab_t),
                  pl.BlockSpec((tm, LANES), tab), pl.BlockSpec((tm, LANES), tab), pl.BlockSpec((tm, LANES), tab)],
        out_specs=[pl.BlockSpec((1, qw, tm), feat), pl.BlockSpec((tm, kw), row), pl.BlockSpec((1, kw, tm), feat),
                   pl.BlockSpec((win, kw), per_seq), pl.BlockSpec((win, kw), per_seq)],
        out_shape=[jax.ShapeDtypeStruct((batch, qw, seq), BF16), jax.ShapeDtypeStruct((n, kw), BF16),
                   jax.ShapeDtypeStruct((batch, kw, seq), BF16),
                   jax.ShapeDtypeStruct((batch * win, kw), F32), jax.ShapeDtypeStruct((batch * win, kw), F32)],
        compiler_params=_params(("arbitrary",), vmem),
        name="swa_in_t",
    )(h, g_q, g_kv, wqt, wk, wvt, wv, cos_t, sin_t, c_tab, sa_tab, sb_tab)


def _attn_prompt_body(sinks_ref, mask_ref, qt_ref, kp_ref, kc_ref, vtp_ref, vtc_ref, ot_ref, *, group):
    blk = kp_ref.shape[0]
    nblk = kc_ref.shape[0] // blk
    hd = SWA_HEAD_DIM
    kvh_n = kc_ref.shape[1] // hd
    cols = group * blk
    kj = lax.broadcasted_iota(jnp.int32, (blk, cols), 0)
    qi = lax.broadcasted_iota(jnp.int32, (blk, cols), 1) % blk
    own = kj <= qi
    lane_head = lax.broadcasted_iota(jnp.int32, (1, cols), 1) // blk
    ones_rows = jnp.ones((BF16_SUBLANES, blk), BF16)
    has_prev = pl.program_id(1) > 0

    def scores(j, kvh):
        tile, lo = divmod(kvh * hd, LANES)
        q4t = jnp.concatenate([qt_ref[0, (kvh * group + g) * hd:(kvh * group + g + 1) * hd, j * blk:(j + 1) * blk]
                               for g in range(group)], axis=1)
        rhs = jnp.concatenate([q4t if part * hd == lo else jnp.zeros_like(q4t) for part in range(LANES // hd)], axis=0)
        k_tile = slice(tile * LANES, (tile + 1) * LANES)
        k_prev = kp_ref[:, k_tile] if j == 0 else kc_ref[(j - 1) * blk:j * blk, k_tile]
        return _dot(kc_ref[j * blk:(j + 1) * blk, k_tile], rhs), _dot(k_prev, rhs)

    def finish(j, kvh, s_own, s_prev):
        if j == 0:
            s_prev = jnp.where(has_prev, s_prev, NEG)
        s = jnp.where(own, s_own, s_prev)
        sink = jnp.full((1, cols), sinks_ref[kvh * group] * LOG2E, F32)
        for g in range(1, group):
            sink = jnp.where(lane_head == g, sinks_ref[kvh * group + g] * LOG2E, sink)
        m = jnp.maximum(jnp.max(s, axis=0, keepdims=True), sink)
        e = jnp.exp2(s - m).astype(BF16)
        p_own = e * mask_ref[...]
        p = jnp.concatenate([p_own, e - p_own], axis=0)
        head_rows = slice(kvh * hd, (kvh + 1) * hd)
        vt_own = vtc_ref[0, head_rows, j * blk:(j + 1) * blk]
        vt_prev = vtp_ref[0, head_rows, :] if j == 0 else vtc_ref[0, head_rows, (j - 1) * blk:j * blk]
        vt = jnp.concatenate([jnp.concatenate([vt_own, ones_rows], axis=0),
                              jnp.concatenate([vt_prev, ones_rows], axis=0)], axis=1)
        acc = _dot(vt, p)
        denom = acc[hd:hd + 1, :] + jnp.exp2(sink - m)
        ot = acc[:hd] / denom
        for g in range(group):
            hq = kvh * group + g
            ot_ref[0, hq * hd:(hq + 1) * hd, j * blk:(j + 1) * blk] = ot[:, g * blk:(g + 1) * blk].astype(ot_ref.dtype)

    units = [(j, kvh) for j in range(nblk) for kvh in range(kvh_n)]
    queue = [scores(*unit) for unit in units[:ATTN_LOOKAHEAD]]
    for idx, unit in enumerate(units):
        if idx + ATTN_LOOKAHEAD < len(units):
            queue.append(scores(*units[idx + ATTN_LOOKAHEAD]))
        finish(*unit, *queue.pop(0))


def _attn_prompt(sinks, qt, k, vt):
    batch, qw, seq = qt.shape
    kw = k.shape[1]
    blk = WINDOW
    nblk = ATTN_BLOCKS_PER_STEP if seq % (ATTN_BLOCKS_PER_STEP * blk) == 0 else 1
    span = nblk * blk
    steps = seq // span
    group = qw // kw
    cols = group * blk
    own = (jnp.arange(blk)[:, None] <= (jnp.arange(cols) % blk)[None, :]).astype(BF16)
    cur_t = lambda b, i: (b, 0, i)
    prev_t = lambda b, i: (b, 0, jnp.maximum(i * nblk - 1, 0))
    cur = lambda b, i: (b * steps + i, 0)
    prev = lambda b, i: (b * steps * nblk + jnp.maximum(i * nblk - 1, 0), 0)
    vmem = (4 * _nbytes((qw, span), BF16) + 6 * _nbytes((span, kw), BF16) + 16 * _nbytes((blk, cols), F32))
    return pl.pallas_call(
        functools.partial(_attn_prompt_body, group=group),
        grid=(batch, steps),
        in_specs=[pl.BlockSpec(memory_space=pltpu.SMEM), _resident((blk, cols)), pl.BlockSpec((1, qw, span), cur_t),
                  pl.BlockSpec((blk, kw), prev), pl.BlockSpec((span, kw), cur),
                  pl.BlockSpec((1, kw, blk), prev_t), pl.BlockSpec((1, kw, span), cur_t)],
        out_specs=pl.BlockSpec((1, qw, span), cur_t),
        out_shape=jax.ShapeDtypeStruct((batch, qw, seq), BF16),
        compiler_params=_params(("parallel", "parallel"), vmem),
        name="attn_prompt",
    )(sinks, own, qt, k, k, vt, vt)


def _attn_sample_body(sink_ref, q_ref, kn_ref, vn_ref, kc_ref, vc_ref, o_ref, kw_ref, vw_ref, *, seq, group, q_start):
    rows = q_ref.shape[0]
    nseq = rows // seq
    win = kc_ref.shape[1]
    hd = SWA_HEAD_DIM
    kw = kn_ref.shape[1]
    heads = q_ref.shape[1] // kw
    srows = heads * rows
    lhs = jnp.concatenate([q_ref[:, hq * kw:(hq + 1) * kw] for hq in range(heads)], axis=0)
    kn = kn_ref[...]
    vn = vn_ref[...]
    sink = sink_ref[...]

    r_c = lax.broadcasted_iota(jnp.int32, (srows, win), 0) % rows
    c_c = lax.broadcasted_iota(jnp.int32, (srows, win), 1)
    seq_c = r_c // seq
    rel_c = (r_c % seq) + win - c_c
    ok_c = (rel_c >= 0) & (rel_c < WINDOW) & (q_start - win + c_c >= 0)
    s_c = _dot_nt(lhs, kc_ref[0].astype(BF16))
    for b in range(1, nseq):
        s_c = jnp.where(seq_c == b, _dot_nt(lhs, kc_ref[b].astype(BF16)), s_c)
    s_c = jnp.where(ok_c, s_c, NEG)

    r_n = lax.broadcasted_iota(jnp.int32, (srows, rows), 0) % rows
    c_n = lax.broadcasted_iota(jnp.int32, (srows, rows), 1)
    rel_n = (r_n % seq) - (c_n % seq)
    ok_n = (rel_n >= 0) & (rel_n < WINDOW) & ((r_n // seq) == (c_n // seq))
    s_n = jnp.where(ok_n, _dot_nt(lhs, kn.astype(BF16)), NEG)

    m = jnp.maximum(jnp.maximum(jnp.max(s_c, axis=-1, keepdims=True), jnp.max(s_n, axis=-1, keepdims=True)), sink)
    e_c = jnp.exp2(s_c - m)
    e_n = jnp.exp2(s_n - m)
    denom = jnp.sum(e_c, axis=-1, keepdims=True) + jnp.sum(e_n, axis=-1, keepdims=True) + jnp.exp2(sink - m)
    acc = _dot(e_n.astype(BF16), vn.astype(BF16))
    for b in range(nseq):
        acc = acc + _dot(jnp.where(seq_c == b, e_c, 0.0).astype(BF16), vc_ref[b].astype(BF16))
    o = acc / denom
    for hq in range(heads):
        kvh = hq // group
        o_ref[:, hq * hd:(hq + 1) * hd] = o[hq * rows:(hq + 1) * rows, kvh * hd:(kvh + 1) * hd].astype(o_ref.dtype)
    for b in range(nseq):
        kw_ref[b, 0:win - seq, :] = kc_ref[b, seq:win, :]
        kw_ref[b, win - seq:win, :] = kn[b * seq:(b + 1) * seq, :]
        vw_ref[b, 0:win - seq, :] = vc_ref[b, seq:win, :]
        vw_ref[b, win - seq:win, :] = vn[b * seq:(b + 1) * seq, :]


def _attn_sample(sinks, q_wide, k_new, v_new, k_cache, v_cache, *, seq, q_start):
    n, qww = q_wide.shape
    kw = k_new.shape[1]
    heads = qww // kw
    hd = SWA_HEAD_DIM
    group = heads // (kw // hd)
    batch, win, _ = k_cache.shape
    nseq = ATTN_SAMPLE_SEQS if batch % ATTN_SAMPLE_SEQS == 0 else batch
    rows = nseq * seq
    sink_rows = jnp.repeat(sinks * LOG2E, rows)[:, None]
    row = lambda i: (i, 0)
    cache = lambda i: (i, 0, 0)
    vmem = (8 * _nbytes((nseq, win, kw), F32) + 4 * _nbytes((rows, qww), BF16)
            + 16 * _nbytes((heads * rows, win + kw), F32))
    return pl.pallas_call(
        functools.partial(_attn_sample_body, seq=seq, group=group, q_start=q_start),
        grid=(batch // nseq,),
        in_specs=[_resident((heads * rows, 1)), pl.BlockSpec((rows, qww), row),
                  pl.BlockSpec((rows, kw), row), pl.BlockSpec((rows, kw), row),
                  pl.BlockSpec((nseq, win, kw), cache), pl.BlockSpec((nseq, win, kw), cache)],
        out_specs=[pl.BlockSpec((rows, heads * hd), row), pl.BlockSpec((nseq, win, kw), cache),
                   pl.BlockSpec((nseq, win, kw), cache)],
        out_shape=[jax.ShapeDtypeStruct((n, heads * hd), BF16), jax.ShapeDtypeStruct((batch, win, kw), F32),
                   jax.ShapeDtypeStruct((batch, win, kw), F32)],
        compiler_params=_params(("parallel",), vmem),
        name="attn_sample",
    )(sink_rows, q_wide, k_new, v_new, k_cache, v_cache)


def _ret_rope_tables(pos, dk):
    inv = 1.0 / (RET_ROPE_THETA ** jnp.linspace(0.0, 1.0, dk // 2, dtype=F32))
    ang = pos[:, None] * inv[None, :]
    return jnp.cos(ang), jnp.sin(ang)


def _ret_key_scale(log_g, seq, n, dk):
    chunk = _ret_chunk(seq)
    tm = min(TOKEN_TILE, n)
    assert tm % chunk == 0
    left = (chunk - 1 - jnp.arange(tm) % chunk).astype(F32)
    per_head = jnp.exp(log_g[None, :] * left[:, None]) * dk ** -0.5
    return jnp.repeat(per_head, dk, axis=1)


def _swa_cos_sin(pos):
    half = ROT_DIM // 2
    inv = ROPE_THETA ** (-jnp.arange(half, dtype=F32) / half)
    ang = pos[:, None] * inv[None, :]
    return jnp.cos(ang), jnp.sin(ang)


def _swa_rope_tables(pos):
    half = ROT_DIM // 2
    cos, sin = _swa_cos_sin(pos)
    n = pos.shape[0]
    pad = jnp.zeros((n, SWA_HEAD_DIM - 2 * half), F32)
    c_head = jnp.concatenate([cos, cos, pad + 1.0], axis=1)
    sa_head = jnp.concatenate([-sin, jnp.zeros_like(sin), pad], axis=1)
    sb_head = jnp.concatenate([jnp.zeros_like(sin), sin, pad], axis=1)
    reps = LANES // SWA_HEAD_DIM
    return tuple(jnp.tile(t, (1, reps)) for t in (c_head, sa_head, sb_head))


def _tile_rows(tab, seq, n):
    tm = min(TOKEN_TILE, n)
    return tab if seq >= tm else jnp.tile(tab, (tm // seq, 1))


def kernel(x_prompt, x_sample, state_ret, cache_k_win, cache_v_win, ret_norm_pre, ret_w_in, ret_w_out, ret_norm_post, kv_norm, w_kv, swa_norm_pre, swa_w_q, swa_sinks, swa_w_o, swa_norm_post, ffn_norm_pre, ffn_w1, ffn_w2, ffn_norm_post):
    n_a = DEPTH // 2
    assert n_a == 1 and DEPTH == 2, "one retention layer followed by one sliding-window layer"
    d = x_prompt.shape[-1]
    heads = RET_HEADS
    dk = ret_w_out.shape[-1] // heads
    dv = ret_w_out.shape[-2] // heads
    kvh, hd = SWA_KV_HEADS, SWA_HEAD_DIM
    row2 = lambda g: g.reshape(1, d)
    log_g = jnp.log1p(-jnp.exp2(-5.0 - jnp.arange(heads, dtype=F32)))

    w_in = ret_w_in[0].astype(BF16)
    w_out = ret_w_out[0].astype(BF16)
    wq = swa_w_q[0].astype(BF16)
    wkv = w_kv.astype(BF16)
    wo = swa_w_o[0].astype(BF16)
    w1 = ffn_w1.astype(BF16)
    w2 = ffn_w2.astype(BF16)
    sinks = swa_sinks[0]
    q_heads = wq.shape[1] // hd
    on_kv_head = (jnp.arange(q_heads)[:, None] // (q_heads // kvh) == jnp.arange(kvh)[None, :]).astype(BF16)
    wq_wide = (wq.reshape(d, q_heads, 1, hd) * on_kv_head[None, :, :, None]).reshape(d, q_heads * kvh * hd)

    b_p, t_p, _ = x_prompt.shape
    b_s, t_s, _ = x_sample.shape
    n_p, n_s = b_p * t_p, b_s * t_s
    pos_p = jnp.arange(t_p, dtype=F32)
    pos_s = PAST_LEN + jnp.arange(t_s, dtype=F32)
    h_p = x_prompt.reshape(n_p, d)
    h_s = x_sample.reshape(n_s, d)

    def ret_tables(pos, t, n):
        cos, sin = (_tile_rows(tab, t, n) for tab in _ret_rope_tables(pos, dk))
        return cos, sin, _ret_key_scale(log_g, t, n, dk)

    o_p, state_p = _ret_prompt(log_g, h_p, row2(ret_norm_pre[0]), w_in, *ret_tables(pos_p, t_p, n_p),
                               batch=b_p, heads=heads, dk=dk, dv=dv)
    q, kd, v, sg = _ret_in(h_s, row2(ret_norm_pre[0]), w_in, *ret_tables(pos_s, t_s, n_s),
                           heads=heads, dk=dk, dv=dv, out_dtype=F32)
    o_s, state_s = _ret_sample(log_g, q, kd, v, sg, state_ret[0], seq=t_s)
    h_p, h_s = _out_ffn(o_p, h_p, o_s, h_s, w_out, row2(ret_norm_post[0]), row2(ffn_norm_pre[0]), w1, w2,
                        row2(ffn_norm_post[0]), layer=0, o_layout="tokens", o2_layout="heads")

    w_p = min(WINDOW, t_p)
    cos, sin = _swa_cos_sin(pos_p)
    qt, k, vt, k_win_p, v_win_p = _swa_in_t(h_p, row2(swa_norm_pre[0]), row2(kv_norm), wq, wkv, cos.T, sin.T,
                                            *_swa_rope_tables(pos_p), batch=b_p, win=w_p)
    o_p = _attn_prompt(sinks, qt, k, vt)
    w_s = cache_k_win.shape[1]
    tabs = tuple(_tile_rows(tab, t_s, n_s) for tab in _swa_rope_tables(pos_s))
    q, k, v = _swa_in(h_s, row2(swa_norm_pre[0]), row2(kv_norm), wq_wide, wkv, *tabs, q_dtype=BF16)
    o_s, k_win_s, v_win_s = _attn_sample(sinks, q, k, v, cache_k_win.reshape(b_s, w_s, kvh * hd),
                                         cache_v_win.reshape(b_s, w_s, kvh * hd), seq=t_s, q_start=PAST_LEN)
    h_p, h_s = _out_ffn(o_p, h_p, o_s, h_s, wo, row2(swa_norm_post[0]), row2(ffn_norm_pre[1]), w1, w2,
                        row2(ffn_norm_post[1]), layer=1, o_layout="features", o2_layout="tokens")

    return (h_p.reshape(b_p, t_p, d), h_s.reshape(b_s, t_s, d), state_p[None], state_s[None],
            k_win_p.reshape(b_p, w_p, kvh, hd), v_win_p.reshape(b_p, w_p, kvh, hd),
            k_win_s.reshape(b_s, w_s, kvh, hd), v_win_s.reshape(b_s, w_s, kvh, hd))
```

```python
import functools

import jax
import jax.numpy as jnp
from jax import lax
from jax.experimental import pallas as pl
from jax.experimental.pallas import tpu as pltpu

DEPTH = 2
PAST_LEN = 16384
RET_HEADS = 4
RET_ROPE_THETA = 10000.0
SWA_HEAD_DIM = 64
SWA_KV_HEADS = 4
WINDOW = 128
ROPE_THETA = 500000.0
ROT_DIM = SWA_HEAD_DIM // 4
EPS = 1e-6
NEG = -1e30
LOG2E = 1.4426950408889634

LANES = 128
BF16_SUBLANES = 16
VMEM_CAP_BYTES = 64 * 1024 * 1024
VMEM_BUDGET_BYTES = VMEM_CAP_BYTES - 8 * 1024 * 1024
VMEM_FLOOR_BYTES = 16 * 1024 * 1024

TOKEN_TILE = 512
COL_CHUNK = 1024
RET_KERNEL_CHUNK = 256
ATTN_SAMPLE_SEQS = 4
ATTN_BLOCKS_PER_STEP = 16
ATTN_LOOKAHEAD = 4
OUT_FFN_ROW_PARTS = 2
SWA_TOKEN_TILE = 1024
SWA_ROW_PARTS = 4

F32 = jnp.float32
BF16 = jnp.bfloat16


def _params(semantics, vmem_bytes):
    limit = int(min(max(vmem_bytes, VMEM_FLOOR_BYTES), VMEM_BUDGET_BYTES))
    return pltpu.CompilerParams(dimension_semantics=semantics, vmem_limit_bytes=limit)


def _resident(shape):
    nd = len(shape)
    return pl.BlockSpec(shape, lambda *_: (0,) * nd, pipeline_mode=pl.Buffered(1))


def _nbytes(shape, dtype):
    n = 1
    for s in shape:
        n *= s
    return n * jnp.dtype(dtype).itemsize


def _rms_rows(x):
    return x * lax.rsqrt(jnp.mean(x * x, axis=-1, keepdims=True) + EPS)


def _dot(a, b):
    return jnp.dot(a, b, preferred_element_type=F32)


def _dot_nt(a, b):
    return lax.dot_general(a, b, (((1,), (1,)), ((), ())), preferred_element_type=F32)


def _dot_tn(a, b):
    return lax.dot_general(a, b, (((0,), (0,)), ((), ())), preferred_element_type=F32)


def _ret_in_body(h_ref, g_ref, w_ref, cos_ref, sin_ref, kscale_ref, q_ref, kd_ref, v_ref, sg_ref, *, heads, dk, dv,
                 rows=slice(None)):
    xn = (_rms_rows(h_ref[rows, :]) * g_ref[...]).astype(BF16)
    cos = cos_ref[rows, :]
    sin = sin_ref[rows, :]
    half = dk // 2
    qk_w = heads * dk
    v_w = heads * dv

    def proj(lo, width):
        return _dot(xn, w_ref[:, lo:lo + width])

    for base, ref, scale_ref in ((0, q_ref, None), (qk_w, kd_ref, kscale_ref)):
        p = proj(base, qk_w)
        for hh in range(heads):
            lo, mid, hi = hh * dk, hh * dk + half, (hh + 1) * dk
            x1 = p[:, lo:mid]
            x2 = p[:, mid:hi]
            o1 = x1 * cos - x2 * sin
            o2 = x2 * cos + x1 * sin
            if scale_ref is not None:
                o1 = o1 * scale_ref[rows, lo:mid]
                o2 = o2 * scale_ref[rows, mid:hi]
            ref[hh, rows, :half] = o1.astype(ref.dtype)
            ref[hh, rows, half:] = o2.astype(ref.dtype)
    cw = min(COL_CHUNK, v_w)
    per_chunk = cw // dv
    for c in range(v_w // cw):
        v = proj(2 * qk_w + c * cw, cw)
        for j in range(per_chunk):
            v_ref[c * per_chunk + j, rows, :] = v[:, j * dv:(j + 1) * dv].astype(v_ref.dtype)
    for c in range(v_w // cw):
        gate = proj(2 * qk_w + v_w + c * cw, cw)
        sg = gate * jax.nn.sigmoid(gate)
        for j in range(per_chunk):
            sg_ref[c * per_chunk + j, rows, :] = sg[:, j * dv:(j + 1) * dv].astype(sg_ref.dtype)


def _ret_in(h, g, w_in, cos, sin, kscale, *, heads, dk, dv, out_dtype):
    n, d = h.shape
    tm = min(TOKEN_TILE, n)
    qk_w, v_w = heads * dk, heads * dv
    pos_tiles = cos.shape[0] // tm
    row = lambda i: (i, 0)
    tab = lambda i: (i % pos_tiles, 0)
    by_head = lambda i: (0, i, 0)
    vmem = (2 * _nbytes((tm, d), F32) + _nbytes(w_in.shape, BF16) + 4 * _nbytes((tm, dk // 2), F32)
            + _nbytes((tm, qk_w), F32) + 2 * _nbytes((tm, 2 * qk_w + 2 * v_w), out_dtype)
            + 4 * _nbytes((tm, COL_CHUNK), F32))
    return pl.pallas_call(
        functools.partial(_ret_in_body, heads=heads, dk=dk, dv=dv),
        grid=(n // tm,),
        in_specs=[pl.BlockSpec((tm, d), row), _resident((1, d)), _resident(w_in.shape),
                  pl.BlockSpec((tm, dk // 2), tab), pl.BlockSpec((tm, dk // 2), tab), _resident((tm, qk_w))],
        out_specs=[pl.BlockSpec((heads, tm, dk), by_head), pl.BlockSpec((heads, tm, dk), by_head),
                   pl.BlockSpec((heads, tm, dv), by_head), pl.BlockSpec((heads, tm, dv), by_head)],
        out_shape=[jax.ShapeDtypeStruct((heads, n, dk), out_dtype), jax.ShapeDtypeStruct((heads, n, dk), out_dtype),
                   jax.ShapeDtypeStruct((heads, n, dv), out_dtype), jax.ShapeDtypeStruct((heads, n, dv), out_dtype)],
        compiler_params=_params(("parallel",), vmem),
        name="ret_in",
    )(h, g, w_in, cos, sin, kscale)


def _ret_prompt_body(lg_ref, h_ref, g_ref, w_ref, cos_ref, sin_ref, kscale_ref,
                     q2_ref, kd2_ref, v2_ref, sg2_ref, s2_in_ref,
                     o_ref, s_out_ref, o2_ref, s2_out_ref,
                     q_s, kd_s, v_s, sg_s, s_ref, *, heads, dk, dv, chunk, seq2):
    t = pl.program_id(1)

    @pl.when(t == 0)
    def _():
        s_ref[...] = jnp.zeros_like(s_ref)

    _ret_sample_body(lg_ref, q2_ref, kd2_ref, v2_ref, sg2_ref, s2_in_ref, o2_ref, s2_out_ref, seq=seq2)

    tm = h_ref.shape[0]
    for c in range(tm // chunk):
        _ret_in_body(h_ref, g_ref, w_ref, cos_ref, sin_ref, kscale_ref, q_s, kd_s, v_s, sg_s,
                     heads=heads, dk=dk, dv=dv, rows=pl.ds(c * chunk, chunk))

    ri = lax.broadcasted_iota(jnp.int32, (chunk, chunk), 0)
    ci = lax.broadcasted_iota(jnp.int32, (chunk, chunk), 1)
    lower = (ri >= ci).astype(F32)
    row_v = lax.broadcasted_iota(jnp.int32, (chunk, dv), 0).astype(F32)
    causal, q_decay, chunk_decay = [], [], []
    for hh in range(heads):
        lg = lg_ref[hh]
        causal.append(lower * jnp.exp(jnp.full((1, chunk), -lg * chunk, F32)))
        q_decay.append(jnp.exp(lg * (row_v + 1.0)))
        chunk_decay.append(jnp.exp(jnp.full((1, dv), lg * chunk, F32)))

    for c in range(tm // chunk):
        rows = pl.ds(c * chunk, chunk)
        qk = [_dot_nt(q_s[hh, rows, :], kd_s[hh, rows, :]) for hh in range(heads)]
        grow = [_dot_tn(kd_s[hh, rows, :], v_s[hh, rows, :]) for hh in range(heads)]
        for hh in range(heads):
            s_prev = s_ref[hh]
            lhs = jnp.concatenate([(qk[hh] * causal[hh]).astype(BF16), q_s[hh, rows, :]], axis=1)
            rhs = jnp.concatenate([v_s[hh, rows, :], s_prev.astype(BF16)], axis=0)
            o = _rms_rows(q_decay[hh] * _dot(lhs, rhs))
            s_ref[hh] = chunk_decay[hh] * s_prev + grow[hh]
            o_ref[rows, hh * dv:(hh + 1) * dv] = (o * sg_s[hh, rows, :].astype(F32)).astype(o_ref.dtype)

    @pl.when(t == pl.num_programs(1) - 1)
    def _():
        s_out_ref[0] = s_ref[...]


def _ret_chunk(seq):
    return RET_KERNEL_CHUNK if seq % RET_KERNEL_CHUNK == 0 else seq


def _ret_prompt(log_g, h, g, w_in, cos, sin, kscale, q2, kd2, v2, sg2, state2, *, batch, heads, dk, dv, seq2):
    n, d = h.shape
    seq = n // batch
    tm = min(TOKEN_TILE, seq)
    chunk = _ret_chunk(seq)
    assert tm % chunk == 0 and seq % tm == 0
    nt = seq // tm
    steps = batch * nt
    n2 = q2.shape[1]
    batch2 = n2 // seq2
    assert batch2 % steps == 0
    group2 = batch2 // steps
    rows2 = group2 * seq2
    qk_w, v_w = heads * dk, heads * dv
    row = lambda b, t: (b * nt + t, 0)
    tab = lambda b, t: (t, 0)
    by_head2 = lambda b, t: (0, b * nt + t, 0)
    state_blk2 = lambda b, t: (b * nt + t, 0, 0, 0)
    vmem = (2 * _nbytes((tm, d), F32) + _nbytes(w_in.shape, BF16) + 4 * _nbytes((tm, dk // 2), F32)
            + _nbytes((tm, qk_w), F32) + 2 * _nbytes((tm, v_w), BF16) + _nbytes((tm, 2 * qk_w + 2 * v_w), BF16)
            + 3 * _nbytes((heads, dk, dv), F32) + 4 * _nbytes((tm, COL_CHUNK), F32)
            + 2 * heads * (_nbytes((chunk, chunk), F32) + _nbytes((dk, dv), F32) + _nbytes((chunk, dv), F32))
            + 4 * _nbytes((group2, heads, dk, dv), F32) + 2 * _nbytes((rows2, 2 * qk_w + 3 * v_w), F32))
    return pl.pallas_call(
        functools.partial(_ret_prompt_body, heads=heads, dk=dk, dv=dv, chunk=chunk, seq2=seq2),
        grid=(batch, nt),
        in_specs=[pl.BlockSpec(memory_space=pltpu.SMEM),
                  pl.BlockSpec((tm, d), row), _resident((1, d)), _resident(w_in.shape),
                  pl.BlockSpec((tm, dk // 2), tab), pl.BlockSpec((tm, dk // 2), tab), _resident((tm, qk_w)),
                  pl.BlockSpec((heads, rows2, dk), by_head2), pl.BlockSpec((heads, rows2, dk), by_head2),
                  pl.BlockSpec((heads, rows2, dv), by_head2), pl.BlockSpec((heads, rows2, dv), by_head2),
                  pl.BlockSpec((group2, heads, dk, dv), state_blk2)],
        out_specs=[pl.BlockSpec((tm, v_w), row),
                   pl.BlockSpec((1, heads, dk, dv), lambda b, t: (b, 0, 0, 0)),
                   pl.BlockSpec((heads, rows2, dv), by_head2),
                   pl.BlockSpec((group2, heads, dk, dv), state_blk2)],
        out_shape=[jax.ShapeDtypeStruct((n, v_w), BF16),
                   jax.ShapeDtypeStruct((batch, heads, dk, dv), F32),
                   jax.ShapeDtypeStruct((heads, n2, dv), F32),
                   jax.ShapeDtypeStruct((batch2, heads, dk, dv), F32)],
        scratch_shapes=[pltpu.VMEM((heads, tm, dk), BF16), pltpu.VMEM((heads, tm, dk), BF16),
                        pltpu.VMEM((heads, tm, dv), BF16), pltpu.VMEM((heads, tm, dv), BF16),
                        pltpu.VMEM((heads, dk, dv), F32)],
        compiler_params=_params(("parallel", "arbitrary"), vmem),
        name="ret_prompt",
    )(log_g, h, g, w_in, cos, sin, kscale, q2, kd2, v2, sg2, state2)


def _ret_sample_body(lg_ref, q_ref, kd_ref, v_ref, sg_ref, s_in_ref, o_ref, s_out_ref, *, seq):
    heads, rows, dk = q_ref.shape
    dv = v_ref.shape[2]
    group = rows // seq
    ri = lax.broadcasted_iota(jnp.int32, (rows, rows), 0)
    ci = lax.broadcasted_iota(jnp.int32, (rows, rows), 1)
    visible = ((ri // seq) == (ci // seq)) & (ri >= ci)
    row_v = lax.broadcasted_iota(jnp.int32, (rows, dv), 0)
    row_k = lax.broadcasted_iota(jnp.int32, (rows, dk), 0)
    for hh in range(heads):
        lg = lg_ref[hh]
        causal = jnp.where(visible, jnp.exp(jnp.full((rows, rows), -lg * seq, F32)), 0.0)
        q_decay = jnp.exp(lg * ((row_v % seq).astype(F32) + 1.0))
        chunk_decay = jnp.exp(jnp.full((1, dv), lg * seq, F32))
        q = q_ref[hh].astype(BF16)
        kd = kd_ref[hh]
        v = v_ref[hh].astype(BF16)
        o = _dot((_dot_nt(q, kd.astype(BF16)) * causal).astype(BF16), v)
        for g in range(group):
            s_prev = s_in_ref[g, hh]
            o = jnp.where((row_v // seq) == g, o + _dot(q, s_prev.astype(BF16)), o)
            kd_g = jnp.where((row_k // seq) == g, kd, 0.0).astype(BF16)
            s_out_ref[g, hh] = chunk_decay * s_prev + _dot_tn(kd_g, v)
        o = _rms_rows(q_decay * o)
        o_ref[hh] = (o * sg_ref[hh]).astype(o_ref.dtype)


def _out_ffn_body(o_ref, h_ref, o2_ref, h2_ref, wo_ref, g_post_ref, g_pre_ref, w1_ref, w2_ref, g_ffn_ref,
                  y_ref, y2_ref, *, o_layout, o2_layout):
    weights = (wo_ref, g_post_ref, g_pre_ref, w1_ref, w2_ref, g_ffn_ref)
    last = pl.program_id(0) == pl.num_programs(0) - 1
    pl.when(jnp.logical_not(last))(lambda: _out_ffn_tile(o_ref, h_ref, *weights, y_ref, o_layout=o_layout))
    pl.when(last)(lambda: _out_ffn_tile(o2_ref, h2_ref, *weights, y2_ref, o_layout=o2_layout))


def _out_ffn_tile(o_ref, h_ref, wo_ref, g_post_ref, g_pre_ref, w1_ref, w2_ref, g_ffn_ref, y_ref, *, o_layout):
    tm = h_ref.shape[0]
    parts = OUT_FFN_ROW_PARTS if tm % (OUT_FFN_ROW_PARTS * BF16_SUBLANES) == 0 else 1
    rp = tm // parts
    d_ff = w1_ref.shape[2]
    fc = min(COL_CHUNK, d_ff)

    def mixer_out(p):
        rows = slice(p * rp, (p + 1) * rp)
        if o_layout == "features":
            return _dot_tn(o_ref[0, :, rows], wo_ref[...])
        if o_layout == "heads":
            o = jnp.concatenate([o_ref[hh, rows, :] for hh in range(o_ref.shape[0])], axis=1)
            return _dot(o.astype(BF16), wo_ref[...])
        return _dot(o_ref[rows, :].astype(BF16), wo_ref[...])

    a = [mixer_out(p) for p in range(parts)]
    h1, x = [], []
    for p in range(parts):
        rows = slice(p * rp, (p + 1) * rp)
        h1.append(h_ref[rows, :] + _rms_rows(a[p]) * g_post_ref[...])
        x.append((_rms_rows(h1[p]) * g_pre_ref[...]).astype(BF16))
    acc = [jnp.zeros((rp, h_ref.shape[1]), F32) for _ in range(parts)]
    for c in range(d_ff // fc):
        for p in range(parts):
            u = jnp.maximum(_dot(x[p], w1_ref[0, :, c * fc:(c + 1) * fc]), 0.0)
            acc[p] = acc[p] + _dot((u * u).astype(BF16), w2_ref[0, c * fc:(c + 1) * fc, :])
    for p in range(parts):
        rows = slice(p * rp, (p + 1) * rp)
        y_ref[rows, :] = h1[p] + _rms_rows(acc[p]) * g_ffn_ref[...]


def _out_ffn(o, h, o2, h2, w_o, g_post, g_pre, w1, w2, g_ffn, *, layer, o_layout, o2_layout):
    n, d = h.shape
    n2 = h2.shape[0]
    one_layer = lambda w: pl.BlockSpec((1,) + w.shape[1:], lambda i: (layer, 0, 0), pipeline_mode=pl.Buffered(1))
    kdim = w_o.shape[0]
    tm = min(TOKEN_TILE, n)
    tiles_n = n // tm
    tile = lambda i: jnp.minimum(i, tiles_n - 1)
    row = lambda i: (tile(i), 0)
    if o_layout == "features":
        tiles = o.shape[2] // tm
        o_spec = pl.BlockSpec((1, kdim, tm), lambda i: (tile(i) // tiles, 0, tile(i) % tiles))
    elif o_layout == "heads":
        o_spec = pl.BlockSpec((o.shape[0], tm, o.shape[2]), lambda i: (0, tile(i), 0))
    else:
        o_spec = pl.BlockSpec((tm, kdim), row)
    assert o2_layout in ("heads", "tokens") and n2 <= tm
    vmem = (2 * _nbytes((tm, kdim), o.dtype) + 4 * _nbytes((tm, d), F32) + _nbytes(w_o.shape, BF16)
            + _nbytes(w1.shape[1:], BF16) + _nbytes(w2.shape[1:], BF16) + 6 * _nbytes((tm, COL_CHUNK), F32)
            + _nbytes(o2.shape, o2.dtype) + 3 * _nbytes((n2, d), F32))
    return pl.pallas_call(
        functools.partial(_out_ffn_body, o_layout=o_layout, o2_layout=o2_layout),
        grid=(tiles_n + 1,),
        in_specs=[o_spec, pl.BlockSpec((tm, d), row), _resident(o2.shape), _resident((n2, d)), _resident(w_o.shape),
                  _resident((1, d)), _resident((1, d)), one_layer(w1), one_layer(w2),
                  _resident((1, d))],
        out_specs=[pl.BlockSpec((tm, d), row), pl.BlockSpec((n2, d), lambda i: (0, 0))],
        out_shape=[jax.ShapeDtypeStruct((n, d), F32), jax.ShapeDtypeStruct((n2, d), F32)],
        compiler_params=_params(("arbitrary",), vmem),
        name="out_ffn",
    )(o, h, o2, h2, w_o, g_post, g_pre, w1, w2, g_ffn)


def _partial_rope(x, c_tab, sa_tab, sb_tab):
    half = ROT_DIM // 2
    outs = []
    for j in range(x.shape[1] // LANES):
        s = x[:, j * LANES:(j + 1) * LANES]
        outs.append(s * c_tab + pltpu.roll(s, LANES - half, axis=1) * sa_tab + pltpu.roll(s, half, axis=1) * sb_tab)
    return outs


def _swa_in_body(h_ref, g_q_ref, g_kv_ref, wq_ref, wkv_ref, c_ref, sa_ref, sb_ref, q_ref, k_ref, v_ref):
    y = _rms_rows(h_ref[...])
    xq = (y * g_q_ref[...]).astype(BF16)
    xkv = (y * g_kv_ref[...]).astype(BF16)
    c_tab, sa_tab, sb_tab = c_ref[...], sa_ref[...], sb_ref[...]
    q = _dot(xq, wq_ref[...]) * (SWA_HEAD_DIM ** -0.5 * LOG2E)
    for j, s in enumerate(_partial_rope(q, c_tab, sa_tab, sb_tab)):
        q_ref[:, j * LANES:(j + 1) * LANES] = s.astype(q_ref.dtype)
    kv = _dot(xkv, wkv_ref[...])
    kw = k_ref.shape[1]
    for j, s in enumerate(_partial_rope(kv[:, :kw], c_tab, sa_tab, sb_tab)):
        k_ref[:, j * LANES:(j + 1) * LANES] = s
    v_ref[...] = kv[:, kw:]


def _swa_in(h, g_q, g_kv, w_q, w_kv, c_tab, sa_tab, sb_tab, *, q_dtype):
    n, d = h.shape
    tm = min(TOKEN_TILE, n)
    qw = w_q.shape[1]
    kw = w_kv.shape[1] // 2
    pos_tiles = c_tab.shape[0] // tm
    row = lambda i: (i, 0)
    tab = lambda i: (i % pos_tiles, 0)
    vmem = (2 * _nbytes((tm, d), F32) + _nbytes(w_q.shape, BF16) + _nbytes(w_kv.shape, BF16)
            + 6 * _nbytes((tm, LANES), F32) + 2 * _nbytes((tm, qw), q_dtype) + 4 * _nbytes((tm, kw), F32)
            + 6 * _nbytes((tm, qw), F32))
    return pl.pallas_call(
        _swa_in_body,
        grid=(n // tm,),
        in_specs=[pl.BlockSpec((tm, d), row), _resident((1, d)), _resident((1, d)),
                  _resident(w_q.shape), _resident(w_kv.shape),
                  pl.BlockSpec((tm, LANES), tab), pl.BlockSpec((tm, LANES), tab), pl.BlockSpec((tm, LANES), tab)],
        out_specs=[pl.BlockSpec((tm, qw), row), pl.BlockSpec((tm, kw), row), pl.BlockSpec((tm, kw), row)],
        out_shape=[jax.ShapeDtypeStruct((n, qw), q_dtype), jax.ShapeDtypeStruct((n, kw), F32),
                   jax.ShapeDtypeStruct((n, kw), F32)],
        compiler_params=_params(("parallel",), vmem),
        name="swa_in",
    )(h, g_q, g_kv, w_q, w_kv, c_tab, sa_tab, sb_tab)


def _swa_in_t_body(h_ref, g_q_ref, g_kv_ref, wqt_ref, wk_ref, wvt_ref, wv_ref, cos_t_ref, sin_t_ref,
                   c_ref, sa_ref, sb_ref, qt_ref, k_ref, vt_ref, kwin_ref, vwin_ref, *, tiles):
    tm = h_ref.shape[0]
    hd = SWA_HEAD_DIM
    half = ROT_DIM // 2
    win = kwin_ref.shape[0]
    parts = SWA_ROW_PARTS if tm % (SWA_ROW_PARTS * LANES) == 0 else 1
    rp = tm // parts
    assert rp >= win
    for p in range(parts):
        rows = slice(p * rp, (p + 1) * rp)
        y = _rms_rows(h_ref[rows, :])
        xq = (y * g_q_ref[...]).astype(BF16)
        xkv = (y * g_kv_ref[...]).astype(BF16)
        cos_t, sin_t = cos_t_ref[:, rows], sin_t_ref[:, rows]
        qt = _dot_nt(wqt_ref[...], xq) * (hd ** -0.5 * LOG2E)
        for hq in range(qt.shape[0] // hd):
            base = hq * hd
            x1 = qt[base:base + half]
            x2 = qt[base + half:base + 2 * half]
            rot = jnp.concatenate([x1 * cos_t - x2 * sin_t, x2 * cos_t + x1 * sin_t], axis=0)
            qt_ref[0, base:base + 2 * half, rows] = rot.astype(qt_ref.dtype)
            qt_ref[0, base + 2 * half:base + hd, rows] = qt[base + 2 * half:base + hd].astype(qt_ref.dtype)
        k_rot = _partial_rope(_dot(xkv, wk_ref[...]), c_ref[rows, :], sa_ref[rows, :], sb_ref[rows, :])
        for j, s in enumerate(k_rot):
            k_ref[rows, j * LANES:(j + 1) * LANES] = s.astype(k_ref.dtype)
        vt_ref[0, :, rows] = _dot_nt(wvt_ref[...], xkv).astype(vt_ref.dtype)
        if p == parts - 1:
            @pl.when(pl.program_id(0) % tiles == tiles - 1)
            def _():
                for j, s in enumerate(k_rot):
                    kwin_ref[:, j * LANES:(j + 1) * LANES] = s[rp - win:, :]
                vwin_ref[...] = _dot(xkv[rp - win:, :], wv_ref[...])


def _swa_in_t(h, g_q, g_kv, w_q, w_kv, cos_t, sin_t, c_tab, sa_tab, sb_tab, *, batch, win):
    n, d = h.shape
    seq = n // batch
    tm = min(SWA_TOKEN_TILE, seq)
    tiles = seq // tm
    qw = w_q.shape[1]
    kw = w_kv.shape[1] // 2
    wqt = w_q.T
    wk, wv = w_kv[:, :kw], w_kv[:, kw:]
    wvt = wv.T
    row = lambda i: (i, 0)
    tab = lambda i: (i % tiles, 0)
    tab_t = lambda i: (0, i % tiles)
    feat = lambda i: (i // tiles, 0, i % tiles)
    per_seq = lambda i: (i // tiles, 0)
    half = ROT_DIM // 2
    vmem = (2 * _nbytes((tm, d), F32) + 2 * _nbytes(w_q.shape, BF16) + 3 * _nbytes(w_kv.shape, BF16)
            + 8 * _nbytes((tm, LANES), F32) + 2 * _nbytes((tm, qw + 2 * kw), BF16) + 4 * _nbytes((win, kw), F32)
            + 4 * _nbytes((tm, qw), F32))
    return pl.pallas_call(
        functools.partial(_swa_in_t_body, tiles=tiles),
        grid=(n // tm,),
        in_specs=[pl.BlockSpec((tm, d), row), _resident((1, d)), _resident((1, d)),
                  _resident(wqt.shape), _resident(wk.shape), _resident(wvt.shape), _resident(wv.shape),
                  pl.BlockSpec((half, tm), tab_t), pl.BlockSpec((half, tm), tab_t),
                  pl.BlockSpec((tm, LANES), tab), pl.BlockSpec((tm, LANES), tab), pl.BlockSpec((tm, LANES), tab)],
        out_specs=[pl.BlockSpec((1, qw, tm), feat), pl.BlockSpec((tm, kw), row), pl.BlockSpec((1, kw, tm), feat),
                   pl.BlockSpec((win, kw), per_seq), pl.BlockSpec((win, kw), per_seq)],
        out_shape=[jax.ShapeDtypeStruct((batch, qw, seq), BF16), jax.ShapeDtypeStruct((n, kw), BF16),
                   jax.ShapeDtypeStruct((batch, kw, seq), BF16),
                   jax.ShapeDtypeStruct((batch * win, kw), F32), jax.ShapeDtypeStruct((batch * win, kw), F32)],
        compiler_params=_params(("arbitrary",), vmem),
        name="swa_in_t",
    )(h, g_q, g_kv, wqt, wk, wvt, wv, cos_t, sin_t, c_tab, sa_tab, sb_tab)


def _attn_prompt_body(sinks_ref, mask_ref, qt_ref, kp_ref, kc_ref, vtp_ref, vtc_ref, ot_ref, *, group):
    blk = kp_ref.shape[0]
    nblk = kc_ref.shape[0] // blk
    hd = SWA_HEAD_DIM
    kvh_n = kc_ref.shape[1] // hd
    cols = group * blk
    kj = lax.broadcasted_iota(jnp.int32, (blk, cols), 0)
    qi = lax.broadcasted_iota(jnp.int32, (blk, cols), 1) % blk
    own = kj <= qi
    lane_head = lax.broadcasted_iota(jnp.int32, (1, cols), 1) // blk
    ones_rows = jnp.ones((BF16_SUBLANES, blk), BF16)
    has_prev = pl.program_id(1) > 0

    def scores(j, kvh):
        tile, lo = divmod(kvh * hd, LANES)
        q4t = jnp.concatenate([qt_ref[0, (kvh * group + g) * hd:(kvh * group + g + 1) * hd, j * blk:(j + 1) * blk]
                               for g in range(group)], axis=1)
        rhs = jnp.concatenate([q4t if part * hd == lo else jnp.zeros_like(q4t) for part in range(LANES // hd)], axis=0)
        k_tile = slice(tile * LANES, (tile + 1) * LANES)
        k_prev = kp_ref[:, k_tile] if j == 0 else kc_ref[(j - 1) * blk:j * blk, k_tile]
        return _dot(kc_ref[j * blk:(j + 1) * blk, k_tile], rhs), _dot(k_prev, rhs)

    def finish(j, kvh, s_own, s_prev):
        if j == 0:
            s_prev = jnp.where(has_prev, s_prev, NEG)
        s = jnp.where(own, s_own, s_prev)
        sink = jnp.full((1, cols), sinks_ref[kvh * group] * LOG2E, F32)
        for g in range(1, group):
            sink = jnp.where(lane_head == g, sinks_ref[kvh * group + g] * LOG2E, sink)
        m = jnp.maximum(jnp.max(s, axis=0, keepdims=True), sink)
        e = jnp.exp2(s - m).astype(BF16)
        p_own = e * mask_ref[...]
        p = jnp.concatenate([p_own, e - p_own], axis=0)
        head_rows = slice(kvh * hd, (kvh + 1) * hd)
        vt_own = vtc_ref[0, head_rows, j * blk:(j + 1) * blk]
        vt_prev = vtp_ref[0, head_rows, :] if j == 0 else vtc_ref[0, head_rows, (j - 1) * blk:j * blk]
        vt = jnp.concatenate([jnp.concatenate([vt_own, ones_rows], axis=0),
                              jnp.concatenate([vt_prev, ones_rows], axis=0)], axis=1)
        acc = _dot(vt, p)
        denom = acc[hd:hd + 1, :] + jnp.exp2(sink - m)
        ot = acc[:hd] / denom
        for g in range(group):
            hq = kvh * group + g
            ot_ref[0, hq * hd:(hq + 1) * hd, j * blk:(j + 1) * blk] = ot[:, g * blk:(g + 1) * blk].astype(ot_ref.dtype)

    units = [(j, kvh) for j in range(nblk) for kvh in range(kvh_n)]
    queue = [scores(*unit) for unit in units[:ATTN_LOOKAHEAD]]
    for idx, unit in enumerate(units):
        if idx + ATTN_LOOKAHEAD < len(units):
            queue.append(scores(*units[idx + ATTN_LOOKAHEAD]))
        finish(*unit, *queue.pop(0))


def _attn_prompt(sinks, qt, k, vt):
    batch, qw, seq = qt.shape
    kw = k.shape[1]
    blk = WINDOW
    nblk = ATTN_BLOCKS_PER_STEP if seq % (ATTN_BLOCKS_PER_STEP * blk) == 0 else 1
    span = nblk * blk
    steps = seq // span
    group = qw // kw
    cols = group * blk
    own = (jnp.arange(blk)[:, None] <= (jnp.arange(cols) % blk)[None, :]).astype(BF16)
    cur_t = lambda b, i: (b, 0, i)
    prev_t = lambda b, i: (b, 0, jnp.maximum(i * nblk - 1, 0))
    cur = lambda b, i: (b * steps + i, 0)
    prev = lambda b, i: (b * steps * nblk + jnp.maximum(i * nblk - 1, 0), 0)
    vmem = (4 * _nbytes((qw, span), BF16) + 6 * _nbytes((span, kw), BF16) + 16 * _nbytes((blk, cols), F32))
    return pl.pallas_call(
        functools.partial(_attn_prompt_body, group=group),
        grid=(batch, steps),
        in_specs=[pl.BlockSpec(memory_space=pltpu.SMEM), _resident((blk, cols)), pl.BlockSpec((1, qw, span), cur_t),
                  pl.BlockSpec((blk, kw), prev), pl.BlockSpec((span, kw), cur),
                  pl.BlockSpec((1, kw, blk), prev_t), pl.BlockSpec((1, kw, span), cur_t)],
        out_specs=pl.BlockSpec((1, qw, span), cur_t),
        out_shape=jax.ShapeDtypeStruct((batch, qw, seq), BF16),
        compiler_params=_params(("parallel", "parallel"), vmem),
        name="attn_prompt",
    )(sinks, own, qt, k, k, vt, vt)


def _attn_sample_body(sink_ref, q_ref, kn_ref, vn_ref, kc_ref, vc_ref, o_ref, kw_ref, vw_ref, *, seq, group, q_start):
    rows = q_ref.shape[0]
    nseq = rows // seq
    win = kc_ref.shape[1]
    hd = SWA_HEAD_DIM
    kw = kn_ref.shape[1]
    heads = q_ref.shape[1] // kw
    srows = heads * rows
    lhs = jnp.concatenate([q_ref[:, hq * kw:(hq + 1) * kw] for hq in range(heads)], axis=0)
    kn = kn_ref[...]
    vn = vn_ref[...]
    sink = sink_ref[...]

    r_c = lax.broadcasted_iota(jnp.int32, (srows, win), 0) % rows
    c_c = lax.broadcasted_iota(jnp.int32, (srows, win), 1)
    seq_c = r_c // seq
    rel_c = (r_c % seq) + win - c_c
    ok_c = (rel_c >= 0) & (rel_c < WINDOW) & (q_start - win + c_c >= 0)
    s_c = _dot_nt(lhs, kc_ref[0].astype(BF16))
    for b in range(1, nseq):
        s_c = jnp.where(seq_c == b, _dot_nt(lhs, kc_ref[b].astype(BF16)), s_c)
    s_c = jnp.where(ok_c, s_c, NEG)

    r_n = lax.broadcasted_iota(jnp.int32, (srows, rows), 0) % rows
    c_n = lax.broadcasted_iota(jnp.int32, (srows, rows), 1)
    rel_n = (r_n % seq) - (c_n % seq)
    ok_n = (rel_n >= 0) & (rel_n < WINDOW) & ((r_n // seq) == (c_n // seq))
    s_n = jnp.where(ok_n, _dot_nt(lhs, kn.astype(BF16)), NEG)

    m = jnp.maximum(jnp.maximum(jnp.max(s_c, axis=-1, keepdims=True), jnp.max(s_n, axis=-1, keepdims=True)), sink)
    e_c = jnp.exp2(s_c - m)
    e_n = jnp.exp2(s_n - m)
    denom = jnp.sum(e_c, axis=-1, keepdims=True) + jnp.sum(e_n, axis=-1, keepdims=True) + jnp.exp2(sink - m)
    acc = _dot(e_n.astype(BF16), vn.astype(BF16))
    for b in range(nseq):
        acc = acc + _dot(jnp.where(seq_c == b, e_c, 0.0).astype(BF16), vc_ref[b].astype(BF16))
    o = acc / denom
    for hq in range(heads):
        kvh = hq // group
        o_ref[:, hq * hd:(hq + 1) * hd] = o[hq * rows:(hq + 1) * rows, kvh * hd:(kvh + 1) * hd].astype(o_ref.dtype)
    for b in range(nseq):
        kw_ref[b, 0:win - seq, :] = kc_ref[b, seq:win, :]
        kw_ref[b, win - seq:win, :] = kn[b * seq:(b + 1) * seq, :]
        vw_ref[b, 0:win - seq, :] = vc_ref[b, seq:win, :]
        vw_ref[b, win - seq:win, :] = vn[b * seq:(b + 1) * seq, :]


def _attn_sample(sinks, q_wide, k_new, v_new, k_cache, v_cache, *, seq, q_start):
    n, qww = q_wide.shape
    kw = k_new.shape[1]
    heads = qww // kw
    hd = SWA_HEAD_DIM
    group = heads // (kw // hd)
    batch, win, _ = k_cache.shape
    nseq = ATTN_SAMPLE_SEQS if batch % ATTN_SAMPLE_SEQS == 0 else batch
    rows = nseq * seq
    sink_rows = jnp.repeat(sinks * LOG2E, rows)[:, None]
    row = lambda i: (i, 0)
    cache = lambda i: (i, 0, 0)
    vmem = (8 * _nbytes((nseq, win, kw), F32) + 4 * _nbytes((rows, qww), BF16)
            + 16 * _nbytes((heads * rows, win + kw), F32))
    return pl.pallas_call(
        functools.partial(_attn_sample_body, seq=seq, group=group, q_start=q_start),
        grid=(batch // nseq,),
        in_specs=[_resident((heads * rows, 1)), pl.BlockSpec((rows, qww), row),
                  pl.BlockSpec((rows, kw), row), pl.BlockSpec((rows, kw), row),
                  pl.BlockSpec((nseq, win, kw), cache), pl.BlockSpec((nseq, win, kw), cache)],
        out_specs=[pl.BlockSpec((rows, heads * hd), row), pl.BlockSpec((nseq, win, kw), cache),
                   pl.BlockSpec((nseq, win, kw), cache)],
        out_shape=[jax.ShapeDtypeStruct((n, heads * hd), BF16), jax.ShapeDtypeStruct((batch, win, kw), F32),
                   jax.ShapeDtypeStruct((batch, win, kw), F32)],
        compiler_params=_params(("parallel",), vmem),
        name="attn_sample",
    )(sink_rows, q_wide, k_new, v_new, k_cache, v_cache)


def _ret_rope_tables(pos, dk):
    inv = 1.0 / (RET_ROPE_THETA ** jnp.linspace(0.0, 1.0, dk // 2, dtype=F32))
    ang = pos[:, None] * inv[None, :]
    return jnp.cos(ang), jnp.sin(ang)


def _ret_key_scale(log_g, seq, n, dk):
    chunk = _ret_chunk(seq)
    tm = min(TOKEN_TILE, n)
    assert tm % chunk == 0
    left = (chunk - 1 - jnp.arange(tm) % chunk).astype(F32)
    per_head = jnp.exp(log_g[None, :] * left[:, None]) * dk ** -0.5
    return jnp.repeat(per_head, dk, axis=1)


def _swa_cos_sin(pos):
    half = ROT_DIM // 2
    inv = ROPE_THETA ** (-jnp.arange(half, dtype=F32) / half)
    ang = pos[:, None] * inv[None, :]
    return jnp.cos(ang), jnp.sin(ang)


def _swa_rope_tables(pos):
    half = ROT_DIM // 2
    cos, sin = _swa_cos_sin(pos)
    n = pos.shape[0]
    pad = jnp.zeros((n, SWA_HEAD_DIM - 2 * half), F32)
    c_head = jnp.concatenate([cos, cos, pad + 1.0], axis=1)
    sa_head = jnp.concatenate([-sin, jnp.zeros_like(sin), pad], axis=1)
    sb_head = jnp.concatenate([jnp.zeros_like(sin), sin, pad], axis=1)
    reps = LANES // SWA_HEAD_DIM
    return tuple(jnp.tile(t, (1, reps)) for t in (c_head, sa_head, sb_head))


def _tile_rows(tab, seq, n):
    tm = min(TOKEN_TILE, n)
    return tab if seq >= tm else jnp.tile(tab, (tm // seq, 1))


def kernel(x_prompt, x_sample, state_ret, cache_k_win, cache_v_win, ret_norm_pre, ret_w_in, ret_w_out, ret_norm_post, kv_norm, w_kv, swa_norm_pre, swa_w_q, swa_sinks, swa_w_o, swa_norm_post, ffn_norm_pre, ffn_w1, ffn_w2, ffn_norm_post):
    n_a = DEPTH // 2
    assert n_a == 1 and DEPTH == 2, "one retention layer followed by one sliding-window layer"
    d = x_prompt.shape[-1]
    heads = RET_HEADS
    dk = ret_w_out.shape[-1] // heads
    dv = ret_w_out.shape[-2] // heads
    kvh, hd = SWA_KV_HEADS, SWA_HEAD_DIM
    row2 = lambda g: g.reshape(1, d)
    log_g = jnp.log1p(-jnp.exp2(-5.0 - jnp.arange(heads, dtype=F32)))

    w_in = ret_w_in[0].astype(BF16)
    w_out = ret_w_out[0].astype(BF16)
    wq = swa_w_q[0].astype(BF16)
    wkv = w_kv.astype(BF16)
    wo = swa_w_o[0].astype(BF16)
    w1 = ffn_w1.astype(BF16)
    w2 = ffn_w2.astype(BF16)
    sinks = swa_sinks[0]
    q_heads = wq.shape[1] // hd
    on_kv_head = (jnp.arange(q_heads)[:, None] // (q_heads // kvh) == jnp.arange(kvh)[None, :]).astype(BF16)
    wq_wide = (wq.reshape(d, q_heads, 1, hd) * on_kv_head[None, :, :, None]).reshape(d, q_heads * kvh * hd)

    b_p, t_p, _ = x_prompt.shape
    b_s, t_s, _ = x_sample.shape
    n_p, n_s = b_p * t_p, b_s * t_s
    pos_p = jnp.arange(t_p, dtype=F32)
    pos_s = PAST_LEN + jnp.arange(t_s, dtype=F32)
    h_p = x_prompt.reshape(n_p, d)
    h_s = x_sample.reshape(n_s, d)

    def ret_tables(pos, t, n):
        cos, sin = (_tile_rows(tab, t, n) for tab in _ret_rope_tables(pos, dk))
        return cos, sin, _ret_key_scale(log_g, t, n, dk)

    q, kd, v, sg = _ret_in(h_s, row2(ret_norm_pre[0]), w_in, *ret_tables(pos_s, t_s, n_s),
                           heads=heads, dk=dk, dv=dv, out_dtype=F32)
    o_p, state_p, o_s, state_s = _ret_prompt(log_g, h_p, row2(ret_norm_pre[0]), w_in, *ret_tables(pos_p, t_p, n_p),
                                             q, kd, v, sg, state_ret[0],
                                             batch=b_p, heads=heads, dk=dk, dv=dv, seq2=t_s)
    h_p, h_s = _out_ffn(o_p, h_p, o_s, h_s, w_out, row2(ret_norm_post[0]), row2(ffn_norm_pre[0]), w1, w2,
                        row2(ffn_norm_post[0]), layer=0, o_layout="tokens", o2_layout="heads")

    w_p = min(WINDOW, t_p)
    cos, sin = _swa_cos_sin(pos_p)
    qt, k, vt, k_win_p, v_win_p = _swa_in_t(h_p, row2(swa_norm_pre[0]), row2(kv_norm), wq, wkv, cos.T, sin.T,
                                            *_swa_rope_tables(pos_p), batch=b_p, win=w_p)
    o_p = _attn_prompt(sinks, qt, k, vt)
    w_s = cache_k_win.shape[1]
    tabs = tuple(_tile_rows(tab, t_s, n_s) for tab in _swa_rope_tables(pos_s))
    q, k, v = _swa_in(h_s, row2(swa_norm_pre[0]), row2(kv_norm), wq_wide, wkv, *tabs, q_dtype=BF16)
    o_s, k_win_s, v_win_s = _attn_sample(sinks, q, k, v, cache_k_win.reshape(b_s, w_s, kvh * hd),
                                         cache_v_win.reshape(b_s, w_s, kvh * hd), seq=t_s, q_start=PAST_LEN)
    h_p, h_s = _out_ffn(o_p, h_p, o_s, h_s, wo, row2(swa_norm_post[0]), row2(ffn_norm_pre[1]), w1, w2,
                        row2(ffn_norm_post[1]), layer=1, o_layout="features", o2_layout="tokens")

    return (h_p.reshape(b_p, t_p, d), h_s.reshape(b_s, t_s, d), state_p[None], state_s[None],
            k_win_p.reshape(b_p, w_p, kvh, hd), v_win_p.reshape(b_p, w_p, kvh, hd),
            k_win_s.reshape(b_s, w_s, kvh, hd), v_win_s.reshape(b_s, w_s, kvh, hd))
```

```python
import functools

import jax
import jax.numpy as jnp
from jax import lax
from jax.experimental import pallas as pl
from jax.experimental.pallas import tpu as pltpu

DEPTH = 2
PAST_LEN = 16384
RET_HEADS = 4
RET_ROPE_THETA = 10000.0
SWA_HEAD_DIM = 64
SWA_KV_HEADS = 4
WINDOW = 128
ROPE_THETA = 500000.0
ROT_DIM = SWA_HEAD_DIM // 4
EPS = 1e-6
NEG = -1e30
LOG2E = 1.4426950408889634

LANES = 128
BF16_SUBLANES = 16
VMEM_CAP_BYTES = 64 * 1024 * 1024
VMEM_BUDGET_BYTES = VMEM_CAP_BYTES - 8 * 1024 * 1024
VMEM_FLOOR_BYTES = 16 * 1024 * 1024

TOKEN_TILE = 512
COL_CHUNK = 1024
RET_KERNEL_CHUNK = 256
ATTN_BLOCKS_PER_STEP = 16
ATTN_LOOKAHEAD = 4
OUT_FFN_ROW_PARTS = 2
SWA_TOKEN_TILE = 1024
SWA_ROW_PARTS = 4

F32 = jnp.float32
BF16 = jnp.bfloat16


def _params(semantics, vmem_bytes):
    limit = int(min(max(vmem_bytes, VMEM_FLOOR_BYTES), VMEM_BUDGET_BYTES))
    return pltpu.CompilerParams(dimension_semantics=semantics, vmem_limit_bytes=limit)


def _resident(shape):
    nd = len(shape)
    return pl.BlockSpec(shape, lambda *_: (0,) * nd, pipeline_mode=pl.Buffered(1))


def _nbytes(shape, dtype):
    n = 1
    for s in shape:
        n *= s
    return n * jnp.dtype(dtype).itemsize


def _rms_rows(x):
    return x * lax.rsqrt(jnp.mean(x * x, axis=-1, keepdims=True) + EPS)


def _dot(a, b):
    return jnp.dot(a, b, preferred_element_type=F32)


def _dot_nt(a, b):
    return lax.dot_general(a, b, (((1,), (1,)), ((), ())), preferred_element_type=F32)


def _dot_tn(a, b):
    return lax.dot_general(a, b, (((0,), (0,)), ((), ())), preferred_element_type=F32)


def _ret_in_body(h_ref, g_ref, w_ref, cos_ref, sin_ref, kscale_ref, q_ref, kd_ref, v_ref, sg_ref, *, heads, dk, dv,
                 rows=slice(None)):
    xn = (_rms_rows(h_ref[rows, :]) * g_ref[...]).astype(BF16)
    cos = cos_ref[rows, :]
    sin = sin_ref[rows, :]
    half = dk // 2
    qk_w = heads * dk
    v_w = heads * dv

    def proj(lo, width):
        return _dot(xn, w_ref[:, lo:lo + width])

    for base, ref, scale_ref in ((0, q_ref, None), (qk_w, kd_ref, kscale_ref)):
        p = proj(base, qk_w)
        for hh in range(heads):
            lo, mid, hi = hh * dk, hh * dk + half, (hh + 1) * dk
            x1 = p[:, lo:mid]
            x2 = p[:, mid:hi]
            o1 = x1 * cos - x2 * sin
            o2 = x2 * cos + x1 * sin
            if scale_ref is not None:
                o1 = o1 * scale_ref[rows, lo:mid]
                o2 = o2 * scale_ref[rows, mid:hi]
            ref[hh, rows, :half] = o1.astype(ref.dtype)
            ref[hh, rows, half:] = o2.astype(ref.dtype)
    cw = min(COL_CHUNK, v_w)
    per_chunk = cw // dv
    for c in range(v_w // cw):
        v = proj(2 * qk_w + c * cw, cw)
        for j in range(per_chunk):
            v_ref[c * per_chunk + j, rows, :] = v[:, j * dv:(j + 1) * dv].astype(v_ref.dtype)
    for c in range(v_w // cw):
        gate = proj(2 * qk_w + v_w + c * cw, cw)
        sg = gate * jax.nn.sigmoid(gate)
        for j in range(per_chunk):
            sg_ref[c * per_chunk + j, rows, :] = sg[:, j * dv:(j + 1) * dv].astype(sg_ref.dtype)


def _ret_in(h, g, w_in, cos, sin, kscale, *, heads, dk, dv, out_dtype):
    n, d = h.shape
    tm = min(TOKEN_TILE, n)
    qk_w, v_w = heads * dk, heads * dv
    pos_tiles = cos.shape[0] // tm
    row = lambda i: (i, 0)
    tab = lambda i: (i % pos_tiles, 0)
    by_head = lambda i: (0, i, 0)
    vmem = (2 * _nbytes((tm, d), F32) + _nbytes(w_in.shape, BF16) + 4 * _nbytes((tm, dk // 2), F32)
            + _nbytes((tm, qk_w), F32) + 2 * _nbytes((tm, 2 * qk_w + 2 * v_w), out_dtype)
            + 4 * _nbytes((tm, COL_CHUNK), F32))
    return pl.pallas_call(
        functools.partial(_ret_in_body, heads=heads, dk=dk, dv=dv),
        grid=(n // tm,),
        in_specs=[pl.BlockSpec((tm, d), row), _resident((1, d)), _resident(w_in.shape),
                  pl.BlockSpec((tm, dk // 2), tab), pl.BlockSpec((tm, dk // 2), tab), _resident((tm, qk_w))],
        out_specs=[pl.BlockSpec((heads, tm, dk), by_head), pl.BlockSpec((heads, tm, dk), by_head),
                   pl.BlockSpec((heads, tm, dv), by_head), pl.BlockSpec((heads, tm, dv), by_head)],
        out_shape=[jax.ShapeDtypeStruct((heads, n, dk), out_dtype), jax.ShapeDtypeStruct((heads, n, dk), out_dtype),
                   jax.ShapeDtypeStruct((heads, n, dv), out_dtype), jax.ShapeDtypeStruct((heads, n, dv), out_dtype)],
        compiler_params=_params(("parallel",), vmem),
        name="ret_in",
    )(h, g, w_in, cos, sin, kscale)


def _ret_prompt_body(lg_ref, h_ref, g_ref, w_ref, cos_ref, sin_ref, kscale_ref,
                     q2_ref, kd2_ref, v2_ref, sg2_ref, s2_in_ref,
                     o_ref, s_out_ref, o2_ref, s2_out_ref,
                     q_s, kd_s, v_s, sg_s, s_ref, *, heads, dk, dv, chunk, seq2):
    t = pl.program_id(1)

    @pl.when(t == 0)
    def _():
        s_ref[...] = jnp.zeros_like(s_ref)

    _ret_sample_body(lg_ref, q2_ref, kd2_ref, v2_ref, sg2_ref, s2_in_ref, o2_ref, s2_out_ref, seq=seq2)

    tm = h_ref.shape[0]
    for c in range(tm // chunk):
        _ret_in_body(h_ref, g_ref, w_ref, cos_ref, sin_ref, kscale_ref, q_s, kd_s, v_s, sg_s,
                     heads=heads, dk=dk, dv=dv, rows=pl.ds(c * chunk, chunk))

    ri = lax.broadcasted_iota(jnp.int32, (chunk, chunk), 0)
    ci = lax.broadcasted_iota(jnp.int32, (chunk, chunk), 1)
    lower = (ri >= ci).astype(F32)
    row_v = lax.broadcasted_iota(jnp.int32, (chunk, dv), 0).astype(F32)
    causal, q_decay, chunk_decay = [], [], []
    for hh in range(heads):
        lg = lg_ref[hh]
        causal.append(lower * jnp.exp(jnp.full((1, chunk), -lg * chunk, F32)))
        q_decay.append(jnp.exp(lg * (row_v + 1.0)))
        chunk_decay.append(jnp.exp(jnp.full((1, dv), lg * chunk, F32)))

    for c in range(tm // chunk):
        rows = pl.ds(c * chunk, chunk)
        qk = [_dot_nt(q_s[hh, rows, :], kd_s[hh, rows, :]) for hh in range(heads)]
        grow = [_dot_tn(kd_s[hh, rows, :], v_s[hh, rows, :]) for hh in range(heads)]
        for hh in range(heads):
            s_prev = s_ref[hh]
            lhs = jnp.concatenate([(qk[hh] * causal[hh]).astype(BF16), q_s[hh, rows, :]], axis=1)
            rhs = jnp.concatenate([v_s[hh, rows, :], s_prev.astype(BF16)], axis=0)
            o = _rms_rows(q_decay[hh] * _dot(lhs, rhs))
            s_ref[hh] = chunk_decay[hh] * s_prev + grow[hh]
            o_ref[rows, hh * dv:(hh + 1) * dv] = (o * sg_s[hh, rows, :].astype(F32)).astype(o_ref.dtype)

    @pl.when(t == pl.num_programs(1) - 1)
    def _():
        s_out_ref[0] = s_ref[...]


def _ret_chunk(seq):
    return RET_KERNEL_CHUNK if seq % RET_KERNEL_CHUNK == 0 else seq


def _ret_prompt(log_g, h, g, w_in, cos, sin, kscale, q2, kd2, v2, sg2, state2, *, batch, heads, dk, dv, seq2):
    n, d = h.shape
    seq = n // batch
    tm = min(TOKEN_TILE, seq)
    chunk = _ret_chunk(seq)
    assert tm % chunk == 0 and seq % tm == 0
    nt = seq // tm
    steps = batch * nt
    n2 = q2.shape[1]
    batch2 = n2 // seq2
    assert batch2 % steps == 0
    group2 = batch2 // steps
    rows2 = group2 * seq2
    qk_w, v_w = heads * dk, heads * dv
    row = lambda b, t: (b * nt + t, 0)
    tab = lambda b, t: (t, 0)
    by_head2 = lambda b, t: (0, b * nt + t, 0)
    state_blk2 = lambda b, t: (b * nt + t, 0, 0, 0)
    vmem = (2 * _nbytes((tm, d), F32) + _nbytes(w_in.shape, BF16) + 4 * _nbytes((tm, dk // 2), F32)
            + _nbytes((tm, qk_w), F32) + 2 * _nbytes((tm, v_w), BF16) + _nbytes((tm, 2 * qk_w + 2 * v_w), BF16)
            + 3 * _nbytes((heads, dk, dv), F32) + 4 * _nbytes((tm, COL_CHUNK), F32)
            + 2 * heads * (_nbytes((chunk, chunk), F32) + _nbytes((dk, dv), F32) + _nbytes((chunk, dv), F32))
            + 4 * _nbytes((group2, heads, dk, dv), F32) + 2 * _nbytes((rows2, 2 * qk_w + 3 * v_w), F32))
    return pl.pallas_call(
        functools.partial(_ret_prompt_body, heads=heads, dk=dk, dv=dv, chunk=chunk, seq2=seq2),
        grid=(batch, nt),
        in_specs=[pl.BlockSpec(memory_space=pltpu.SMEM),
                  pl.BlockSpec((tm, d), row), _resident((1, d)), _resident(w_in.shape),
                  pl.BlockSpec((tm, dk // 2), tab), pl.BlockSpec((tm, dk // 2), tab), _resident((tm, qk_w)),
                  pl.BlockSpec((heads, rows2, dk), by_head2), pl.BlockSpec((heads, rows2, dk), by_head2),
                  pl.BlockSpec((heads, rows2, dv), by_head2), pl.BlockSpec((heads, rows2, dv), by_head2),
                  pl.BlockSpec((group2, heads, dk, dv), state_blk2)],
        out_specs=[pl.BlockSpec((tm, v_w), row),
                   pl.BlockSpec((1, heads, dk, dv), lambda b, t: (b, 0, 0, 0)),
                   pl.BlockSpec((heads, rows2, dv), by_head2),
                   pl.BlockSpec((group2, heads, dk, dv), state_blk2)],
        out_shape=[jax.ShapeDtypeStruct((n, v_w), BF16),
                   jax.ShapeDtypeStruct((batch, heads, dk, dv), F32),
                   jax.ShapeDtypeStruct((heads, n2, dv), F32),
                   jax.ShapeDtypeStruct((batch2, heads, dk, dv), F32)],
        scratch_shapes=[pltpu.VMEM((heads, tm, dk), BF16), pltpu.VMEM((heads, tm, dk), BF16),
                        pltpu.VMEM((heads, tm, dv), BF16), pltpu.VMEM((heads, tm, dv), BF16),
                        pltpu.VMEM((heads, dk, dv), F32)],
        compiler_params=_params(("parallel", "arbitrary"), vmem),
        name="ret_prompt",
    )(log_g, h, g, w_in, cos, sin, kscale, q2, kd2, v2, sg2, state2)


def _ret_sample_body(lg_ref, q_ref, kd_ref, v_ref, sg_ref, s_in_ref, o_ref, s_out_ref, *, seq):
    heads, rows, dk = q_ref.shape
    dv = v_ref.shape[2]
    group = rows // seq
    ri = lax.broadcasted_iota(jnp.int32, (rows, rows), 0)
    ci = lax.broadcasted_iota(jnp.int32, (rows, rows), 1)
    visible = ((ri // seq) == (ci // seq)) & (ri >= ci)
    row_v = lax.broadcasted_iota(jnp.int32, (rows, dv), 0)
    row_k = lax.broadcasted_iota(jnp.int32, (rows, dk), 0)
    for hh in range(heads):
        lg = lg_ref[hh]
        causal = jnp.where(visible, jnp.exp(jnp.full((rows, rows), -lg * seq, F32)), 0.0)
        q_decay = jnp.exp(lg * ((row_v % seq).astype(F32) + 1.0))
        chunk_decay = jnp.exp(jnp.full((1, dv), lg * seq, F32))
        q = q_ref[hh].astype(BF16)
        kd = kd_ref[hh]
        v = v_ref[hh].astype(BF16)
        o = _dot((_dot_nt(q, kd.astype(BF16)) * causal).astype(BF16), v)
        for g in range(group):
            s_prev = s_in_ref[g, hh]
            o = jnp.where((row_v // seq) == g, o + _dot(q, s_prev.astype(BF16)), o)
            kd_g = jnp.where((row_k // seq) == g, kd, 0.0).astype(BF16)
            s_out_ref[g, hh] = chunk_decay * s_prev + _dot_tn(kd_g, v)
        o = _rms_rows(q_decay * o)
        o_ref[hh] = (o * sg_ref[hh]).astype(o_ref.dtype)


def _out_ffn_body(o_ref, h_ref, o2_ref, h2_ref, wo_ref, g_post_ref, g_pre_ref, w1_ref, w2_ref, g_ffn_ref,
                  y_ref, y2_ref, *, o_layout, o2_layout):
    weights = (wo_ref, g_post_ref, g_pre_ref, w1_ref, w2_ref, g_ffn_ref)
    last = pl.program_id(0) == pl.num_programs(0) - 1
    pl.when(jnp.logical_not(last))(lambda: _out_ffn_tile(o_ref, h_ref, *weights, y_ref, o_layout=o_layout))
    pl.when(last)(lambda: _out_ffn_tile(o2_ref, h2_ref, *weights, y2_ref, o_layout=o2_layout))


def _out_ffn_tile(o_ref, h_ref, wo_ref, g_post_ref, g_pre_ref, w1_ref, w2_ref, g_ffn_ref, y_ref, *, o_layout):
    tm = h_ref.shape[0]
    parts = OUT_FFN_ROW_PARTS if tm % (OUT_FFN_ROW_PARTS * BF16_SUBLANES) == 0 else 1
    rp = tm // parts
    d_ff = w1_ref.shape[2]
    fc = min(COL_CHUNK, d_ff)

    def mixer_out(p):
        rows = slice(p * rp, (p + 1) * rp)
        if o_layout == "features":
            return _dot_tn(o_ref[0, :, rows], wo_ref[...])
        if o_layout == "heads":
            o = jnp.concatenate([o_ref[hh, rows, :] for hh in range(o_ref.shape[0])], axis=1)
            return _dot(o.astype(BF16), wo_ref[...])
        return _dot(o_ref[rows, :].astype(BF16), wo_ref[...])

    a = [mixer_out(p) for p in range(parts)]
    h1, x = [], []
    for p in range(parts):
        rows = slice(p * rp, (p + 1) * rp)
        h1.append(h_ref[rows, :] + _rms_rows(a[p]) * g_post_ref[...])
        x.append((_rms_rows(h1[p]) * g_pre_ref[...]).astype(BF16))
    acc = [jnp.zeros((rp, h_ref.shape[1]), F32) for _ in range(parts)]
    for c in range(d_ff // fc):
        for p in range(parts):
            u = jnp.maximum(_dot(x[p], w1_ref[0, :, c * fc:(c + 1) * fc]), 0.0)
            acc[p] = acc[p] + _dot((u * u).astype(BF16), w2_ref[0, c * fc:(c + 1) * fc, :])
    for p in range(parts):
        rows = slice(p * rp, (p + 1) * rp)
        y_ref[rows, :] = h1[p] + _rms_rows(acc[p]) * g_ffn_ref[...]


def _out_ffn(o, h, o2, h2, w_o, g_post, g_pre, w1, w2, g_ffn, *, layer, o_layout, o2_layout):
    n, d = h.shape
    n2 = h2.shape[0]
    one_layer = lambda w: pl.BlockSpec((1,) + w.shape[1:], lambda i: (layer, 0, 0), pipeline_mode=pl.Buffered(1))
    kdim = w_o.shape[0]
    tm = min(TOKEN_TILE, n)
    tiles_n = n // tm
    tile = lambda i: jnp.minimum(i, tiles_n - 1)
    row = lambda i: (tile(i), 0)
    if o_layout == "features":
        tiles = o.shape[2] // tm
        o_spec = pl.BlockSpec((1, kdim, tm), lambda i: (tile(i) // tiles, 0, tile(i) % tiles))
    elif o_layout == "heads":
        o_spec = pl.BlockSpec((o.shape[0], tm, o.shape[2]), lambda i: (0, tile(i), 0))
    else:
        o_spec = pl.BlockSpec((tm, kdim), row)
    assert o2_layout in ("heads", "tokens") and n2 <= tm
    vmem = (2 * _nbytes((tm, kdim), o.dtype) + 4 * _nbytes((tm, d), F32) + _nbytes(w_o.shape, BF16)
            + _nbytes(w1.shape[1:], BF16) + _nbytes(w2.shape[1:], BF16) + 6 * _nbytes((tm, COL_CHUNK), F32)
            + _nbytes(o2.shape, o2.dtype) + 3 * _nbytes((n2, d), F32))
    return pl.pallas_call(
        functools.partial(_out_ffn_body, o_layout=o_layout, o2_layout=o2_layout),
        grid=(tiles_n + 1,),
        in_specs=[o_spec, pl.BlockSpec((tm, d), row), _resident(o2.shape), _resident((n2, d)), _resident(w_o.shape),
                  _resident((1, d)), _resident((1, d)), one_layer(w1), one_layer(w2),
                  _resident((1, d))],
        out_specs=[pl.BlockSpec((tm, d), row), pl.BlockSpec((n2, d), lambda i: (0, 0))],
        out_shape=[jax.ShapeDtypeStruct((n, d), F32), jax.ShapeDtypeStruct((n2, d), F32)],
        compiler_params=_params(("arbitrary",), vmem),
        name="out_ffn",
    )(o, h, o2, h2, w_o, g_post, g_pre, w1, w2, g_ffn)


def _partial_rope(x, c_tab, sa_tab, sb_tab):
    half = ROT_DIM // 2
    outs = []
    for j in range(x.shape[1] // LANES):
        s = x[:, j * LANES:(j + 1) * LANES]
        outs.append(s * c_tab + pltpu.roll(s, LANES - half, axis=1) * sa_tab + pltpu.roll(s, half, axis=1) * sb_tab)
    return outs


def _swa_in_body(h_ref, g_q_ref, g_kv_ref, wq_ref, wkv_ref, c_ref, sa_ref, sb_ref, q_ref, k_ref, v_ref):
    y = _rms_rows(h_ref[...])
    xq = (y * g_q_ref[...]).astype(BF16)
    xkv = (y * g_kv_ref[...]).astype(BF16)
    c_tab, sa_tab, sb_tab = c_ref[...], sa_ref[...], sb_ref[...]
    q = _dot(xq, wq_ref[...]) * (SWA_HEAD_DIM ** -0.5 * LOG2E)
    for j, s in enumerate(_partial_rope(q, c_tab, sa_tab, sb_tab)):
        q_ref[:, j * LANES:(j + 1) * LANES] = s.astype(q_ref.dtype)
    kv = _dot(xkv, wkv_ref[...])
    kw = k_ref.shape[1]
    for j, s in enumerate(_partial_rope(kv[:, :kw], c_tab, sa_tab, sb_tab)):
        k_ref[:, j * LANES:(j + 1) * LANES] = s
    v_ref[...] = kv[:, kw:]


def _swa_in(h, g_q, g_kv, w_q, w_kv, c_tab, sa_tab, sb_tab, *, q_dtype):
    n, d = h.shape
    tm = min(TOKEN_TILE, n)
    qw = w_q.shape[1]
    kw = w_kv.shape[1] // 2
    pos_tiles = c_tab.shape[0] // tm
    row = lambda i: (i, 0)
    tab = lambda i: (i % pos_tiles, 0)
    vmem = (2 * _nbytes((tm, d), F32) + _nbytes(w_q.shape, BF16) + _nbytes(w_kv.shape, BF16)
            + 6 * _nbytes((tm, LANES), F32) + 2 * _nbytes((tm, qw), q_dtype) + 4 * _nbytes((tm, kw), F32)
            + 6 * _nbytes((tm, qw), F32))
    return pl.pallas_call(
        _swa_in_body,
        grid=(n // tm,),
        in_specs=[pl.BlockSpec((tm, d), row), _resident((1, d)), _resident((1, d)),
                  _resident(w_q.shape), _resident(w_kv.shape),
                  pl.BlockSpec((tm, LANES), tab), pl.BlockSpec((tm, LANES), tab), pl.BlockSpec((tm, LANES), tab)],
        out_specs=[pl.BlockSpec((tm, qw), row), pl.BlockSpec((tm, kw), row), pl.BlockSpec((tm, kw), row)],
        out_shape=[jax.ShapeDtypeStruct((n, qw), q_dtype), jax.ShapeDtypeStruct((n, kw), F32),
                   jax.ShapeDtypeStruct((n, kw), F32)],
        compiler_params=_params(("parallel",), vmem),
        name="swa_in",
    )(h, g_q, g_kv, w_q, w_kv, c_tab, sa_tab, sb_tab)


def _swa_in_t_body(h_ref, g_q_ref, g_kv_ref, wqt_ref, wk_ref, wvt_ref, wv_ref, cos_t_ref, sin_t_ref,
                   c_ref, sa_ref, sb_ref, sink2_ref, q2_ref, kn2_ref, vn2_ref, kc2_ref, vc2_ref,
                   qt_ref, k_ref, vt_ref, kwin_ref, vwin_ref, o2_ref, kw2_ref, vw2_ref,
                   *, tiles, seq2, group2, q_start):
    _attn_sample_body(sink2_ref, q2_ref, kn2_ref, vn2_ref, kc2_ref, vc2_ref, o2_ref, kw2_ref, vw2_ref,
                      seq=seq2, group=group2, q_start=q_start)
    tm = h_ref.shape[0]
    hd = SWA_HEAD_DIM
    half = ROT_DIM // 2
    win = kwin_ref.shape[0]
    parts = SWA_ROW_PARTS if tm % (SWA_ROW_PARTS * LANES) == 0 else 1
    rp = tm // parts
    assert rp >= win
    for p in range(parts):
        rows = slice(p * rp, (p + 1) * rp)
        y = _rms_rows(h_ref[rows, :])
        xq = (y * g_q_ref[...]).astype(BF16)
        xkv = (y * g_kv_ref[...]).astype(BF16)
        cos_t, sin_t = cos_t_ref[:, rows], sin_t_ref[:, rows]
        qt = _dot_nt(wqt_ref[...], xq) * (hd ** -0.5 * LOG2E)
        for hq in range(qt.shape[0] // hd):
            base = hq * hd
            x1 = qt[base:base + half]
            x2 = qt[base + half:base + 2 * half]
            rot = jnp.concatenate([x1 * cos_t - x2 * sin_t, x2 * cos_t + x1 * sin_t], axis=0)
            qt_ref[0, base:base + 2 * half, rows] = rot.astype(qt_ref.dtype)
            qt_ref[0, base + 2 * half:base + hd, rows] = qt[base + 2 * half:base + hd].astype(qt_ref.dtype)
        k_rot = _partial_rope(_dot(xkv, wk_ref[...]), c_ref[rows, :], sa_ref[rows, :], sb_ref[rows, :])
        for j, s in enumerate(k_rot):
            k_ref[rows, j * LANES:(j + 1) * LANES] = s.astype(k_ref.dtype)
        vt_ref[0, :, rows] = _dot_nt(wvt_ref[...], xkv).astype(vt_ref.dtype)
        if p == parts - 1:
            @pl.when(pl.program_id(0) % tiles == tiles - 1)
            def _():
                for j, s in enumerate(k_rot):
                    kwin_ref[:, j * LANES:(j + 1) * LANES] = s[rp - win:, :]
                vwin_ref[...] = _dot(xkv[rp - win:, :], wv_ref[...])


def _swa_in_t(h, g_q, g_kv, w_q, w_kv, cos_t, sin_t, c_tab, sa_tab, sb_tab, sinks, q2, kn2, vn2, kc2, vc2,
              *, batch, win, seq2, q_start):
    n, d = h.shape
    seq = n // batch
    tm = min(SWA_TOKEN_TILE, seq)
    tiles = seq // tm
    steps = n // tm
    batch2, win2, kw2 = kc2.shape
    assert batch2 % steps == 0
    nseq2 = batch2 // steps
    rows2 = nseq2 * seq2
    heads2 = q2.shape[1] // kw2
    group2 = heads2 // (kw2 // SWA_HEAD_DIM)
    sink_rows = jnp.repeat(sinks * LOG2E, rows2)[:, None]
    cache2 = lambda i: (i, 0, 0)
    qw = w_q.shape[1]
    kw = w_kv.shape[1] // 2
    wqt = w_q.T
    wk, wv = w_kv[:, :kw], w_kv[:, kw:]
    wvt = wv.T
    row = lambda i: (i, 0)
    tab = lambda i: (i % tiles, 0)
    tab_t = lambda i: (0, i % tiles)
    feat = lambda i: (i // tiles, 0, i % tiles)
    per_seq = lambda i: (i // tiles, 0)
    half = ROT_DIM // 2
    vmem = (2 * _nbytes((tm, d), F32) + 2 * _nbytes(w_q.shape, BF16) + 3 * _nbytes(w_kv.shape, BF16)
            + 8 * _nbytes((tm, LANES), F32) + 2 * _nbytes((tm, qw + 2 * kw), BF16) + 4 * _nbytes((win, kw), F32)
            + 4 * _nbytes((tm, qw), F32)
            + 8 * _nbytes((nseq2, win2, kw2), F32) + 4 * _nbytes((rows2, q2.shape[1]), BF16)
            + 16 * _nbytes((heads2 * rows2, win2 + kw2), F32))
    return pl.pallas_call(
        functools.partial(_swa_in_t_body, tiles=tiles, seq2=seq2, group2=group2, q_start=q_start),
        grid=(steps,),
        in_specs=[pl.BlockSpec((tm, d), row), _resident((1, d)), _resident((1, d)),
                  _resident(wqt.shape), _resident(wk.shape), _resident(wvt.shape), _resident(wv.shape),
                  pl.BlockSpec((half, tm), tab_t), pl.BlockSpec((half, tm), tab_t),
                  pl.BlockSpec((tm, LANES), tab), pl.BlockSpec((tm, LANES), tab), pl.BlockSpec((tm, LANES), tab),
                  _resident((heads2 * rows2, 1)), pl.BlockSpec((rows2, q2.shape[1]), row),
                  pl.BlockSpec((rows2, kw2), row), pl.BlockSpec((rows2, kw2), row),
                  pl.BlockSpec((nseq2, win2, kw2), cache2), pl.BlockSpec((nseq2, win2, kw2), cache2)],
        out_specs=[pl.BlockSpec((1, qw, tm), feat), pl.BlockSpec((tm, kw), row), pl.BlockSpec((1, kw, tm), feat),
                   pl.BlockSpec((win, kw), per_seq), pl.BlockSpec((win, kw), per_seq),
                   pl.BlockSpec((rows2, heads2 * SWA_HEAD_DIM), row),
                   pl.BlockSpec((nseq2, win2, kw2), cache2), pl.BlockSpec((nseq2, win2, kw2), cache2)],
        out_shape=[jax.ShapeDtypeStruct((batch, qw, seq), BF16), jax.ShapeDtypeStruct((n, kw), BF16),
                   jax.ShapeDtypeStruct((batch, kw, seq), BF16),
                   jax.ShapeDtypeStruct((batch * win, kw), F32), jax.ShapeDtypeStruct((batch * win, kw), F32),
                   jax.ShapeDtypeStruct((q2.shape[0], heads2 * SWA_HEAD_DIM), BF16 if rows2 % BF16_SUBLANES == 0 else F32),
                   jax.ShapeDtypeStruct((batch2, win2, kw2), F32), jax.ShapeDtypeStruct((batch2, win2, kw2), F32)],
        compiler_params=_params(("arbitrary",), vmem),
        name="swa_in_t",
    )(h, g_q, g_kv, wqt, wk, wvt, wv, cos_t, sin_t, c_tab, sa_tab, sb_tab, sink_rows, q2, kn2, vn2, kc2, vc2)


def _attn_prompt_body(sinks_ref, mask_ref, qt_ref, kp_ref, kc_ref, vtp_ref, vtc_ref, ot_ref, *, group):
    blk = kp_ref.shape[0]
    nblk = kc_ref.shape[0] // blk
    hd = SWA_HEAD_DIM
    kvh_n = kc_ref.shape[1] // hd
    cols = group * blk
    kj = lax.broadcasted_iota(jnp.int32, (blk, cols), 0)
    qi = lax.broadcasted_iota(jnp.int32, (blk, cols), 1) % blk
    own = kj <= qi
    lane_head = lax.broadcasted_iota(jnp.int32, (1, cols), 1) // blk
    ones_rows = jnp.ones((BF16_SUBLANES, blk), BF16)
    has_prev = pl.program_id(1) > 0

    def scores(j, kvh):
        tile, lo = divmod(kvh * hd, LANES)
        q4t = jnp.concatenate([qt_ref[0, (kvh * group + g) * hd:(kvh * group + g + 1) * hd, j * blk:(j + 1) * blk]
                               for g in range(group)], axis=1)
        rhs = jnp.concatenate([q4t if part * hd == lo else jnp.zeros_like(q4t) for part in range(LANES // hd)], axis=0)
        k_tile = slice(tile * LANES, (tile + 1) * LANES)
        k_prev = kp_ref[:, k_tile] if j == 0 else kc_ref[(j - 1) * blk:j * blk, k_tile]
        return _dot(kc_ref[j * blk:(j + 1) * blk, k_tile], rhs), _dot(k_prev, rhs)

    def finish(j, kvh, s_own, s_prev):
        if j == 0:
            s_prev = jnp.where(has_prev, s_prev, NEG)
        s = jnp.where(own, s_own, s_prev)
        sink = jnp.full((1, cols), sinks_ref[kvh * group] * LOG2E, F32)
        for g in range(1, group):
            sink = jnp.where(lane_head == g, sinks_ref[kvh * group + g] * LOG2E, sink)
        m = jnp.maximum(jnp.max(s, axis=0, keepdims=True), sink)
        e = jnp.exp2(s - m).astype(BF16)
        p_own = e * mask_ref[...]
        p = jnp.concatenate([p_own, e - p_own], axis=0)
        head_rows = slice(kvh * hd, (kvh + 1) * hd)
        vt_own = vtc_ref[0, head_rows, j * blk:(j + 1) * blk]
        vt_prev = vtp_ref[0, head_rows, :] if j == 0 else vtc_ref[0, head_rows, (j - 1) * blk:j * blk]
        vt = jnp.concatenate([jnp.concatenate([vt_own, ones_rows], axis=0),
                              jnp.concatenate([vt_prev, ones_rows], axis=0)], axis=1)
        acc = _dot(vt, p)
        denom = acc[hd:hd + 1, :] + jnp.exp2(sink - m)
        ot = acc[:hd] / denom
        for g in range(group):
            hq = kvh * group + g
            ot_ref[0, hq * hd:(hq + 1) * hd, j * blk:(j + 1) * blk] = ot[:, g * blk:(g + 1) * blk].astype(ot_ref.dtype)

    units = [(j, kvh) for j in range(nblk) for kvh in range(kvh_n)]
    queue = [scores(*unit) for unit in units[:ATTN_LOOKAHEAD]]
    for idx, unit in enumerate(units):
        if idx + ATTN_LOOKAHEAD < len(units):
            queue.append(scores(*units[idx + ATTN_LOOKAHEAD]))
        finish(*unit, *queue.pop(0))


def _attn_prompt(sinks, qt, k, vt):
    batch, qw, seq = qt.shape
    kw = k.shape[1]
    blk = WINDOW
    nblk = ATTN_BLOCKS_PER_STEP if seq % (ATTN_BLOCKS_PER_STEP * blk) == 0 else 1
    span = nblk * blk
    steps = seq // span
    group = qw // kw
    cols = group * blk
    own = (jnp.arange(blk)[:, None] <= (jnp.arange(cols) % blk)[None, :]).astype(BF16)
    cur_t = lambda b, i: (b, 0, i)
    prev_t = lambda b, i: (b, 0, jnp.maximum(i * nblk - 1, 0))
    cur = lambda b, i: (b * steps + i, 0)
    prev = lambda b, i: (b * steps * nblk + jnp.maximum(i * nblk - 1, 0), 0)
    vmem = (4 * _nbytes((qw, span), BF16) + 6 * _nbytes((span, kw), BF16) + 16 * _nbytes((blk, cols), F32))
    return pl.pallas_call(
        functools.partial(_attn_prompt_body, group=group),
        grid=(batch, steps),
        in_specs=[pl.BlockSpec(memory_space=pltpu.SMEM), _resident((blk, cols)), pl.BlockSpec((1, qw, span), cur_t),
                  pl.BlockSpec((blk, kw), prev), pl.BlockSpec((span, kw), cur),
                  pl.BlockSpec((1, kw, blk), prev_t), pl.BlockSpec((1, kw, span), cur_t)],
        out_specs=pl.BlockSpec((1, qw, span), cur_t),
        out_shape=jax.ShapeDtypeStruct((batch, qw, seq), BF16),
        compiler_params=_params(("parallel", "parallel"), vmem),
        name="attn_prompt",
    )(sinks, own, qt, k, k, vt, vt)


def _attn_sample_body(sink_ref, q_ref, kn_ref, vn_ref, kc_ref, vc_ref, o_ref, kw_ref, vw_ref, *, seq, group, q_start):
    rows = q_ref.shape[0]
    nseq = rows // seq
    win = kc_ref.shape[1]
    hd = SWA_HEAD_DIM
    kw = kn_ref.shape[1]
    heads = q_ref.shape[1] // kw
    srows = heads * rows
    lhs = jnp.concatenate([q_ref[:, hq * kw:(hq + 1) * kw] for hq in range(heads)], axis=0)
    kn = kn_ref[...]
    vn = vn_ref[...]
    sink = sink_ref[...]

    r_c = lax.broadcasted_iota(jnp.int32, (srows, win), 0) % rows
    c_c = lax.broadcasted_iota(jnp.int32, (srows, win), 1)
    seq_c = r_c // seq
    rel_c = (r_c % seq) + win - c_c
    ok_c = (rel_c >= 0) & (rel_c < WINDOW) & (q_start - win + c_c >= 0)
    s_c = _dot_nt(lhs, kc_ref[0].astype(BF16))
    for b in range(1, nseq):
        s_c = jnp.where(seq_c == b, _dot_nt(lhs, kc_ref[b].astype(BF16)), s_c)
    s_c = jnp.where(ok_c, s_c, NEG)

    r_n = lax.broadcasted_iota(jnp.int32, (srows, rows), 0) % rows
    c_n = lax.broadcasted_iota(jnp.int32, (srows, rows), 1)
    rel_n = (r_n % seq) - (c_n % seq)
    ok_n = (rel_n >= 0) & (rel_n < WINDOW) & ((r_n // seq) == (c_n // seq))
    s_n = jnp.where(ok_n, _dot_nt(lhs, kn.astype(BF16)), NEG)

    m = jnp.maximum(jnp.maximum(jnp.max(s_c, axis=-1, keepdims=True), jnp.max(s_n, axis=-1, keepdims=True)), sink)
    e_c = jnp.exp2(s_c - m)
    e_n = jnp.exp2(s_n - m)
    denom = jnp.sum(e_c, axis=-1, keepdims=True) + jnp.sum(e_n, axis=-1, keepdims=True) + jnp.exp2(sink - m)
    acc = _dot(e_n.astype(BF16), vn.astype(BF16))
    for b in range(nseq):
        acc = acc + _dot(jnp.where(seq_c == b, e_c, 0.0).astype(BF16), vc_ref[b].astype(BF16))
    o = acc / denom
    for hq in range(heads):
        kvh = hq // group
        o_ref[:, hq * hd:(hq + 1) * hd] = o[hq * rows:(hq + 1) * rows, kvh * hd:(kvh + 1) * hd].astype(o_ref.dtype)
    for b in range(nseq):
        kw_ref[b, 0:win - seq, :] = kc_ref[b, seq:win, :]
        kw_ref[b, win - seq:win, :] = kn[b * seq:(b + 1) * seq, :]
        vw_ref[b, 0:win - seq, :] = vc_ref[b, seq:win, :]
        vw_ref[b, win - seq:win, :] = vn[b * seq:(b + 1) * seq, :]


def _ret_rope_tables(pos, dk):
    inv = 1.0 / (RET_ROPE_THETA ** jnp.linspace(0.0, 1.0, dk // 2, dtype=F32))
    ang = pos[:, None] * inv[None, :]
    return jnp.cos(ang), jnp.sin(ang)


def _ret_key_scale(log_g, seq, n, dk):
    chunk = _ret_chunk(seq)
    tm = min(TOKEN_TILE, n)
    assert tm % chunk == 0
    left = (chunk - 1 - jnp.arange(tm) % chunk).astype(F32)
    per_head = jnp.exp(log_g[None, :] * left[:, None]) * dk ** -0.5
    return jnp.repeat(per_head, dk, axis=1)


def _swa_cos_sin(pos):
    half = ROT_DIM // 2
    inv = ROPE_THETA ** (-jnp.arange(half, dtype=F32) / half)
    ang = pos[:, None] * inv[None, :]
    return jnp.cos(ang), jnp.sin(ang)


def _swa_rope_tables(pos):
    half = ROT_DIM // 2
    cos, sin = _swa_cos_sin(pos)
    n = pos.shape[0]
    pad = jnp.zeros((n, SWA_HEAD_DIM - 2 * half), F32)
    c_head = jnp.concatenate([cos, cos, pad + 1.0], axis=1)
    sa_head = jnp.concatenate([-sin, jnp.zeros_like(sin), pad], axis=1)
    sb_head = jnp.concatenate([jnp.zeros_like(sin), sin, pad], axis=1)
    reps = LANES // SWA_HEAD_DIM
    return tuple(jnp.tile(t, (1, reps)) for t in (c_head, sa_head, sb_head))


def _tile_rows(tab, seq, n):
    tm = min(TOKEN_TILE, n)
    return tab if seq >= tm else jnp.tile(tab, (tm // seq, 1))


def kernel(x_prompt, x_sample, state_ret, cache_k_win, cache_v_win, ret_norm_pre, ret_w_in, ret_w_out, ret_norm_post, kv_norm, w_kv, swa_norm_pre, swa_w_q, swa_sinks, swa_w_o, swa_norm_post, ffn_norm_pre, ffn_w1, ffn_w2, ffn_norm_post):
    n_a = DEPTH // 2
    assert n_a == 1 and DEPTH == 2, "one retention layer followed by one sliding-window layer"
    d = x_prompt.shape[-1]
    heads = RET_HEADS
    dk = ret_w_out.shape[-1] // heads
    dv = ret_w_out.shape[-2] // heads
    kvh, hd = SWA_KV_HEADS, SWA_HEAD_DIM
    row2 = lambda g: g.reshape(1, d)
    log_g = jnp.log1p(-jnp.exp2(-5.0 - jnp.arange(heads, dtype=F32)))

    w_in = ret_w_in[0].astype(BF16)
    w_out = ret_w_out[0].astype(BF16)
    wq = swa_w_q[0].astype(BF16)
    wkv = w_kv.astype(BF16)
    wo = swa_w_o[0].astype(BF16)
    w1 = ffn_w1.astype(BF16)
    w2 = ffn_w2.astype(BF16)
    sinks = swa_sinks[0]
    q_heads = wq.shape[1] // hd
    on_kv_head = (jnp.arange(q_heads)[:, None] // (q_heads // kvh) == jnp.arange(kvh)[None, :]).astype(BF16)
    wq_wide = (wq.reshape(d, q_heads, 1, hd) * on_kv_head[None, :, :, None]).reshape(d, q_heads * kvh * hd)

    b_p, t_p, _ = x_prompt.shape
    b_s, t_s, _ = x_sample.shape
    n_p, n_s = b_p * t_p, b_s * t_s
    pos_p = jnp.arange(t_p, dtype=F32)
    pos_s = PAST_LEN + jnp.arange(t_s, dtype=F32)
    h_p = x_prompt.reshape(n_p, d)
    h_s = x_sample.reshape(n_s, d)

    def ret_tables(pos, t, n):
        cos, sin = (_tile_rows(tab, t, n) for tab in _ret_rope_tables(pos, dk))
        return cos, sin, _ret_key_scale(log_g, t, n, dk)

    q, kd, v, sg = _ret_in(h_s, row2(ret_norm_pre[0]), w_in, *ret_tables(pos_s, t_s, n_s),
                           heads=heads, dk=dk, dv=dv, out_dtype=F32)
    o_p, state_p, o_s, state_s = _ret_prompt(log_g, h_p, row2(ret_norm_pre[0]), w_in, *ret_tables(pos_p, t_p, n_p),
                                             q, kd, v, sg, state_ret[0],
                                             batch=b_p, heads=heads, dk=dk, dv=dv, seq2=t_s)
    h_p, h_s = _out_ffn(o_p, h_p, o_s, h_s, w_out, row2(ret_norm_post[0]), row2(ffn_norm_pre[0]), w1, w2,
                        row2(ffn_norm_post[0]), layer=0, o_layout="tokens", o2_layout="heads")

    w_p = min(WINDOW, t_p)
    w_s = cache_k_win.shape[1]
    tabs = tuple(_tile_rows(tab, t_s, n_s) for tab in _swa_rope_tables(pos_s))
    q_s, kn_s, vn_s = _swa_in(h_s, row2(swa_norm_pre[0]), row2(kv_norm), wq_wide, wkv, *tabs, q_dtype=BF16)
    cos, sin = _swa_cos_sin(pos_p)
    qt, k, vt, k_win_p, v_win_p, o_s, k_win_s, v_win_s = _swa_in_t(
        h_p, row2(swa_norm_pre[0]), row2(kv_norm), wq, wkv, cos.T, sin.T, *_swa_rope_tables(pos_p),
        sinks, q_s, kn_s, vn_s, cache_k_win.reshape(b_s, w_s, kvh * hd), cache_v_win.reshape(b_s, w_s, kvh * hd),
        batch=b_p, win=w_p, seq2=t_s, q_start=PAST_LEN)
    o_p = _attn_prompt(sinks, qt, k, vt)
    h_p, h_s = _out_ffn(o_p, h_p, o_s, h_s, wo, row2(swa_norm_post[0]), row2(ffn_norm_pre[1]), w1, w2,
                        row2(ffn_norm_post[1]), layer=1, o_layout="features", o2_layout="tokens")

    return (h_p.reshape(b_p, t_p, d), h_s.reshape(b_s, t_s, d), state_p[None], state_s[None],
            k_win_p.reshape(b_p, w_p, kvh, hd), v_win_p.reshape(b_p, w_p, kvh, hd),
            k_win_s.reshape(b_s, w_s, kvh, hd), v_win_s.reshape(b_s, w_s, kvh, hd))
```
